```python
import math
import jax, jax.numpy as jnp
from jax import lax
import numpy as np

D_MODEL = 1024
BATCH = 8
SEQ = 4096
DEPTH = 1

N_META = 16
SSD_HEAD_DIM = 64
SSD_INNER = D_MODEL
SSD_HEADS = SSD_INNER // SSD_HEAD_DIM
SSD_GROUPS = 2
SSD_STATE = 128
SSD_CONV = 4
CHUNK = 128
SB_HEAD_DIM = 64
SB_WIDTH = D_MODEL
SB_HEADS = SB_WIDTH // SB_HEAD_DIM
Q_BLOCK = 128
MIX_WIDTH = SSD_INNER + SB_WIDTH
D_FF = 256 * ((8 * D_MODEL // 3 + 255) // 256)
FFN_CONV = 3
EPS = 1e-6

XBC_WIDTH = SSD_INNER + 2 * SSD_GROUPS * SSD_STATE
OFF_Z = 0
OFF_XBC = OFF_Z + SSD_INNER
OFF_DT = OFF_XBC + XBC_WIDTH
OFF_Q = OFF_DT + SSD_HEADS
OFF_K = OFF_Q + SB_WIDTH
OFF_V = OFF_K + SB_WIDTH
IN_COLS = OFF_V + SB_WIDTH

kernel_name = "hymba_ssd_stickbreaking_convffn_layer"


def rms_norm(x, g):
    x32 = x.astype(jnp.float32)
    y = x32 * lax.rsqrt(jnp.mean(x32 * x32, axis=-1, keepdims=True) + EPS)
    return (y * g.astype(jnp.float32)).astype(x.dtype)


def causal_dwconv(x, w, b):
    K = w.shape[0]
    L = x.shape[1]
    xp = jnp.pad(x, ((0, 0), (K - 1, 0), (0, 0)))
    y = b
    for k in range(K):
        y = y + xp[:, k:k + L] * w[k]
    return y


def ssd_mixer(z, xbc, dt_raw, conv_w, conv_b, dt_bias, a_log, d_skip, norm_g):
    out_dtype = z.dtype
    Bsz, L, _ = xbc.shape
    H, P, G, N = SSD_HEADS, SSD_HEAD_DIM, SSD_GROUPS, SSD_STATE
    J = H // G
    f32 = jnp.float32
    xbc = jax.nn.silu(causal_dwconv(xbc, conv_w, conv_b)).astype(f32)
    xs = xbc[..., :SSD_INNER].reshape(Bsz, L, H, P)
    Bm = xbc[..., SSD_INNER:SSD_INNER + G * N].reshape(Bsz, L, G, N)
    Cm = xbc[..., SSD_INNER + G * N:].reshape(Bsz, L, G, N)
    dt = jax.nn.softplus(dt_raw.astype(f32) + dt_bias.astype(f32))
    A = -jnp.exp(a_log.astype(f32))

    pad = CHUNK - N_META
    Lp = L + pad
    nc = Lp // CHUNK

    def front_pad(t):
        return jnp.pad(t, ((0, 0), (pad, 0)) + ((0, 0),) * (t.ndim - 2))

    Xdt = front_pad(xs * dt[..., None]).reshape(Bsz, nc, CHUNK, G, J, P)
    Adt = front_pad(dt * A).reshape(Bsz, nc, CHUNK, G, J).transpose(0, 3, 4, 1, 2)
    Bc = front_pad(Bm).reshape(Bsz, nc, CHUNK, G, N)
    Cc = front_pad(Cm).reshape(Bsz, nc, CHUNK, G, N)

    Acs = jnp.cumsum(Adt, axis=-1)
    causal = jnp.tril(jnp.ones((CHUNK, CHUNK), dtype=bool))
    seg = Acs[..., :, None] - Acs[..., None, :]
    Ldec = jnp.where(causal, jnp.exp(jnp.where(causal, seg, 0.0)), 0.0)

    CB = jnp.einsum('bclgn,bcsgn->bgcls', Cc, Bc)
    y_diag = jnp.einsum('bgcls,bgjcls,bcsgjp->bclgjp', CB, Ldec, Xdt)

    decay_states = jnp.exp(Acs[..., -1:] - Acs)
    states = jnp.einsum('bclgn,bgjcl,bclgjp->bcgjpn', Bc, decay_states, Xdt)
    chunk_decay = jnp.exp(Acs[..., -1])

    def step(carry, inp):
        st, dec = inp
        return carry * dec[..., None, None] + st, carry

    init = jnp.zeros((Bsz, G, J, P, N), f32)
    _, prev = lax.scan(step, init, (states.transpose(1, 0, 2, 3, 4, 5),
                                    chunk_decay.transpose(3, 0, 1, 2)))
    prev = prev.transpose(1, 0, 2, 3, 4, 5)

    y_off = jnp.einsum('bclgn,bcgjpn,bgjcl->bclgjp', Cc, prev, jnp.exp(Acs))

    y = (y_diag + y_off).reshape(Bsz, Lp, H, P)[:, pad:]
    y = y + xs * d_skip.astype(f32)[:, None]
    y = y.reshape(Bsz, L, SSD_INNER) * jax.nn.silu(z.astype(f32))
    return rms_norm(y, norm_g).astype(out_dtype)


def stick_breaking_attention(q, k, v):
    out_dtype = q.dtype
    Bsz, L, H, D = q.shape
    f32 = jnp.float32
    scale = 1.0 / math.sqrt(D)
    pad = Q_BLOCK - N_META
    Lp = L + pad
    nb = Lp // Q_BLOCK
    padw = ((0, 0), (pad, 0), (0, 0), (0, 0))
    qp = jnp.pad(q.astype(f32), padw)
    kp = jnp.pad(k.astype(f32), padw)
    vp = jnp.pad(v.astype(f32), padw)
    qb = qp.reshape(Bsz, nb, Q_BLOCK, H, D).transpose(1, 0, 2, 3, 4)
    key_pos = jnp.arange(Lp)

    def block(args):
        i, qi = args
        q_pos = i * Q_BLOCK + jnp.arange(Q_BLOCK)
        valid = (key_pos[None, :] < q_pos[:, None]) & (key_pos[None, :] >= pad)
        zlog = jnp.einsum('bqhd,bkhd->bhqk', qi, kp) * scale
        log_beta = jax.nn.log_sigmoid(zlog)
        log_keep = jnp.where(valid, log_beta - zlog, 0.0)
        after = lax.cumsum(log_keep, axis=3, reverse=True) - log_keep
        w = jnp.where(valid, jnp.exp(log_beta + after), 0.0)
        return jnp.einsum('bhqk,bkhd->bqhd', w, vp)

    o = lax.map(block, (jnp.arange(nb), qb))
    o = o.transpose(1, 0, 2, 3, 4).reshape(Bsz, Lp, H, D)[:, pad:]
    return o.astype(out_dtype)


def setup_inputs(seed: int = 0) -> dict:
    key = jax.random.key(seed)
    ks = jax.random.split(key, 20)
    f32 = jnp.float32

    def gain(k, n):
        return 1.0 + 0.05 * jax.random.normal(k, (DEPTH, n), f32)

    dt0 = jnp.exp(jax.random.uniform(ks[5], (DEPTH, SSD_HEADS), f32,
                                     math.log(1e-3), math.log(1e-1)))
    dt_bias = dt0 + jnp.log(-jnp.expm1(-dt0))
    return {
        "x": jax.random.normal(ks[0], (BATCH, SEQ, D_MODEL), f32),
        "meta_tokens": jax.random.normal(ks[1], (N_META, D_MODEL), f32),
        "mix_pre_g": gain(ks[2], D_MODEL),
        "w_in": jax.random.normal(ks[3], (DEPTH, D_MODEL, IN_COLS), f32) * D_MODEL ** -0.5,
        "ssd_conv_w": jax.random.normal(ks[4], (DEPTH, SSD_CONV, XBC_WIDTH), f32) * SSD_CONV ** -0.5,
        "ssd_conv_b": 0.01 * jax.random.normal(ks[6], (DEPTH, XBC_WIDTH), f32),
        "ssd_dt_bias": dt_bias,
        "ssd_a_log": jnp.log(jax.random.uniform(ks[7], (DEPTH, SSD_HEADS), f32, 1.0, 16.0)),
        "ssd_d": 1.0 + 0.1 * jax.random.normal(ks[8], (DEPTH, SSD_HEADS), f32),
        "ssd_norm_g": gain(ks[9], SSD_INNER),
        "sb_norm_g": gain(ks[10], SB_WIDTH),
        "w_out": jax.random.normal(ks[11], (DEPTH, MIX_WIDTH, D_MODEL), f32) * MIX_WIDTH ** -0.5,
        "mix_post_g": gain(ks[12], D_MODEL),
        "ffn_pre_g": gain(ks[13], D_MODEL),
        "w_up": jax.random.normal(ks[14], (DEPTH, D_MODEL, 2 * D_FF), f32) * D_MODEL ** -0.5,
        "ffn_conv_w": jax.random.normal(ks[15], (DEPTH, FFN_CONV, D_FF), f32) * FFN_CONV ** -0.5,
        "ffn_conv_b": 0.01 * jax.random.normal(ks[16], (DEPTH, D_FF), f32),
        "w_down": jax.random.normal(ks[17], (DEPTH, D_FF, D_MODEL), f32) * D_FF ** -0.5,
        "ffn_post_g": gain(ks[18], D_MODEL),
    }


def reference(x, meta_tokens, mix_pre_g, w_in, ssd_conv_w, ssd_conv_b, ssd_dt_bias,
              ssd_a_log, ssd_d, ssd_norm_g, sb_norm_g, w_out, mix_post_g, ffn_pre_g,
              w_up, ffn_conv_w, ffn_conv_b, w_down, ffn_post_g):
    Bsz = x.shape[0]
    meta = jnp.broadcast_to(meta_tokens.astype(x.dtype)[None], (Bsz, N_META, x.shape[-1]))
    h = jnp.concatenate([meta, x], axis=1)
    L = h.shape[1]
    for l in range(DEPTH):
        xn = rms_norm(h, mix_pre_g[l])
        proj = xn @ w_in[l]
        z = proj[..., OFF_Z:OFF_XBC]
        xbc = proj[..., OFF_XBC:OFF_DT]
        dt_raw = proj[..., OFF_DT:OFF_Q]
        q = proj[..., OFF_Q:OFF_K].reshape(Bsz, L, SB_HEADS, SB_HEAD_DIM)
        k = proj[..., OFF_K:OFF_V].reshape(Bsz, L, SB_HEADS, SB_HEAD_DIM)
        v = proj[..., OFF_V:IN_COLS].reshape(Bsz, L, SB_HEADS, SB_HEAD_DIM)
        y_ssd = ssd_mixer(z, xbc, dt_raw, ssd_conv_w[l], ssd_conv_b[l], ssd_dt_bias[l],
                          ssd_a_log[l], ssd_d[l], ssd_norm_g[l])
        y_sb = rms_norm(stick_breaking_attention(q, k, v).reshape(Bsz, L, SB_WIDTH), sb_norm_g[l])
        mix = jnp.concatenate([y_ssd, y_sb], axis=-1) @ w_out[l]
        h = h + rms_norm(mix, mix_post_g[l])
        xn = rms_norm(h, ffn_pre_g[l])
        gu = xn @ w_up[l]
        g = causal_dwconv(gu[..., :D_FF], ffn_conv_w[l], ffn_conv_b[l])
        f = (jax.nn.gelu(g, approximate=True) * gu[..., D_FF:]) @ w_down[l]
        h = h + rms_norm(f, ffn_post_g[l])
    return h[:, N_META:]
```

```python
import functools
import math

import jax
import jax.numpy as jnp
from jax import lax
from jax.experimental import pallas as pl
from jax.experimental.pallas import tpu as pltpu

F32 = jnp.float32
BF16 = jnp.bfloat16

D_MODEL = 1024
N_META = 16
BLK = 128
PAD = BLK - N_META
SSD_HEADS = 16
SSD_HEAD_DIM = 64
SSD_GROUPS = 2
SSD_STATE = 128
SSD_INNER = 1024
SSD_CONV = 4
XBC_WIDTH = SSD_INNER + 2 * SSD_GROUPS * SSD_STATE
SB_WIDTH = 1024
SB_HEAD_DIM = 64
D_FF = 2816
FFN_CONV = 3
EPS = 1e-6
DT_LANES = 128
GROUP_COLS = SSD_INNER // SSD_GROUPS

V7X_VMEM_LIMIT = 56 * 1024 * 1024


def _rms(x, g):
    return x * lax.rsqrt(jnp.mean(x * x, axis=-1, keepdims=True) + EPS) * g


def _const_spec(shape):
    nd = len(shape)
    return pl.BlockSpec(shape, lambda *_: (0,) * nd, pipeline_mode=pl.Buffered(1))


_PROJ_SPLITS = (("z", SSD_INNER), ("xbc", XBC_WIDTH), ("q", SB_WIDTH), ("k", SB_WIDTH), ("v", SB_WIDTH))


def _inproj_kernel(h_ref, g_ref, w_ref, wdt_ref, z_ref, xbc_ref, q_ref, k_ref, v_ref, dt_ref):
    xn = _rms(h_ref[...], g_ref[...]).astype(BF16)
    lo = 0
    for ref, (_, width) in zip((z_ref, xbc_ref, q_ref, k_ref, v_ref), _PROJ_SPLITS):
        ref[...] = jnp.dot(xn, w_ref[:, lo:lo + width], preferred_element_type=F32).astype(BF16)
        lo += width
    dt_ref[...] = jnp.dot(xn, wdt_ref[...], preferred_element_type=F32)


def _inproj(h2d, g, w_main, w_dt, tm):
    m = h2d.shape[0]
    n_main = w_main.shape[1]
    row = lambda i: (i, 0)
    outs = [jax.ShapeDtypeStruct((m, width), BF16) for _, width in _PROJ_SPLITS]
    outs.append(jax.ShapeDtypeStruct((m, DT_LANES), F32))
    out_specs = [pl.BlockSpec((tm, width), row) for _, width in _PROJ_SPLITS]
    out_specs.append(pl.BlockSpec((tm, DT_LANES), row))
    return pl.pallas_call(
        _inproj_kernel,
        grid=(m // tm,),
        in_specs=[
            pl.BlockSpec((tm, D_MODEL), row),
            _const_spec((1, D_MODEL)),
            _const_spec((D_MODEL, n_main)),
            _const_spec((D_MODEL, DT_LANES)),
        ],
        out_specs=out_specs,
        out_shape=outs,
        compiler_params=pltpu.CompilerParams(
            dimension_semantics=("arbitrary",), vmem_limit_bytes=V7X_VMEM_LIMIT),
        name="inproj",
    )(h2d, g, w_main, w_dt)


def _ssd_kernel(xbc_ref, z_ref, dt_ref, cw_ref, cb_ref, dtb_ref, alog_ref, dexp_ref, g_ref, e_ref,
                out_ref, cbuf, state):
    c = pl.program_id(1)

    @pl.when(c == 0)
    def _():
        cbuf[0:8, :] = jnp.zeros((8, XBC_WIDTH), F32)
        state[...] = jnp.zeros_like(state)

    x_raw = xbc_ref[0].astype(F32)
    cbuf[8:8 + BLK, :] = x_raw
    cw = cw_ref[...]
    acc = cb_ref[...] + cw[3:4, :] * x_raw
    for tap in range(SSD_CONV - 1):
        acc = acc + cw[tap:tap + 1, :] * cbuf[5 + tap:5 + tap + BLK, :]
    cbuf[0:8, :] = x_raw[BLK - 8:, :]
    xc = acc * jax.nn.sigmoid(acc)

    row = lax.broadcasted_iota(jnp.int32, (BLK, 1), 0)
    real = jnp.logical_or(c > 0, row >= PAD)
    xc = jnp.where(real, xc, 0.0)
    xs = xc[:, :SSD_INNER]

    lane = lax.broadcasted_iota(jnp.int32, (BLK, DT_LANES), 1)
    dt = jax.nn.softplus(dt_ref[0] + dtb_ref[...])
    dt = jnp.where(jnp.logical_and(real, lane < SSD_HEADS), dt, 0.0)
    a_neg = -jnp.exp(alog_ref[...])
    adt = dt * a_neg

    ri = lax.broadcasted_iota(jnp.int32, (BLK, BLK), 0)
    ci = lax.broadcasted_iota(jnp.int32, (BLK, BLK), 1)
    causal = ri >= ci
    tri = jnp.where(causal, 1.0, 0.0).astype(F32)
    hi = lax.Precision.HIGHEST
    acs = jnp.dot(tri, adt, precision=hi, preferred_element_type=F32)
    acs_t = acs.T

    e = e_ref[...]
    dt_exp = jnp.dot(dt, e, precision=hi, preferred_element_type=F32)
    acs_exp = jnp.dot(acs, e, precision=hi, preferred_element_type=F32)
    last = acs_exp[BLK - 1:BLK, :]
    xdt = xs * dt_exp
    xdec = (xdt * jnp.exp(last - acs_exp)).astype(BF16)
    xdt_b = xdt.astype(BF16)
    off_scale = jnp.exp(acs_exp)
    chunk_decay = jnp.exp(last)

    half = lax.broadcasted_iota(jnp.int32, (BLK, BLK), 1) < SSD_HEAD_DIM
    y_cols = []
    for grp in range(SSD_GROUPS):
        b_g = xc[:, SSD_INNER + grp * SSD_STATE:SSD_INNER + (grp + 1) * SSD_STATE]
        c_g = xc[:, SSD_INNER + (SSD_GROUPS + grp) * SSD_STATE:
                 SSD_INNER + (SSD_GROUPS + grp + 1) * SSD_STATE]
        b_gb = b_g.astype(BF16)
        c_gb = c_g.astype(BF16)
        cb_mat = lax.dot_general(c_gb, b_gb, (((1,), (1,)), ((), ())), preferred_element_type=F32)
        gcols = slice(grp * GROUP_COLS, (grp + 1) * GROUP_COLS)
        s_prev = state[:, gcols]
        y_off = jnp.dot(c_gb, s_prev.astype(BF16), preferred_element_type=F32) * off_scale[:, gcols]
        for pair in range(GROUP_COLS // BLK):
            h0 = (grp * GROUP_COLS + pair * BLK) // SSD_HEAD_DIM
            cols = slice(grp * GROUP_COLS + pair * BLK, grp * GROUP_COLS + (pair + 1) * BLK)
            x_pair = xdt_b[:, cols]
            parts = []
            for hh in (h0, h0 + 1):
                seg = acs[:, hh:hh + 1] - acs_t[hh:hh + 1, :]
                ldec = jnp.exp(jnp.where(causal, seg, -1e30))
                m = (cb_mat * ldec).astype(BF16)
                parts.append(jnp.dot(m, x_pair, preferred_element_type=F32))
            y_cols.append(jnp.where(half, parts[0], parts[1])
                          + y_off[:, pair * BLK:(pair + 1) * BLK])
        new = jnp.dot(b_g.T.astype(BF16), xdec[:, gcols], preferred_element_type=F32)
        state[:, gcols] = s_prev * chunk_decay[:, gcols] + new

    y = jnp.concatenate(y_cols, axis=1) + xs * dexp_ref[...]
    zf = z_ref[0].astype(F32)
    y = y * (zf * jax.nn.sigmoid(zf))
    out_ref[0] = _rms(y, g_ref[...]).astype(BF16)


def _ssd(xbc, z, dt_raw, conv_w, conv_b, dt_bias, a_log, d_exp, norm_g, expand):
    bsz, lp, _ = xbc.shape
    nblk = lp // BLK
    blk = lambda b, c: (b, (c + nblk - 1) % nblk, 0)
    return pl.pallas_call(
        _ssd_kernel,
        grid=(bsz, nblk),
        in_specs=[
            pl.BlockSpec((1, BLK, XBC_WIDTH), blk),
            pl.BlockSpec((1, BLK, SSD_INNER), blk),
            pl.BlockSpec((1, BLK, DT_LANES), blk),
            _const_spec((SSD_CONV, XBC_WIDTH)),
            _const_spec((1, XBC_WIDTH)),
            _const_spec((1, DT_LANES)),
            _const_spec((1, DT_LANES)),
            _const_spec((1, SSD_INNER)),
            _const_spec((1, SSD_INNER)),
            _const_spec((DT_LANES, SSD_INNER)),
        ],
        out_specs=pl.BlockSpec((1, BLK, SSD_INNER), blk),
        out_shape=jax.ShapeDtypeStruct((bsz, lp, SSD_INNER), BF16),
        scratch_shapes=[
            pltpu.VMEM((BLK + 8, XBC_WIDTH), F32),
            pltpu.VMEM((SSD_STATE, SSD_INNER), F32),
        ],
        compiler_params=pltpu.CompilerParams(
            dimension_semantics=("arbitrary", "arbitrary"), vmem_limit_bytes=V7X_VMEM_LIMIT),
        name="ssd",
    )(xbc, z, dt_raw, conv_w, conv_b, dt_bias, a_log, d_exp, norm_g, expand)


def _sb_kernel(q_ref, k_ref, v_ref, tt_ref, o_ref, *, nblk):
    qi = pl.program_id(2)
    q = q_ref[0]
    lane = lax.broadcasted_iota(jnp.int32, (BLK, BLK), 1)
    sub = lax.broadcasted_iota(jnp.int32, (BLK, BLK), 0)
    first = lane < SB_HEAD_DIM
    scale = 1.0 / math.sqrt(SB_HEAD_DIM)
    zero = jnp.zeros_like(q)
    q_heads = (jnp.where(first, q, zero) * scale, jnp.where(first, zero, q) * scale)
    tt = tt_ref[...]

    def body(it, carry):
        j = qi - it
        off = pl.multiple_of(((j + nblk - 1) % nblk) * BLK, BLK)
        kb = k_ref[0, pl.ds(off, BLK), :]
        vb = v_ref[0, pl.ds(off, BLK), :]
        kpos = j * BLK + lane
        valid = jnp.logical_and(kpos < qi * BLK + sub, kpos >= PAD)
        out = []
        for hh in range(2):
            run, acc = carry[2 * hh], carry[2 * hh + 1]
            s = lax.dot_general(q_heads[hh], kb, (((1,), (1,)), ((), ())), preferred_element_type=F32)
            sp = jnp.log(1.0 + jnp.exp(-jnp.abs(s)))
            log_beta = jnp.minimum(s, 0.0) - sp
            log_keep = jnp.where(valid, log_beta - s, 0.0)
            lk_hi = log_keep.astype(BF16)
            lk_lo = (log_keep - lk_hi.astype(F32)).astype(BF16)
            sums = jnp.dot(jnp.concatenate([lk_hi, lk_lo], axis=1), tt, preferred_element_type=F32)
            after = sums[:, :BLK] + run
            w = jnp.where(valid, jnp.exp(log_beta + after), 0.0)
            acc = acc + jnp.dot(w.astype(BF16), vb, preferred_element_type=F32)
            run = run + sums[:, BLK:]
            out += [run, acc]
        return tuple(out)

    z = jnp.zeros((BLK, BLK), F32)
    res = lax.fori_loop(0, qi + 1, body, (z, z, z, z))
    o_ref[0] = jnp.where(first, res[1], res[3]).astype(BF16)


def _sb_attention(q, k, v, tt):
    bsz, lp, width = q.shape
    nblk = lp // BLK
    npair = width // BLK
    qblk = lambda b, p, i: (b, (i + nblk - 1) % nblk, p)
    kv = lambda b, p, i: (b, 0, p)
    return pl.pallas_call(
        functools.partial(_sb_kernel, nblk=nblk),
        grid=(bsz, npair, nblk),
        in_specs=[
            pl.BlockSpec((1, BLK, BLK), qblk),
            pl.BlockSpec((1, lp, BLK), kv),
            pl.BlockSpec((1, lp, BLK), kv),
            _const_spec((2 * BLK, 2 * BLK)),
        ],
        out_specs=pl.BlockSpec((1, BLK, BLK), qblk),
        out_shape=jax.ShapeDtypeStruct((bsz, lp, width), BF16),
        compiler_params=pltpu.CompilerParams(
            dimension_semantics=("arbitrary", "arbitrary", "arbitrary"),
            vmem_limit_bytes=V7X_VMEM_LIMIT),
        name="sb_attention",
    )(q, k, v, tt)


def _outproj_kernel(ys_ref, o_ref, h_ref, wa_ref, wb_ref, gsb_ref, gpost_ref, gffn_ref,
                    h1_ref, xn2_ref):
    osb = _rms(o_ref[...].astype(F32), gsb_ref[...]).astype(BF16)
    mix = jnp.dot(ys_ref[...], wa_ref[...], preferred_element_type=F32)
    mix = mix + jnp.dot(osb, wb_ref[...], preferred_element_type=F32)
    h1 = h_ref[...] + _rms(mix, gpost_ref[...])
    h1_ref[...] = h1
    xn2_ref[...] = _rms(h1, gffn_ref[...]).astype(BF16)


def _outproj(y_ssd, o_sb, h2d, w_a, w_b, g_sb, g_post, g_ffn, tm):
    m = h2d.shape[0]
    row = lambda i: (i, 0)
    return pl.pallas_call(
        _outproj_kernel,
        grid=(m // tm,),
        in_specs=[
            pl.BlockSpec((tm, SSD_INNER), row),
            pl.BlockSpec((tm, SB_WIDTH), row),
            pl.BlockSpec((tm, D_MODEL), row),
            _const_spec((SSD_INNER, D_MODEL)),
            _const_spec((SB_WIDTH, D_MODEL)),
            _const_spec((1, SB_WIDTH)),
            _const_spec((1, D_MODEL)),
            _const_spec((1, D_MODEL)),
        ],
        out_specs=[pl.BlockSpec((tm, D_MODEL), row), pl.BlockSpec((tm, D_MODEL), row)],
        out_shape=[jax.ShapeDtypeStruct((m, D_MODEL), F32), jax.ShapeDtypeStruct((m, D_MODEL), BF16)],
        compiler_params=pltpu.CompilerParams(
            dimension_semantics=("arbitrary",), vmem_limit_bytes=V7X_VMEM_LIMIT),
        name="outproj",
    )(y_ssd, o_sb, h2d, w_a, w_b, g_sb, g_post, g_ffn)


def _gelu_tanh(x):
    return 0.5 * x * (1.0 + jnp.tanh(math.sqrt(2.0 / math.pi) * (x + 0.044715 * (x * x * x))))


def _ffn_kernel(xn_ref, xm_ref, h1_ref, wup_ref, cw_ref, cb_ref, wd_ref, g_ref, out_ref,
                gbuf, halo, *, tm, tf):
    i = pl.program_id(1)
    xn = xn_ref[0]
    acc = jnp.zeros((tm, D_MODEL), F32)
    for cidx in range(D_FF // tf):
        cols = slice(cidx * tf, (cidx + 1) * tf)
        w_g = wup_ref[:, cols]
        gate = jnp.dot(xn, w_g, preferred_element_type=F32)
        up = jnp.dot(xn, wup_ref[:, D_FF + cidx * tf:D_FF + (cidx + 1) * tf],
                     preferred_element_type=F32)

        @pl.when(i == 0)
        def _():
            gm = jnp.dot(xm_ref[0], w_g, preferred_element_type=F32)
            halo[:, cols] = gm[N_META - 8:, :]

        gbuf[0:8, :] = halo[:, cols]
        gbuf[8:8 + tm, :] = gate
        halo[:, cols] = gate[tm - 8:, :]
        cw = cw_ref[:, cols]
        conv = cb_ref[:, cols] + cw[2:3, :] * gate
        conv = conv + cw[1:2, :] * gbuf[7:7 + tm, :] + cw[0:1, :] * gbuf[6:6 + tm, :]
        act = (_gelu_tanh(conv) * up).astype(BF16)
        acc = acc + jnp.dot(act, wd_ref[cols, :], preferred_element_type=F32)
    out_ref[0] = h1_ref[0] + _rms(acc, g_ref[...])


def _ffn(xn2, h1, w_up, conv_w, conv_b, w_down, g_post, seq, tm, tf):
    bsz, lp, _ = xn2.shape
    xblk = lambda b, i: (b, i, 0)
    meta = lambda b, i: (b, (lp - N_META) // N_META, 0)
    return pl.pallas_call(
        functools.partial(_ffn_kernel, tm=tm, tf=tf),
        grid=(bsz, seq // tm),
        in_specs=[
            pl.BlockSpec((1, tm, D_MODEL), xblk),
            pl.BlockSpec((1, N_META, D_MODEL), meta),
            pl.BlockSpec((1, tm, D_MODEL), xblk),
            _const_spec((D_MODEL, 2 * D_FF)),
            _const_spec((FFN_CONV, D_FF)),
            _const_spec((1, D_FF)),
            _const_spec((D_FF, D_MODEL)),
            _const_spec((1, D_MODEL)),
        ],
        out_specs=pl.BlockSpec((1, tm, D_MODEL), xblk),
        out_shape=jax.ShapeDtypeStruct((bsz, seq, D_MODEL), F32),
        scratch_shapes=[
            pltpu.VMEM((tm + 8, tf), F32),
            pltpu.VMEM((8, D_FF), F32),
        ],
        compiler_params=pltpu.CompilerParams(
            dimension_semantics=("arbitrary", "arbitrary"), vmem_limit_bytes=V7X_VMEM_LIMIT),
        name="ffn",
    )(xn2, xn2, h1, w_up, conv_w, conv_b, w_down, g_post)


def _pad_lanes(v, n):
    return jnp.pad(v, ((0, 0), (0, n - v.shape[-1])))


def kernel(x, meta_tokens, mix_pre_g, w_in, ssd_conv_w, ssd_conv_b, ssd_dt_bias, ssd_a_log, ssd_d,
           ssd_norm_g, sb_norm_g, w_out, mix_post_g, ffn_pre_g, w_up, ffn_conv_w, ffn_conv_b,
           w_down, ffn_post_g):
    bsz, seq, d = x.shape
    lp = seq + BLK
    depth = w_in.shape[0]
    assert depth == 1 and d == D_MODEL and seq % BLK == 0

    meta = jnp.broadcast_to(meta_tokens.astype(x.dtype)[None], (bsz, N_META, d))
    h = jnp.concatenate([x, jnp.zeros((bsz, PAD, d), x.dtype), meta], axis=1)
    h2d = h.reshape(bsz * lp, d)

    l = 0
    off_xbc = SSD_INNER
    off_dt = off_xbc + XBC_WIDTH
    off_q = off_dt + SSD_HEADS
    w = w_in[l]
    w_main = jnp.concatenate([w[:, :off_dt], w[:, off_q:]], axis=1).astype(BF16)
    w_dt = _pad_lanes(w[:, off_dt:off_q], DT_LANES).astype(BF16)

    tm_rows = lp // 8
    z, xbc, q, k, v, dt_raw = _inproj(h2d, mix_pre_g[l][None], w_main, w_dt, tm_rows)
    to3 = lambda a: a.reshape(bsz, lp, a.shape[-1])

    head_of_col = jnp.arange(SSD_INNER) // SSD_HEAD_DIM
    expand = (jnp.arange(DT_LANES)[:, None] == head_of_col[None, :]).astype(F32)
    d_exp = jnp.repeat(ssd_d[l].astype(F32), SSD_HEAD_DIM)[None]
    y_ssd = _ssd(to3(xbc), to3(z), to3(dt_raw), ssd_conv_w[l], ssd_conv_b[l][None],
                 _pad_lanes(ssd_dt_bias[l][None], DT_LANES), _pad_lanes(ssd_a_log[l][None], DT_LANES),
                 d_exp, ssd_norm_g[l][None], expand)

    kk = jnp.arange(BLK)
    suffix = (kk[:, None] > kk[None, :]).astype(BF16)
    half_tt = jnp.concatenate([suffix, jnp.ones((BLK, BLK), BF16)], axis=1)
    tt = jnp.concatenate([half_tt, half_tt], axis=0)
    o_sb = _sb_attention(to3(q), to3(k), to3(v), tt)

    wo = w_out[l].astype(BF16)
    h1, xn2 = _outproj(y_ssd.reshape(bsz * lp, SSD_INNER), o_sb.reshape(bsz * lp, SB_WIDTH), h2d,
                       wo[:SSD_INNER], wo[SSD_INNER:], sb_norm_g[l][None], mix_post_g[l][None],
                       ffn_pre_g[l][None], tm_rows)

    return _ffn(to3(xn2), to3(h1), w_up[l].astype(BF16), ffn_conv_w[l], ffn_conv_b[l][None],
                w_down[l].astype(BF16), ffn_post_g[l][None], seq, 512, 256)
```

```python
import functools
import math

import jax
import jax.numpy as jnp
from jax import lax
from jax.experimental import pallas as pl
from jax.experimental.pallas import tpu as pltpu

F32 = jnp.float32
BF16 = jnp.bfloat16

D_MODEL = 1024
N_META = 16
BLK = 128
PAD = BLK - N_META
SSD_HEADS = 16
SSD_HEAD_DIM = 64
SSD_GROUPS = 2
SSD_STATE = 128
SSD_INNER = 1024
SSD_CONV = 4
XBC_WIDTH = SSD_INNER + 2 * SSD_GROUPS * SSD_STATE
SB_WIDTH = 1024
SB_HEAD_DIM = 64
D_FF = 2816
FFN_CONV = 3
EPS = 1e-6
DT_LANES = 128
GROUP_COLS = SSD_INNER // SSD_GROUPS

V7X_VMEM_LIMIT = 56 * 1024 * 1024


def _rms(x, g):
    return x * lax.rsqrt(jnp.mean(x * x, axis=-1, keepdims=True) + EPS) * g


def _const_spec(shape):
    nd = len(shape)
    return pl.BlockSpec(shape, lambda *_: (0,) * nd, pipeline_mode=pl.Buffered(1))


_PROJ_SPLITS = (("z", SSD_INNER), ("xbc", XBC_WIDTH), ("q", SB_WIDTH), ("k", SB_WIDTH), ("v", SB_WIDTH))


def _inproj_kernel(h_ref, g_ref, w_ref, wdt_ref, z_ref, xbc_ref, q_ref, k_ref, v_ref, dt_ref):
    xn = _rms(h_ref[...], g_ref[...]).astype(BF16)
    lo = 0
    for ref, (_, width) in zip((z_ref, xbc_ref, q_ref, k_ref, v_ref), _PROJ_SPLITS):
        ref[...] = jnp.dot(xn, w_ref[:, lo:lo + width], preferred_element_type=F32).astype(BF16)
        lo += width
    dt_ref[...] = jnp.dot(xn, wdt_ref[...], preferred_element_type=F32)


def _inproj(h2d, g, w_main, w_dt, tm):
    m = h2d.shape[0]
    n_main = w_main.shape[1]
    row = lambda i: (i, 0)
    outs = [jax.ShapeDtypeStruct((m, width), BF16) for _, width in _PROJ_SPLITS]
    outs.append(jax.ShapeDtypeStruct((m, DT_LANES), F32))
    out_specs = [pl.BlockSpec((tm, width), row) for _, width in _PROJ_SPLITS]
    out_specs.append(pl.BlockSpec((tm, DT_LANES), row))
    return pl.pallas_call(
        _inproj_kernel,
        grid=(m // tm,),
        in_specs=[
            pl.BlockSpec((tm, D_MODEL), row),
            _const_spec((1, D_MODEL)),
            _const_spec((D_MODEL, n_main)),
            _const_spec((D_MODEL, DT_LANES)),
        ],
        out_specs=out_specs,
        out_shape=outs,
        compiler_params=pltpu.CompilerParams(
            dimension_semantics=("arbitrary",), vmem_limit_bytes=V7X_VMEM_LIMIT),
        name="inproj",
    )(h2d, g, w_main, w_dt)


def _ssd_kernel(xbc_ref, z_ref, dt_ref, cw_ref, cb_ref, dtb_ref, alog_ref, dexp_ref, g_ref, e_ref,
                out_ref, cbuf, state):
    c = pl.program_id(1)

    @pl.when(c == 0)
    def _():
        cbuf[0:8, :] = jnp.zeros((8, XBC_WIDTH), F32)
        state[...] = jnp.zeros_like(state)

    x_raw = xbc_ref[0].astype(F32)
    cbuf[8:8 + BLK, :] = x_raw
    cw = cw_ref[...]
    acc = cb_ref[...] + cw[3:4, :] * x_raw
    for tap in range(SSD_CONV - 1):
        acc = acc + cw[tap:tap + 1, :] * cbuf[5 + tap:5 + tap + BLK, :]
    cbuf[0:8, :] = x_raw[BLK - 8:, :]
    xc = acc * jax.nn.sigmoid(acc)

    row = lax.broadcasted_iota(jnp.int32, (BLK, 1), 0)
    real = jnp.logical_or(c > 0, row >= PAD)
    xc = jnp.where(real, xc, 0.0)
    xs = xc[:, :SSD_INNER]

    lane = lax.broadcasted_iota(jnp.int32, (BLK, DT_LANES), 1)
    dt = jax.nn.softplus(dt_ref[0] + dtb_ref[...])
    dt = jnp.where(jnp.logical_and(real, lane < SSD_HEADS), dt, 0.0)
    a_neg = -jnp.exp(alog_ref[...])
    adt = dt * a_neg

    ri = lax.broadcasted_iota(jnp.int32, (BLK, BLK), 0)
    ci = lax.broadcasted_iota(jnp.int32, (BLK, BLK), 1)
    causal = ri >= ci
    tri = jnp.where(causal, 1.0, 0.0).astype(F32)
    hi = lax.Precision.HIGHEST
    acs = jnp.dot(tri, adt, precision=hi, preferred_element_type=F32)
    acs_t = acs.T

    e = e_ref[...]
    dt_exp = jnp.dot(dt, e, precision=hi, preferred_element_type=F32)
    acs_exp = jnp.dot(acs, e, precision=hi, preferred_element_type=F32)
    last = acs_exp[BLK - 1:BLK, :]
    xdt = xs * dt_exp
    xdec = (xdt * jnp.exp(last - acs_exp)).astype(BF16)
    xdt_b = xdt.astype(BF16)
    off_scale = jnp.exp(acs_exp)
    chunk_decay = jnp.exp(last)

    half = lax.broadcasted_iota(jnp.int32, (BLK, BLK), 1) < SSD_HEAD_DIM
    y_cols = []
    for grp in range(SSD_GROUPS):
        b_g = xc[:, SSD_INNER + grp * SSD_STATE:SSD_INNER + (grp + 1) * SSD_STATE]
        c_g = xc[:, SSD_INNER + (SSD_GROUPS + grp) * SSD_STATE:
                 SSD_INNER + (SSD_GROUPS + grp + 1) * SSD_STATE]
        b_gb = b_g.astype(BF16)
        c_gb = c_g.astype(BF16)
        cb_mat = lax.dot_general(c_gb, b_gb, (((1,), (1,)), ((), ())), preferred_element_type=F32)
        gcols = slice(grp * GROUP_COLS, (grp + 1) * GROUP_COLS)
        s_prev = state[:, gcols]
        y_off = jnp.dot(c_gb, s_prev.astype(BF16), preferred_element_type=F32) * off_scale[:, gcols]
        for pair in range(GROUP_COLS // BLK):
            h0 = (grp * GROUP_COLS + pair * BLK) // SSD_HEAD_DIM
            cols = slice(grp * GROUP_COLS + pair * BLK, grp * GROUP_COLS + (pair + 1) * BLK)
            x_pair = xdt_b[:, cols]
            parts = []
            for hh in (h0, h0 + 1):
                seg = acs[:, hh:hh + 1] - acs_t[hh:hh + 1, :]
                ldec = jnp.exp(jnp.where(causal, seg, -1e30))
                m = (cb_mat * ldec).astype(BF16)
                parts.append(jnp.dot(m, x_pair, preferred_element_type=F32))
            y_cols.append(jnp.where(half, parts[0], parts[1])
                          + y_off[:, pair * BLK:(pair + 1) * BLK])
        new = jnp.dot(b_g.T.astype(BF16), xdec[:, gcols], preferred_element_type=F32)
        state[:, gcols] = s_prev * chunk_decay[:, gcols] + new

    y = jnp.concatenate(y_cols, axis=1) + xs * dexp_ref[...]
    zf = z_ref[0].astype(F32)
    y = y * (zf * jax.nn.sigmoid(zf))
    out_ref[0] = _rms(y, g_ref[...]).astype(BF16)


def _ssd(xbc, z, dt_raw, conv_w, conv_b, dt_bias, a_log, d_exp, norm_g, expand):
    bsz, lp, _ = xbc.shape
    nblk = lp // BLK
    blk = lambda b, c: (b, (c + nblk - 1) % nblk, 0)
    return pl.pallas_call(
        _ssd_kernel,
        grid=(bsz, nblk),
        in_specs=[
            pl.BlockSpec((1, BLK, XBC_WIDTH), blk),
            pl.BlockSpec((1, BLK, SSD_INNER), blk),
            pl.BlockSpec((1, BLK, DT_LANES), blk),
            _const_spec((SSD_CONV, XBC_WIDTH)),
            _const_spec((1, XBC_WIDTH)),
            _const_spec((1, DT_LANES)),
            _const_spec((1, DT_LANES)),
            _const_spec((1, SSD_INNER)),
            _const_spec((1, SSD_INNER)),
            _const_spec((DT_LANES, SSD_INNER)),
        ],
        out_specs=pl.BlockSpec((1, BLK, SSD_INNER), blk),
        out_shape=jax.ShapeDtypeStruct((bsz, lp, SSD_INNER), BF16),
        scratch_shapes=[
            pltpu.VMEM((BLK + 8, XBC_WIDTH), F32),
            pltpu.VMEM((SSD_STATE, SSD_INNER), F32),
        ],
        compiler_params=pltpu.CompilerParams(
            dimension_semantics=("arbitrary", "arbitrary"), vmem_limit_bytes=V7X_VMEM_LIMIT),
        name="ssd",
    )(xbc, z, dt_raw, conv_w, conv_b, dt_bias, a_log, d_exp, norm_g, expand)


SB_DEAD_LOG = -104.0


def _sb_kernel(q_ref, k_ref, v_ref, tt_ref, o_ref, qm_scr, run_scr, acc_scr, *, nblk, npairs):
    qi = pl.program_id(2)
    lane = lax.broadcasted_iota(jnp.int32, (BLK, BLK), 1)
    sub = lax.broadcasted_iota(jnp.int32, (BLK, BLK), 0)
    first = lane < SB_HEAD_DIM
    strictly_older = lane < sub
    scale = 1.0 / math.sqrt(SB_HEAD_DIM)
    tt = tt_ref[...]

    for p in range(npairs):
        qp = q_ref[0, :, p * BLK:(p + 1) * BLK] * scale
        zero = jnp.zeros_like(qp)
        qm_scr[2 * p] = jnp.where(first, qp, zero)
        qm_scr[2 * p + 1] = jnp.where(first, zero, qp)

    def process(j, diag):
        off = pl.multiple_of(((j + nblk - 1) % nblk) * BLK, BLK)
        worst = None
        for p in range(npairs):
            kb = k_ref[0, pl.ds(off, BLK), p * BLK:(p + 1) * BLK]
            vb = v_ref[0, pl.ds(off, BLK), p * BLK:(p + 1) * BLK]
            vzero = jnp.zeros_like(vb)
            v_heads = (jnp.where(first, vb, vzero), jnp.where(first, vzero, vb))
            pv = None
            for hh in range(2):
                h = 2 * p + hh
                s = lax.dot_general(qm_scr[h], kb, (((1,), (1,)), ((), ())),
                                    preferred_element_type=F32)
                sp = jnp.log(1.0 + jnp.exp(-jnp.abs(s)))
                log_beta = jnp.minimum(s, 0.0) - sp
                log_keep = log_beta - s
                if diag:
                    log_keep = jnp.where(strictly_older, log_keep, 0.0)
                lk_hi = log_keep.astype(BF16)
                lk_lo = (log_keep - lk_hi.astype(F32)).astype(BF16)
                sums = jnp.dot(jnp.concatenate([lk_hi, lk_lo], axis=1), tt,
                               preferred_element_type=F32)
                if diag:
                    w = jnp.where(strictly_older, jnp.exp(log_beta + sums[:, :BLK]), 0.0)
                    run = sums[:, BLK:]
                else:
                    run = run_scr[h]
                    w = jnp.exp(log_beta + (sums[:, :BLK] + run))
                    run = run + sums[:, BLK:]
                run_scr[h] = run
                worst = run if worst is None else jnp.maximum(worst, run)
                contrib = jnp.dot(w.astype(BF16), v_heads[hh], preferred_element_type=F32)
                pv = contrib if pv is None else pv + contrib
            if diag:
                acc_scr[p] = pv
            else:
                acc_scr[p] = acc_scr[p] + pv
        return jnp.max(worst)

    worst0 = process(qi, True)

    def cond(st):
        return jnp.logical_and(st[0] >= 0, st[1] > SB_DEAD_LOG)

    def body(st):
        return st[0] - 1, process(st[0], False)

    lax.while_loop(cond, body, (qi - 1, worst0))
    for p in range(npairs):
        o_ref[0, :, p * BLK:(p + 1) * BLK] = acc_scr[p].astype(BF16)


def _sb_attention(q, k, v, tt, npairs):
    bsz, lp, width = q.shape
    nblk = lp // BLK
    wblk = npairs * BLK
    qblk = lambda b, p, i: (b, (i + nblk - 1) % nblk, p)
    kv = lambda b, p, i: (b, 0, p)
    kv_spec = pl.BlockSpec((1, lp, wblk), kv, pipeline_mode=pl.Buffered(1))
    return pl.pallas_call(
        functools.partial(_sb_kernel, nblk=nblk, npairs=npairs),
        grid=(bsz, width // wblk, nblk),
        in_specs=[
            pl.BlockSpec((1, BLK, wblk), qblk),
            kv_spec,
            kv_spec,
            _const_spec((2 * BLK, 2 * BLK)),
        ],
        out_specs=pl.BlockSpec((1, BLK, wblk), qblk),
        out_shape=jax.ShapeDtypeStruct((bsz, lp, width), BF16),
        scratch_shapes=[
            pltpu.VMEM((2 * npairs, BLK, BLK), BF16),
            pltpu.VMEM((2 * npairs, BLK, BLK), F32),
            pltpu.VMEM((npairs, BLK, BLK), F32),
        ],
        compiler_params=pltpu.CompilerParams(
            dimension_semantics=("arbitrary", "arbitrary", "arbitrary"),
            vmem_limit_bytes=V7X_VMEM_LIMIT),
        name="sb_attention",
    )(q, k, v, tt)


def _outproj_kernel(ys_ref, o_ref, h_ref, wa_ref, wb_ref, gsb_ref, gpost_ref, gffn_ref,
                    h1_ref, xn2_ref):
    osb = _rms(o_ref[...].astype(F32), gsb_ref[...]).astype(BF16)
    mix = jnp.dot(ys_ref[...], wa_ref[...], preferred_element_type=F32)
    mix = mix + jnp.dot(osb, wb_ref[...], preferred_element_type=F32)
    h1 = h_ref[...] + _rms(mix, gpost_ref[...])
    h1_ref[...] = h1
    xn2_ref[...] = _rms(h1, gffn_ref[...]).astype(BF16)


def _outproj(y_ssd, o_sb, h2d, w_a, w_b, g_sb, g_post, g_ffn, tm):
    m = h2d.shape[0]
    row = lambda i: (i, 0)
    return pl.pallas_call(
        _outproj_kernel,
        grid=(m // tm,),
        in_specs=[
            pl.BlockSpec((tm, SSD_INNER), row),
            pl.BlockSpec((tm, SB_WIDTH), row),
            pl.BlockSpec((tm, D_MODEL), row),
            _const_spec((SSD_INNER, D_MODEL)),
            _const_spec((SB_WIDTH, D_MODEL)),
            _const_spec((1, SB_WIDTH)),
            _const_spec((1, D_MODEL)),
            _const_spec((1, D_MODEL)),
        ],
        out_specs=[pl.BlockSpec((tm, D_MODEL), row), pl.BlockSpec((tm, D_MODEL), row)],
        out_shape=[jax.ShapeDtypeStruct((m, D_MODEL), F32), jax.ShapeDtypeStruct((m, D_MODEL), BF16)],
        compiler_params=pltpu.CompilerParams(
            dimension_semantics=("arbitrary",), vmem_limit_bytes=V7X_VMEM_LIMIT),
        name="outproj",
    )(y_ssd, o_sb, h2d, w_a, w_b, g_sb, g_post, g_ffn)


def _gelu_tanh(x):
    return 0.5 * x * (1.0 + jnp.tanh(math.sqrt(2.0 / math.pi) * (x + 0.044715 * (x * x * x))))


def _ffn_kernel(xn_ref, xm_ref, h1_ref, wup_ref, cw_ref, cb_ref, wd_ref, g_ref, out_ref,
                gbuf, halo, *, tm, tf):
    i = pl.program_id(1)
    xn = xn_ref[0]
    acc = jnp.zeros((tm, D_MODEL), F32)
    for cidx in range(D_FF // tf):
        cols = slice(cidx * tf, (cidx + 1) * tf)
        w_g = wup_ref[:, cols]
        gate = jnp.dot(xn, w_g, preferred_element_type=F32)
        up = jnp.dot(xn, wup_ref[:, D_FF + cidx * tf:D_FF + (cidx + 1) * tf],
                     preferred_element_type=F32)

        @pl.when(i == 0)
        def _():
            gm = jnp.dot(xm_ref[0], w_g, preferred_element_type=F32)
            halo[:, cols] = gm[N_META - 8:, :]

        gbuf[0:8, :] = halo[:, cols]
        gbuf[8:8 + tm, :] = gate
        halo[:, cols] = gate[tm - 8:, :]
        cw = cw_ref[:, cols]
        conv = cb_ref[:, cols] + cw[2:3, :] * gate
        conv = conv + cw[1:2, :] * gbuf[7:7 + tm, :] + cw[0:1, :] * gbuf[6:6 + tm, :]
        act = (_gelu_tanh(conv) * up).astype(BF16)
        acc = acc + jnp.dot(act, wd_ref[cols, :], preferred_element_type=F32)
    out_ref[0] = h1_ref[0] + _rms(acc, g_ref[...])


def _ffn(xn2, h1, w_up, conv_w, conv_b, w_down, g_post, seq, tm, tf):
    bsz, lp, _ = xn2.shape
    xblk = lambda b, i: (b, i, 0)
    meta = lambda b, i: (b, (lp - N_META) // N_META, 0)
    return pl.pallas_call(
        functools.partial(_ffn_kernel, tm=tm, tf=tf),
        grid=(bsz, seq // tm),
        in_specs=[
            pl.BlockSpec((1, tm, D_MODEL), xblk),
            pl.BlockSpec((1, N_META, D_MODEL), meta),
            pl.BlockSpec((1, tm, D_MODEL), xblk),
            _const_spec((D_MODEL, 2 * D_FF)),
            _const_spec((FFN_CONV, D_FF)),
            _const_spec((1, D_FF)),
            _const_spec((D_FF, D_MODEL)),
            _const_spec((1, D_MODEL)),
        ],
        out_specs=pl.BlockSpec((1, tm, D_MODEL), xblk),
        out_shape=jax.ShapeDtypeStruct((bsz, seq, D_MODEL), F32),
        scratch_shapes=[
            pltpu.VMEM((tm + 8, tf), F32),
            pltpu.VMEM((8, D_FF), F32),
        ],
        compiler_params=pltpu.CompilerParams(
            dimension_semantics=("arbitrary", "arbitrary"), vmem_limit_bytes=V7X_VMEM_LIMIT),
        name="ffn",
    )(xn2, xn2, h1, w_up, conv_w, conv_b, w_down, g_post)


def _pad_lanes(v, n):
    return jnp.pad(v, ((0, 0), (0, n - v.shape[-1])))


def kernel(x, meta_tokens, mix_pre_g, w_in, ssd_conv_w, ssd_conv_b, ssd_dt_bias, ssd_a_log, ssd_d,
           ssd_norm_g, sb_norm_g, w_out, mix_post_g, ffn_pre_g, w_up, ffn_conv_w, ffn_conv_b,
           w_down, ffn_post_g):
    bsz, seq, d = x.shape
    lp = seq + BLK
    depth = w_in.shape[0]
    assert depth == 1 and d == D_MODEL and seq % BLK == 0

    meta = jnp.broadcast_to(meta_tokens.astype(x.dtype)[None], (bsz, N_META, d))
    h = jnp.concatenate([x, jnp.zeros((bsz, PAD, d), x.dtype), meta], axis=1)
    h2d = h.reshape(bsz * lp, d)

    l = 0
    off_xbc = SSD_INNER
    off_dt = off_xbc + XBC_WIDTH
    off_q = off_dt + SSD_HEADS
    w = w_in[l]
    w_main = jnp.concatenate([w[:, :off_dt], w[:, off_q:]], axis=1).astype(BF16)
    w_dt = _pad_lanes(w[:, off_dt:off_q], DT_LANES).astype(BF16)

    tm_rows = lp // 8
    z, xbc, q, k, v, dt_raw = _inproj(h2d, mix_pre_g[l][None], w_main, w_dt, tm_rows)
    to3 = lambda a: a.reshape(bsz, lp, a.shape[-1])

    head_of_col = jnp.arange(SSD_INNER) // SSD_HEAD_DIM
    expand = (jnp.arange(DT_LANES)[:, None] == head_of_col[None, :]).astype(F32)
    d_exp = jnp.repeat(ssd_d[l].astype(F32), SSD_HEAD_DIM)[None]
    y_ssd = _ssd(to3(xbc), to3(z), to3(dt_raw), ssd_conv_w[l], ssd_conv_b[l][None],
                 _pad_lanes(ssd_dt_bias[l][None], DT_LANES), _pad_lanes(ssd_a_log[l][None], DT_LANES),
                 d_exp, ssd_norm_g[l][None], expand)

    kk = jnp.arange(BLK)
    suffix = (kk[:, None] > kk[None, :]).astype(BF16)
    half_tt = jnp.concatenate([suffix, jnp.ones((BLK, BLK), BF16)], axis=1)
    tt = jnp.concatenate([half_tt, half_tt], axis=0)
    o_sb = _sb_attention(to3(q), to3(k), to3(v), tt, SB_WIDTH // BLK)

    wo = w_out[l].astype(BF16)
    h1, xn2 = _outproj(y_ssd.reshape(bsz * lp, SSD_INNER), o_sb.reshape(bsz * lp, SB_WIDTH), h2d,
                       wo[:SSD_INNER], wo[SSD_INNER:], sb_norm_g[l][None], mix_post_g[l][None],
                       ffn_pre_g[l][None], tm_rows)

    return _ffn(to3(xn2), to3(h1), w_up[l].astype(BF16), ffn_conv_w[l], ffn_conv_b[l][None],
                w_down[l].astype(BF16), ffn_post_g[l][None], seq, 512, 256)
```

```python
import functools
import math

import jax
import jax.numpy as jnp
from jax import lax
from jax.experimental import pallas as pl
from jax.experimental.pallas import tpu as pltpu

F32 = jnp.float32
BF16 = jnp.bfloat16

D_MODEL = 1024
N_META = 16
BLK = 128
PAD = BLK - N_META
SSD_HEADS = 16
SSD_HEAD_DIM = 64
SSD_GROUPS = 2
SSD_STATE = 128
SSD_INNER = 1024
SSD_CONV = 4
XBC_WIDTH = SSD_INNER + 2 * SSD_GROUPS * SSD_STATE
SB_WIDTH = 1024
SB_HEAD_DIM = 64
D_FF = 2816
FFN_CONV = 3
EPS = 1e-6
DT_LANES = 128
GROUP_COLS = SSD_INNER // SSD_GROUPS

V7X_VMEM_LIMIT = 56 * 1024 * 1024


def _rms(x, g):
    return x * lax.rsqrt(jnp.mean(x * x, axis=-1, keepdims=True) + EPS) * g


def _const_spec(shape):
    nd = len(shape)
    return pl.BlockSpec(shape, lambda *_: (0,) * nd, pipeline_mode=pl.Buffered(1))


_PROJ_SPLITS = (("z", SSD_INNER), ("xbc", XBC_WIDTH), ("q", SB_WIDTH), ("k", SB_WIDTH), ("v", SB_WIDTH))


def _inproj_kernel(h_ref, g_ref, w_ref, wdt_ref, z_ref, xbc_ref, q_ref, k_ref, v_ref, dt_ref):
    xn = _rms(h_ref[...], g_ref[...]).astype(BF16)
    lo = 0
    for ref, (_, width) in zip((z_ref, xbc_ref, q_ref, k_ref, v_ref), _PROJ_SPLITS):
        ref[...] = jnp.dot(xn, w_ref[:, lo:lo + width], preferred_element_type=F32).astype(BF16)
        lo += width
    dt_ref[...] = jnp.dot(xn, wdt_ref[...], preferred_element_type=F32)


def _inproj(h2d, g, w_main, w_dt, tm):
    m = h2d.shape[0]
    n_main = w_main.shape[1]
    row = lambda i: (i, 0)
    outs = [jax.ShapeDtypeStruct((m, width), BF16) for _, width in _PROJ_SPLITS]
    outs.append(jax.ShapeDtypeStruct((m, DT_LANES), F32))
    out_specs = [pl.BlockSpec((tm, width), row) for _, width in _PROJ_SPLITS]
    out_specs.append(pl.BlockSpec((tm, DT_LANES), row))
    return pl.pallas_call(
        _inproj_kernel,
        grid=(m // tm,),
        in_specs=[
            pl.BlockSpec((tm, D_MODEL), row),
            _const_spec((1, D_MODEL)),
            _const_spec((D_MODEL, n_main)),
            _const_spec((D_MODEL, DT_LANES)),
        ],
        out_specs=out_specs,
        out_shape=outs,
        compiler_params=pltpu.CompilerParams(
            dimension_semantics=("arbitrary",), vmem_limit_bytes=V7X_VMEM_LIMIT),
        name="inproj",
    )(h2d, g, w_main, w_dt)


def _dot_f32_by_01(lhs, rhs, f32_side):
    x = lhs if f32_side == "lhs" else rhs
    p1 = x.astype(BF16)
    r1 = x - p1.astype(F32)
    p2 = r1.astype(BF16)
    p3 = (r1 - p2.astype(F32)).astype(BF16)
    out = None
    for piece in (p1, p2, p3):
        ops = (piece, rhs) if f32_side == "lhs" else (lhs, piece)
        term = jnp.dot(*ops, preferred_element_type=F32)
        out = term if out is None else out + term
    return out


def _ssd_kernel(xbc_ref, z_ref, dt_ref, cw_ref, cb_ref, dtb_ref, alog_ref, dexp_ref, g_ref, e_ref,
                out_ref, cbuf, state):
    c = pl.program_id(1)

    @pl.when(c == 0)
    def _():
        cbuf[0:8, :] = jnp.zeros((8, XBC_WIDTH), F32)
        state[...] = jnp.zeros_like(state)

    x_raw = xbc_ref[0].astype(F32)
    cbuf[8:8 + BLK, :] = x_raw
    cw = cw_ref[...]
    acc = cb_ref[...] + cw[3:4, :] * x_raw
    for tap in range(SSD_CONV - 1):
        acc = acc + cw[tap:tap + 1, :] * cbuf[5 + tap:5 + tap + BLK, :]
    cbuf[0:8, :] = x_raw[BLK - 8:, :]
    xc = acc * jax.nn.sigmoid(acc)

    row = lax.broadcasted_iota(jnp.int32, (BLK, 1), 0)
    real = jnp.logical_or(c > 0, row >= PAD)
    xc = jnp.where(real, xc, 0.0)
    xs = xc[:, :SSD_INNER]

    lane = lax.broadcasted_iota(jnp.int32, (BLK, DT_LANES), 1)
    dt = jax.nn.softplus(dt_ref[0] + dtb_ref[...])
    dt = jnp.where(jnp.logical_and(real, lane < SSD_HEADS), dt, 0.0)
    a_neg = -jnp.exp(alog_ref[...])
    adt = dt * a_neg

    ri = lax.broadcasted_iota(jnp.int32, (BLK, BLK), 0)
    ci = lax.broadcasted_iota(jnp.int32, (BLK, BLK), 1)
    causal = ri >= ci
    tri = jnp.where(causal, 1.0, 0.0).astype(BF16)
    acs = _dot_f32_by_01(tri, adt, f32_side="rhs")
    acs_t = acs.T

    e = e_ref[...]
    dt_exp = _dot_f32_by_01(dt, e, f32_side="lhs")
    acs_exp = _dot_f32_by_01(acs, e, f32_side="lhs")
    last = acs_exp[BLK - 1:BLK, :]
    xdt = xs * dt_exp
    xdec = (xdt * jnp.exp(last - acs_exp)).astype(BF16)
    xdt_b = xdt.astype(BF16)
    off_scale = jnp.exp(acs_exp)
    chunk_decay = jnp.exp(last)

    half = lax.broadcasted_iota(jnp.int32, (BLK, BLK), 1) < SSD_HEAD_DIM
    y_cols = []
    for grp in range(SSD_GROUPS):
        b_g = xc[:, SSD_INNER + grp * SSD_STATE:SSD_INNER + (grp + 1) * SSD_STATE]
        c_g = xc[:, SSD_INNER + (SSD_GROUPS + grp) * SSD_STATE:
                 SSD_INNER + (SSD_GROUPS + grp + 1) * SSD_STATE]
        b_gb = b_g.astype(BF16)
        c_gb = c_g.astype(BF16)
        cb_mat = lax.dot_general(c_gb, b_gb, (((1,), (1,)), ((), ())), preferred_element_type=F32)
        gcols = slice(grp * GROUP_COLS, (grp + 1) * GROUP_COLS)
        s_prev = state[:, gcols]
        y_off = jnp.dot(c_gb, s_prev.astype(BF16), preferred_element_type=F32) * off_scale[:, gcols]
        for pair in range(GROUP_COLS // BLK):
            h0 = (grp * GROUP_COLS + pair * BLK) // SSD_HEAD_DIM
            cols = slice(grp * GROUP_COLS + pair * BLK, grp * GROUP_COLS + (pair + 1) * BLK)
            x_pair = xdt_b[:, cols]
            parts = []
            for hh in (h0, h0 + 1):
                seg = acs[:, hh:hh + 1] - acs_t[hh:hh + 1, :]
                ldec = jnp.exp(jnp.where(causal, seg, -1e30))
                m = (cb_mat * ldec).astype(BF16)
                parts.append(jnp.dot(m, x_pair, preferred_element_type=F32))
            y_cols.append(jnp.where(half, parts[0], parts[1])
                          + y_off[:, pair * BLK:(pair + 1) * BLK])
        new = jnp.dot(b_g.T.astype(BF16), xdec[:, gcols], preferred_element_type=F32)
        state[:, gcols] = s_prev * chunk_decay[:, gcols] + new

    y = jnp.concatenate(y_cols, axis=1) + xs * dexp_ref[...]
    zf = z_ref[0].astype(F32)
    y = y * (zf * jax.nn.sigmoid(zf))
    out_ref[0] = _rms(y, g_ref[...]).astype(BF16)


def _ssd(xbc, z, dt_raw, conv_w, conv_b, dt_bias, a_log, d_exp, norm_g, expand):
    bsz, lp, _ = xbc.shape
    nblk = lp // BLK
    blk = lambda b, c: (b, (c + nblk - 1) % nblk, 0)
    return pl.pallas_call(
        _ssd_kernel,
        grid=(bsz, nblk),
        in_specs=[
            pl.BlockSpec((1, BLK, XBC_WIDTH), blk),
            pl.BlockSpec((1, BLK, SSD_INNER), blk),
            pl.BlockSpec((1, BLK, DT_LANES), blk),
            _const_spec((SSD_CONV, XBC_WIDTH)),
            _const_spec((1, XBC_WIDTH)),
            _const_spec((1, DT_LANES)),
            _const_spec((1, DT_LANES)),
            _const_spec((1, SSD_INNER)),
            _const_spec((1, SSD_INNER)),
            _const_spec((DT_LANES, SSD_INNER)),
        ],
        out_specs=pl.BlockSpec((1, BLK, SSD_INNER), blk),
        out_shape=jax.ShapeDtypeStruct((bsz, lp, SSD_INNER), BF16),
        scratch_shapes=[
            pltpu.VMEM((BLK + 8, XBC_WIDTH), F32),
            pltpu.VMEM((SSD_STATE, SSD_INNER), F32),
        ],
        compiler_params=pltpu.CompilerParams(
            dimension_semantics=("arbitrary", "arbitrary"), vmem_limit_bytes=V7X_VMEM_LIMIT),
        name="ssd",
    )(xbc, z, dt_raw, conv_w, conv_b, dt_bias, a_log, d_exp, norm_g, expand)


SB_DEAD_LOG = -104.0
LOG2E = 1.4426950408889634
SB_LEAD = 3


def _sb_kernel(q_ref, k_ref, v_ref, tt_ref, o_ref, qm_scr, run_scr, acc_scr, worst_scr, *,
               nblk, npairs):
    qi = pl.program_id(2)
    lane = lax.broadcasted_iota(jnp.int32, (BLK, BLK), 1)
    sub = lax.broadcasted_iota(jnp.int32, (BLK, BLK), 0)
    first = lane < SB_HEAD_DIM
    strictly_older = lane < sub
    scale = 1.0 / math.sqrt(SB_HEAD_DIM)
    tt = tt_ref[...]

    for p in range(npairs):
        qp = q_ref[0, :, p * BLK:(p + 1) * BLK] * scale
        zero = jnp.zeros_like(qp)
        qm_scr[p, :BLK, :] = jnp.where(first, qp, zero)
        qm_scr[p, BLK:, :] = jnp.where(first, zero, qp)

    def process(blocks, init, check):
        offs = [pl.multiple_of(((j + nblk - 1) % nblk) * BLK, BLK) for j in blocks]
        heads = range(2 * npairs)
        nb = range(len(blocks))
        scores = [[
            lax.dot_general(qm_scr[p], k_ref[0, pl.ds(offs[b], BLK), p * BLK:(p + 1) * BLK],
                            (((1,), (1,)), ((), ())), preferred_element_type=F32)
            for p in range(npairs)] for b in nb]
        log_betas = [[None] * len(heads) for _ in nb]
        log_keeps = [[None] * len(heads) for _ in nb]
        totals = [[None] * len(heads) for _ in nb]
        for b in nb:
            for h in heads:
                s = scores[b][h // 2][(h % 2) * BLK:(h % 2 + 1) * BLK, :]
                sp = jnp.log(1.0 + jnp.exp2(jnp.abs(s) * (-LOG2E)))
                log_beta = jnp.minimum(s, 0.0) - sp
                log_keep = log_beta - s
                if init and b == 0:
                    log_keep = jnp.where(strictly_older, log_keep, 0.0)
                log_betas[b][h] = log_beta
                log_keeps[b][h] = log_keep.astype(BF16)
                totals[b][h] = jnp.sum(log_keep, axis=1, keepdims=True)
        sums = [jnp.dot(jnp.concatenate(log_keeps[b], axis=0), tt, preferred_element_type=F32)
                for b in nb]
        weights = [[None] * len(heads) for _ in nb]
        worst = None
        for h in heads:
            run = None if init else run_scr[h]
            for b in nb:
                after, total = sums[b][h * BLK:(h + 1) * BLK, :], totals[b][h]
                if run is None:
                    w = jnp.where(strictly_older, jnp.exp(log_betas[b][h] + after), 0.0)
                    run = jnp.broadcast_to(total, (BLK, BLK))
                else:
                    w = jnp.exp(log_betas[b][h] + (after + run))
                    run = run + total
                weights[b][h] = w.astype(BF16)
            run_scr[h] = run
            if check:
                worst = run if worst is None else jnp.maximum(worst, run)
        for p in range(npairs):
            pv = None if init else acc_scr[p]
            for b in nb:
                vb = v_ref[0, pl.ds(offs[b], BLK), p * BLK:(p + 1) * BLK]
                vzero = jnp.zeros_like(vb)
                v_heads = jnp.concatenate([jnp.where(first, vb, vzero), jnp.where(first, vzero, vb)], axis=0)
                w_heads = jnp.concatenate([weights[b][2 * p], weights[b][2 * p + 1]], axis=1)
                contrib = jnp.dot(w_heads, v_heads, preferred_element_type=F32)
                pv = contrib if pv is None else pv + contrib
            acc_scr[p] = pv
        return jnp.max(worst) if check else None

    for lead in range(1, SB_LEAD + 1):
        last = lead == SB_LEAD

        @pl.when(qi >= lead - 1 if last else qi == lead - 1)
        def _(lead=lead, last=last):
            worst = process([qi - i for i in range(lead)], True, last)
            worst_scr[0] = worst if last else jnp.float32(0.0)

    def cond(st):
        return jnp.logical_and(st[0] >= 0, st[1] > SB_DEAD_LOG)

    def body(st):
        return st[0] - 1, process([st[0]], False, True)

    lax.while_loop(cond, body, (qi - SB_LEAD, worst_scr[0]))
    for p in range(npairs):
        o_ref[0, :, p * BLK:(p + 1) * BLK] = acc_scr[p].astype(BF16)


def _sb_attention(q, k, v, tt, npairs):
    bsz, lp, width = q.shape
    nblk = lp // BLK
    wblk = npairs * BLK
    qblk = lambda b, p, i: (b, (i + nblk - 1) % nblk, p)
    kv = lambda b, p, i: (b, 0, p)
    kv_spec = pl.BlockSpec((1, lp, wblk), kv, pipeline_mode=pl.Buffered(1))
    return pl.pallas_call(
        functools.partial(_sb_kernel, nblk=nblk, npairs=npairs),
        grid=(bsz, width // wblk, nblk),
        in_specs=[
            pl.BlockSpec((1, BLK, wblk), qblk),
            kv_spec,
            kv_spec,
            _const_spec((BLK, BLK)),
        ],
        out_specs=pl.BlockSpec((1, BLK, wblk), qblk),
        out_shape=jax.ShapeDtypeStruct((bsz, lp, width), BF16),
        scratch_shapes=[
            pltpu.VMEM((npairs, 2 * BLK, BLK), BF16),
            pltpu.VMEM((2 * npairs, BLK, BLK), F32),
            pltpu.VMEM((npairs, BLK, BLK), F32),
            pltpu.SMEM((1,), F32),
        ],
        compiler_params=pltpu.CompilerParams(
            dimension_semantics=("arbitrary", "arbitrary", "arbitrary"),
            vmem_limit_bytes=V7X_VMEM_LIMIT),
        name="sb_attention",
    )(q, k, v, tt)


def _outproj_kernel(ys_ref, o_ref, h_ref, wa_ref, wb_ref, gsb_ref, gpost_ref, gffn_ref,
                    h1_ref, xn2_ref):
    osb = _rms(o_ref[...].astype(F32), gsb_ref[...]).astype(BF16)
    mix = jnp.dot(ys_ref[...], wa_ref[...], preferred_element_type=F32)
    mix = mix + jnp.dot(osb, wb_ref[...], preferred_element_type=F32)
    h1 = h_ref[...] + _rms(mix, gpost_ref[...])
    h1_ref[...] = h1
    xn2_ref[...] = _rms(h1, gffn_ref[...]).astype(BF16)


def _outproj(y_ssd, o_sb, h2d, w_a, w_b, g_sb, g_post, g_ffn, tm):
    m = h2d.shape[0]
    row = lambda i: (i, 0)
    return pl.pallas_call(
        _outproj_kernel,
        grid=(m // tm,),
        in_specs=[
            pl.BlockSpec((tm, SSD_INNER), row),
            pl.BlockSpec((tm, SB_WIDTH), row),
            pl.BlockSpec((tm, D_MODEL), row),
            _const_spec((SSD_INNER, D_MODEL)),
            _const_spec((SB_WIDTH, D_MODEL)),
            _const_spec((1, SB_WIDTH)),
            _const_spec((1, D_MODEL)),
            _const_spec((1, D_MODEL)),
        ],
        out_specs=[pl.BlockSpec((tm, D_MODEL), row), pl.BlockSpec((tm, D_MODEL), row)],
        out_shape=[jax.ShapeDtypeStruct((m, D_MODEL), F32), jax.ShapeDtypeStruct((m, D_MODEL), BF16)],
        compiler_params=pltpu.CompilerParams(
            dimension_semantics=("arbitrary",), vmem_limit_bytes=V7X_VMEM_LIMIT),
        name="outproj",
    )(y_ssd, o_sb, h2d, w_a, w_b, g_sb, g_post, g_ffn)


def _gelu_tanh(x):
    return 0.5 * x * (1.0 + jnp.tanh(math.sqrt(2.0 / math.pi) * (x + 0.044715 * (x * x * x))))


def _ffn_kernel(xn_ref, xm_ref, h1_ref, wup_ref, cw_ref, cb_ref, wd_ref, g_ref, out_ref,
                gbuf0, gbuf1, ubuf0, ubuf1, halo, act_scr, *, tm, nch):
    i = pl.program_id(1)
    gbufs, ubufs = (gbuf0, gbuf1), (ubuf0, ubuf1)

    @pl.when(i == 0)
    def _():
        def meta_halo(c, carry):
            gm = jnp.dot(xm_ref[0], wup_ref[c], preferred_element_type=F32)
            halo[c] = gm[N_META - 8:, :]
            return carry
        lax.fori_loop(0, nch, meta_halo, 0)

    def project(c, slot):
        gbuf, ubuf = gbufs[slot], ubufs[slot]
        xn = xn_ref[0]
        gate = jnp.dot(xn, wup_ref[c], preferred_element_type=F32)
        gbuf[0:8, :] = halo[c]
        gbuf[8:8 + tm, :] = gate
        halo[c] = gate[tm - 8:, :]
        ubuf[...] = jnp.dot(xn, wup_ref[nch + c], preferred_element_type=F32)

    def activate(c, slot):
        gbuf, ubuf = gbufs[slot], ubufs[slot]
        cw = cw_ref[c]
        conv = cb_ref[c] + cw[2:3, :] * gbuf[8:8 + tm, :]
        conv = conv + cw[1:2, :] * gbuf[7:7 + tm, :] + cw[0:1, :] * gbuf[6:6 + tm, :]
        act_scr[c] = (_gelu_tanh(conv) * ubuf[...]).astype(BF16)

    assert nch % 2 == 1
    project(0, 0)

    def step(k, carry):
        c = 2 * k
        project(c + 1, 1)
        activate(c, 0)
        project(c + 2, 0)
        activate(c + 1, 1)
        return carry

    lax.fori_loop(0, nch // 2, step, 0)
    activate(nch - 1, 0)
    act = jnp.concatenate([act_scr[c] for c in range(nch)], axis=1)
    down = jnp.dot(act, wd_ref[...], preferred_element_type=F32)
    out_ref[0] = h1_ref[0] + _rms(down, g_ref[...])


def _ffn(xn2, h1, w_up, conv_w, conv_b, w_down, g_post, seq, tm, tf):
    bsz, lp, _ = xn2.shape
    nch = D_FF // tf
    w_up_c = w_up.reshape(D_MODEL, 2 * nch, tf).transpose(1, 0, 2)
    conv_w_c = conv_w.reshape(FFN_CONV, nch, tf).transpose(1, 0, 2)
    conv_b_c = conv_b.reshape(nch, 1, tf)
    xblk = lambda b, i: (b, i, 0)
    meta = lambda b, i: (b, (lp - N_META) // N_META, 0)
    return pl.pallas_call(
        functools.partial(_ffn_kernel, tm=tm, nch=nch),
        grid=(bsz, seq // tm),
        in_specs=[
            pl.BlockSpec((1, tm, D_MODEL), xblk),
            pl.BlockSpec((1, N_META, D_MODEL), meta),
            pl.BlockSpec((1, tm, D_MODEL), xblk),
            _const_spec((2 * nch, D_MODEL, tf)),
            _const_spec((nch, FFN_CONV, tf)),
            _const_spec((nch, 1, tf)),
            _const_spec((D_FF, D_MODEL)),
            _const_spec((1, D_MODEL)),
        ],
        out_specs=pl.BlockSpec((1, tm, D_MODEL), xblk),
        out_shape=jax.ShapeDtypeStruct((bsz, seq, D_MODEL), F32),
        scratch_shapes=[
            pltpu.VMEM((tm + 8, tf), F32),
            pltpu.VMEM((tm + 8, tf), F32),
            pltpu.VMEM((tm, tf), F32),
            pltpu.VMEM((tm, tf), F32),
            pltpu.VMEM((nch, 8, tf), F32),
            pltpu.VMEM((nch, tm, tf), BF16),
        ],
        compiler_params=pltpu.CompilerParams(
            dimension_semantics=("arbitrary", "arbitrary"), vmem_limit_bytes=V7X_VMEM_LIMIT),
        name="ffn",
    )(xn2, xn2, h1, w_up_c, conv_w_c, conv_b_c, w_down, g_post)


def _pad_lanes(v, n):
    return jnp.pad(v, ((0, 0), (0, n - v.shape[-1])))


def kernel(x, meta_tokens, mix_pre_g, w_in, ssd_conv_w, ssd_conv_b, ssd_dt_bias, ssd_a_log, ssd_d,
           ssd_norm_g, sb_norm_g, w_out, mix_post_g, ffn_pre_g, w_up, ffn_conv_w, ffn_conv_b,
           w_down, ffn_post_g):
    bsz, seq, d = x.shape
    lp = seq + BLK
    depth = w_in.shape[0]
    assert depth == 1 and d == D_MODEL and seq % BLK == 0

    meta = jnp.broadcast_to(meta_tokens.astype(x.dtype)[None], (bsz, N_META, d))
    h = jnp.concatenate([x, jnp.zeros((bsz, PAD, d), x.dtype), meta], axis=1)
    h2d = h.reshape(bsz * lp, d)

    l = 0
    off_xbc = SSD_INNER
    off_dt = off_xbc + XBC_WIDTH
    off_q = off_dt + SSD_HEADS
    w = w_in[l]
    w_main = jnp.concatenate([w[:, :off_dt], w[:, off_q:]], axis=1).astype(BF16)
    w_dt = _pad_lanes(w[:, off_dt:off_q], DT_LANES).astype(BF16)

    tm_rows = lp // 8
    z, xbc, q, k, v, dt_raw = _inproj(h2d, mix_pre_g[l][None], w_main, w_dt, tm_rows)
    to3 = lambda a: a.reshape(bsz, lp, a.shape[-1])

    head_of_col = jnp.arange(SSD_INNER) // SSD_HEAD_DIM
    expand = (jnp.arange(DT_LANES)[:, None] == head_of_col[None, :]).astype(BF16)
    d_exp = jnp.repeat(ssd_d[l].astype(F32), SSD_HEAD_DIM)[None]
    y_ssd = _ssd(to3(xbc), to3(z), to3(dt_raw), ssd_conv_w[l], ssd_conv_b[l][None],
                 _pad_lanes(ssd_dt_bias[l][None], DT_LANES), _pad_lanes(ssd_a_log[l][None], DT_LANES),
                 d_exp, ssd_norm_g[l][None], expand)

    kk = jnp.arange(BLK)
    tt = (kk[:, None] > kk[None, :]).astype(BF16)
    o_sb = _sb_attention(to3(q), to3(k), to3(v), tt, SB_WIDTH // BLK)

    wo = w_out[l].astype(BF16)
    h1, xn2 = _outproj(y_ssd.reshape(bsz * lp, SSD_INNER), o_sb.reshape(bsz * lp, SB_WIDTH), h2d,
                       wo[:SSD_INNER], wo[SSD_INNER:], sb_norm_g[l][None], mix_post_g[l][None],
                       ffn_pre_g[l][None], tm_rows)

    return _ffn(to3(xn2), to3(h1), w_up[l].astype(BF16), ffn_conv_w[l], ffn_conv_b[l][None],
                w_down[l].astype(BF16), ffn_post_g[l][None], seq, 512, 256)
```

```python
import functools
import math

import jax
import jax.numpy as jnp
from jax import lax
from jax.experimental import pallas as pl
from jax.experimental.pallas import tpu as pltpu

F32 = jnp.float32
BF16 = jnp.bfloat16

D_MODEL = 1024
N_META = 16
BLK = 128
PAD = BLK - N_META
SSD_HEADS = 16
SSD_HEAD_DIM = 64
SSD_GROUPS = 2
SSD_STATE = 128
SSD_INNER = 1024
SSD_CONV = 4
XBC_WIDTH = SSD_INNER + 2 * SSD_GROUPS * SSD_STATE
SB_WIDTH = 1024
SB_HEAD_DIM = 64
D_FF = 2816
FFN_CONV = 3
EPS = 1e-6
DT_LANES = 128
GROUP_COLS = SSD_INNER // SSD_GROUPS
LOG2E = 1.4426950408889634

V7X_VMEM_LIMIT = 56 * 1024 * 1024


def _rms(x, g):
    return x * lax.rsqrt(jnp.mean(x * x, axis=-1, keepdims=True) + EPS) * g


def _const_spec(shape):
    nd = len(shape)
    return pl.BlockSpec(shape, lambda *_: (0,) * nd, pipeline_mode=pl.Buffered(1))


_PROJ_SPLITS = (("z", SSD_INNER), ("xbc", XBC_WIDTH), ("q", SB_WIDTH), ("k", SB_WIDTH), ("v", SB_WIDTH))


def _inproj_kernel(h_ref, g_ref, w_ref, wdt_ref, z_ref, xbc_ref, q_ref, k_ref, v0_ref, v1_ref, dt_ref):
    xn = _rms(h_ref[...], g_ref[...]).astype(BF16)
    lo = 0
    proj = {}
    for name, width in _PROJ_SPLITS:
        proj[name] = jnp.dot(xn, w_ref[:, lo:lo + width], preferred_element_type=F32)
        lo += width
    z_ref[...] = proj["z"].astype(BF16)
    xbc_ref[...] = proj["xbc"].astype(BF16)
    q_ref[...] = (proj["q"] * (LOG2E / math.sqrt(SB_HEAD_DIM))).astype(BF16)
    k_ref[...] = proj["k"].astype(BF16)
    v = proj["v"].astype(BF16)
    lane = lax.broadcasted_iota(jnp.int32, v.shape, 1)
    even_head = (lane & SB_HEAD_DIM) == 0
    zero = jnp.zeros_like(v)
    v0_ref[...] = jnp.where(even_head, v, zero)
    v1_ref[...] = jnp.where(even_head, zero, v)
    dt_ref[...] = jnp.dot(xn, wdt_ref[...], preferred_element_type=F32)


_INPROJ_OUT_WIDTHS = (SSD_INNER, XBC_WIDTH, SB_WIDTH, SB_WIDTH, SB_WIDTH, SB_WIDTH)


def _inproj(h2d, g, w_main, w_dt, tm):
    m = h2d.shape[0]
    n_main = w_main.shape[1]
    row = lambda i: (i, 0)
    outs = [jax.ShapeDtypeStruct((m, width), BF16) for width in _INPROJ_OUT_WIDTHS]
    outs.append(jax.ShapeDtypeStruct((m, DT_LANES), F32))
    out_specs = [pl.BlockSpec((tm, width), row) for width in _INPROJ_OUT_WIDTHS]
    out_specs.append(pl.BlockSpec((tm, DT_LANES), row))
    return pl.pallas_call(
        _inproj_kernel,
        grid=(m // tm,),
        in_specs=[
            pl.BlockSpec((tm, D_MODEL), row),
            _const_spec((1, D_MODEL)),
            _const_spec((D_MODEL, n_main)),
            _const_spec((D_MODEL, DT_LANES)),
        ],
        out_specs=out_specs,
        out_shape=outs,
        compiler_params=pltpu.CompilerParams(
            dimension_semantics=("arbitrary",), vmem_limit_bytes=V7X_VMEM_LIMIT),
        name="inproj",
    )(h2d, g, w_main, w_dt)


def _dot_f32_by_01(lhs, rhs, f32_side):
    x = lhs if f32_side == "lhs" else rhs
    p1 = x.astype(BF16)
    r1 = x - p1.astype(F32)
    p2 = r1.astype(BF16)
    p3 = (r1 - p2.astype(F32)).astype(BF16)
    out = None
    for piece in (p1, p2, p3):
        ops = (piece, rhs) if f32_side == "lhs" else (lhs, piece)
        term = jnp.dot(*ops, preferred_element_type=F32)
        out = term if out is None else out + term
    return out


def _ssd_kernel(xbc_ref, z_ref, dt_ref, cw_ref, cb_ref, dtb_ref, alog_ref, dexp_ref, g_ref, e_ref,
                out_ref, prev, state, *, nseq):
    c = pl.program_id(1)

    @pl.when(c == 0)
    def _():
        prev[...] = jnp.zeros_like(prev)
        state[...] = jnp.zeros_like(state)

    seqs = range(nseq)
    groups = range(SSD_GROUPS)
    row = lax.broadcasted_iota(jnp.int32, (BLK, 1), 0)
    real = jnp.logical_or(c > 0, row >= PAD)
    lane = lax.broadcasted_iota(jnp.int32, (BLK, DT_LANES), 1)
    dt_valid = jnp.logical_and(real, lane < SSD_HEADS)
    ri = lax.broadcasted_iota(jnp.int32, (BLK, BLK), 0)
    ci = lax.broadcasted_iota(jnp.int32, (BLK, BLK), 1)
    causal = ri >= ci
    tri = jnp.where(causal, 1.0, 0.0).astype(BF16)
    half = ci < SSD_HEAD_DIM
    cw = cw_ref[...]
    e = e_ref[...]
    a_neg = -jnp.exp(alog_ref[...])

    xc, dt, adt = [], [], []
    for s in seqs:
        x_raw = xbc_ref[s].astype(F32)
        prev[s, 8:8 + BLK, :] = x_raw
        acc = cb_ref[...] + cw[3:4, :] * x_raw
        for tap in range(SSD_CONV - 1):
            acc = acc + cw[tap:tap + 1, :] * prev[s, 5 + tap:5 + tap + BLK, :]
        prev[s, 0:8, :] = x_raw[BLK - 8:, :]
        xc.append(jnp.where(real, acc * jax.nn.sigmoid(acc), 0.0))
        dts = jax.nn.softplus(dt_ref[s] + dtb_ref[...])
        dt.append(jnp.where(dt_valid, dts, 0.0))
        adt.append(dt[s] * a_neg)
    xs = [xc[s][:, :SSD_INNER] for s in seqs]

    acs = [_dot_f32_by_01(tri, adt[s], f32_side="rhs") for s in seqs]
    acs_t = [acs[s].T for s in seqs]
    dt_exp = [_dot_f32_by_01(dt[s], e, f32_side="lhs") for s in seqs]
    acs_exp = [_dot_f32_by_01(acs[s], e, f32_side="lhs") for s in seqs]

    xdt_b, xdec, off_scale, chunk_decay, b_t, c_gb, b_gb = [], [], [], [], [], [], []
    for s in seqs:
        last = acs_exp[s][BLK - 1:BLK, :]
        xdt = xs[s] * dt_exp[s]
        xdec.append((xdt * jnp.exp(last - acs_exp[s])).astype(BF16))
        xdt_b.append(xdt.astype(BF16))
        off_scale.append(jnp.exp(acs_exp[s]))
        chunk_decay.append(jnp.exp(last))
        b_g = [xc[s][:, SSD_INNER + g * SSD_STATE:SSD_INNER + (g + 1) * SSD_STATE] for g in groups]
        c_g = [xc[s][:, SSD_INNER + (SSD_GROUPS + g) * SSD_STATE:
                     SSD_INNER + (SSD_GROUPS + g + 1) * SSD_STATE] for g in groups]
        b_gb.append([x.astype(BF16) for x in b_g])
        c_gb.append([x.astype(BF16) for x in c_g])
        b_t.append([x.T.astype(BF16) for x in b_g])

    gcols = [slice(g * GROUP_COLS, (g + 1) * GROUP_COLS) for g in groups]
    cb_mat = [[lax.dot_general(c_gb[s][g], b_gb[s][g], (((1,), (1,)), ((), ())),
                               preferred_element_type=F32) for g in groups] for s in seqs]
    s_prev = [[state[s, :, gcols[g]] for g in groups] for s in seqs]
    y_off = [[jnp.dot(c_gb[s][g], s_prev[s][g].astype(BF16), preferred_element_type=F32)
              for g in groups] for s in seqs]

    heads_per_group = SSD_HEADS // SSD_GROUPS
    decay_mats = [[None] * SSD_HEADS for _ in seqs]
    for s in seqs:
        for hh in range(SSD_HEADS):
            seg = acs[s][:, hh:hh + 1] - acs_t[s][hh:hh + 1, :]
            ldec = jnp.exp(jnp.where(causal, seg, -1e30))
            decay_mats[s][hh] = (cb_mat[s][hh // heads_per_group] * ldec).astype(BF16)

    y_diag = [[jnp.dot(decay_mats[s][hh], xdt_b[s][:, (hh // 2) * BLK:(hh // 2 + 1) * BLK],
                       preferred_element_type=F32) for hh in range(SSD_HEADS)] for s in seqs]
    new = [[jnp.dot(b_t[s][g], xdec[s][:, gcols[g]], preferred_element_type=F32)
            for g in groups] for s in seqs]

    for s in seqs:
        y_cols = []
        for g in groups:
            state[s, :, gcols[g]] = s_prev[s][g] * chunk_decay[s][:, gcols[g]] + new[s][g]
            yo = y_off[s][g] * off_scale[s][:, gcols[g]]
            for pair in range(GROUP_COLS // BLK):
                h0 = (g * GROUP_COLS + pair * BLK) // SSD_HEAD_DIM
                y_cols.append(jnp.where(half, y_diag[s][h0], y_diag[s][h0 + 1])
                              + yo[:, pair * BLK:(pair + 1) * BLK])
        y = jnp.concatenate(y_cols, axis=1) + xs[s] * dexp_ref[...]
        zf = z_ref[s].astype(F32)
        y = y * (zf * jax.nn.sigmoid(zf))
        out_ref[s] = _rms(y, g_ref[...]).astype(BF16)


SSD_SEQS_PER_STEP = 2


def _ssd(xbc, z, dt_raw, conv_w, conv_b, dt_bias, a_log, d_exp, norm_g, expand):
    bsz, lp, _ = xbc.shape
    nblk = lp // BLK
    nseq = SSD_SEQS_PER_STEP
    assert bsz % nseq == 0
    blk = lambda b, c: (b, (c + nblk - 1) % nblk, 0)
    return pl.pallas_call(
        functools.partial(_ssd_kernel, nseq=nseq),
        grid=(bsz // nseq, nblk),
        in_specs=[
            pl.BlockSpec((nseq, BLK, XBC_WIDTH), blk),
            pl.BlockSpec((nseq, BLK, SSD_INNER), blk),
            pl.BlockSpec((nseq, BLK, DT_LANES), blk),
            _const_spec((SSD_CONV, XBC_WIDTH)),
            _const_spec((1, XBC_WIDTH)),
            _const_spec((1, DT_LANES)),
            _const_spec((1, DT_LANES)),
            _const_spec((1, SSD_INNER)),
            _const_spec((1, SSD_INNER)),
            _const_spec((DT_LANES, SSD_INNER)),
        ],
        out_specs=pl.BlockSpec((nseq, BLK, SSD_INNER), blk),
        out_shape=jax.ShapeDtypeStruct((bsz, lp, SSD_INNER), BF16),
        scratch_shapes=[
            pltpu.VMEM((nseq, BLK + 8, XBC_WIDTH), F32),
            pltpu.VMEM((nseq, SSD_STATE, SSD_INNER), F32),
        ],
        compiler_params=pltpu.CompilerParams(
            dimension_semantics=("arbitrary", "arbitrary"), vmem_limit_bytes=V7X_VMEM_LIMIT),
        name="ssd",
    )(xbc, z, dt_raw, conv_w, conv_b, dt_bias, a_log, d_exp, norm_g, expand)


SB_DEAD_LOG2 = -151.0
SB_LEAD = 3


def _neg_abs(x):
    bits = pltpu.bitcast(x, jnp.uint32) | jnp.uint32(0x80000000)
    return pltpu.bitcast(bits, jnp.float32)


def _sb_kernel(q_ref, k_ref, v0_ref, v1_ref, tt_ref, o_ref, qm_scr, run_scr, acc_scr, worst_scr, *,
               nblk, npairs):
    qi = pl.program_id(2)
    lane = lax.broadcasted_iota(jnp.int32, (BLK, BLK), 1)
    sub = lax.broadcasted_iota(jnp.int32, (BLK, BLK), 0)
    first = lane < SB_HEAD_DIM
    strictly_older = lane < sub
    tt = tt_ref[...]

    for p in range(npairs):
        qp = q_ref[0, :, p * BLK:(p + 1) * BLK]
        zero = jnp.zeros_like(qp)
        qm_scr[p, :BLK, :] = jnp.where(first, qp, zero)
        qm_scr[p, BLK:, :] = jnp.where(first, zero, qp)

    def process(blocks, init, check):
        offs = [pl.multiple_of(((j + nblk - 1) % nblk) * BLK, BLK) for j in blocks]
        heads = range(2 * npairs)
        nb = range(len(blocks))
        scores = [[
            lax.dot_general(qm_scr[p], k_ref[0, pl.ds(offs[b], BLK), p * BLK:(p + 1) * BLK],
                            (((1,), (1,)), ((), ())), preferred_element_type=F32)
            for p in range(npairs)] for b in nb]
        log_betas = [[None] * len(heads) for _ in nb]
        log_keeps = [[None] * len(heads) for _ in nb]
        totals = [[None] * len(heads) for _ in nb]
        for b in nb:
            for h in heads:
                s = scores[b][h // 2][(h % 2) * BLK:(h % 2 + 1) * BLK, :]
                sp = jnp.log(1.0 + jnp.exp2(_neg_abs(s))) * LOG2E
                log_beta = jnp.minimum(s, 0.0) - sp
                log_keep = log_beta - s
                if init and b == 0:
                    log_keep = jnp.where(strictly_older, log_keep, 0.0)
                log_betas[b][h] = log_beta
                log_keeps[b][h] = log_keep.astype(BF16)
                totals[b][h] = jnp.sum(log_keep, axis=1, keepdims=True)
        sums = [jnp.dot(jnp.concatenate(log_keeps[b], axis=0), tt, preferred_element_type=F32)
                for b in nb]
        weights = [[None] * len(heads) for _ in nb]
        worst = None
        for h in heads:
            run = None if init else run_scr[h]
            for b in nb:
                after, total = sums[b][h * BLK:(h + 1) * BLK, :], totals[b][h]
                if run is None:
                    w = jnp.where(strictly_older, jnp.exp2(log_betas[b][h] + after), 0.0)
                    run = jnp.broadcast_to(total, (BLK, BLK))
                else:
                    w = jnp.exp2(log_betas[b][h] + (after + run))
                    run = run + total
                weights[b][h] = w.astype(BF16)
            run_scr[h] = run
            if check:
                worst = run if worst is None else jnp.maximum(worst, run)
        for p in range(npairs):
            pv = None if init else acc_scr[p]
            for b in nb:
                v_heads = jnp.concatenate(
                    [v0_ref[0, pl.ds(offs[b], BLK), p * BLK:(p + 1) * BLK],
                     v1_ref[0, pl.ds(offs[b], BLK), p * BLK:(p + 1) * BLK]], axis=0)
                w_heads = jnp.concatenate([weights[b][2 * p], weights[b][2 * p + 1]], axis=1)
                contrib = jnp.dot(w_heads, v_heads, preferred_element_type=F32)
                pv = contrib if pv is None else pv + contrib
            acc_scr[p] = pv
        return jnp.max(worst) if check else None

    for lead in range(1, SB_LEAD + 1):
        last = lead == SB_LEAD

        @pl.when(qi >= lead - 1 if last else qi == lead - 1)
        def _(lead=lead, last=last):
            worst = process([qi - i for i in range(lead)], True, last)
            worst_scr[0] = worst if last else jnp.float32(0.0)

    def cond(st):
        return jnp.logical_and(st[0] >= 0, st[1] > SB_DEAD_LOG2)

    def body(st):
        return st[0] - 1, process([st[0]], False, True)

    lax.while_loop(cond, body, (qi - SB_LEAD, worst_scr[0]))
    for p in range(npairs):
        o_ref[0, :, p * BLK:(p + 1) * BLK] = acc_scr[p].astype(BF16)


def _sb_attention(q, k, v0, v1, tt, npairs):
    bsz, lp, width = q.shape
    nblk = lp // BLK
    wblk = npairs * BLK
    qblk = lambda b, p, i: (b, (i + nblk - 1) % nblk, p)
    kv = lambda b, p, i: (b, 0, p)
    kv_spec = pl.BlockSpec((1, lp, wblk), kv, pipeline_mode=pl.Buffered(1))
    return pl.pallas_call(
        functools.partial(_sb_kernel, nblk=nblk, npairs=npairs),
        grid=(bsz, width // wblk, nblk),
        in_specs=[
            pl.BlockSpec((1, BLK, wblk), qblk),
            kv_spec,
            kv_spec,
            kv_spec,
            _const_spec((BLK, BLK)),
        ],
        out_specs=pl.BlockSpec((1, BLK, wblk), qblk),
        out_shape=jax.ShapeDtypeStruct((bsz, lp, width), BF16),
        scratch_shapes=[
            pltpu.VMEM((npairs, 2 * BLK, BLK), BF16),
            pltpu.VMEM((2 * npairs, BLK, BLK), F32),
            pltpu.VMEM((npairs, BLK, BLK), F32),
            pltpu.SMEM((1,), F32),
        ],
        compiler_params=pltpu.CompilerParams(
            dimension_semantics=("arbitrary", "arbitrary", "arbitrary"),
            vmem_limit_bytes=V7X_VMEM_LIMIT),
        name="sb_attention",
    )(q, k, v0, v1, tt)


def _outproj_kernel(ys_ref, o_ref, h_ref, wa_ref, wb_ref, gsb_ref, gpost_ref, gffn_ref,
                    h1_ref, xn2_ref, *, nsub):
    rows = [pl.ds(k * (h_ref.shape[0] // nsub), h_ref.shape[0] // nsub) for k in range(nsub)]
    osb = [_rms(o_ref[r, :].astype(F32), gsb_ref[...]).astype(BF16) for r in rows]
    mix = [jnp.dot(ys_ref[r, :], wa_ref[...], preferred_element_type=F32)
           + jnp.dot(osb[k], wb_ref[...], preferred_element_type=F32) for k, r in enumerate(rows)]
    for k, r in enumerate(rows):
        h1 = h_ref[r, :] + _rms(mix[k], gpost_ref[...])
        h1_ref[r, :] = h1
        xn2_ref[r, :] = _rms(h1, gffn_ref[...]).astype(BF16)


def _outproj(y_ssd, o_sb, h2d, w_a, w_b, g_sb, g_post, g_ffn, tm):
    m = h2d.shape[0]
    row = lambda i: (i, 0)
    return pl.pallas_call(
        functools.partial(_outproj_kernel, nsub=4),
        grid=(m // tm,),
        in_specs=[
            pl.BlockSpec((tm, SSD_INNER), row),
            pl.BlockSpec((tm, SB_WIDTH), row),
            pl.BlockSpec((tm, D_MODEL), row),
            _const_spec((SSD_INNER, D_MODEL)),
            _const_spec((SB_WIDTH, D_MODEL)),
            _const_spec((1, SB_WIDTH)),
            _const_spec((1, D_MODEL)),
            _const_spec((1, D_MODEL)),
        ],
        out_specs=[pl.BlockSpec((tm, D_MODEL), row), pl.BlockSpec((tm, D_MODEL), row)],
        out_shape=[jax.ShapeDtypeStruct((m, D_MODEL), F32), jax.ShapeDtypeStruct((m, D_MODEL), BF16)],
        compiler_params=pltpu.CompilerParams(
            dimension_semantics=("arbitrary",), vmem_limit_bytes=V7X_VMEM_LIMIT),
        name="outproj",
    )(y_ssd, o_sb, h2d, w_a, w_b, g_sb, g_post, g_ffn)


def _gelu_tanh(x):
    return 0.5 * x * (1.0 + jnp.tanh(math.sqrt(2.0 / math.pi) * (x + 0.044715 * (x * x * x))))


def _ffn_kernel(xn_ref, xm_ref, h1_ref, wup_ref, cw_ref, cb_ref, wd_ref, g_ref, out_ref,
                gbuf0, gbuf1, ubuf0, ubuf1, halo, act_scr, *, tm, nch):
    i = pl.program_id(1)
    gbufs, ubufs = (gbuf0, gbuf1), (ubuf0, ubuf1)

    @pl.when(i == 0)
    def _():
        def meta_halo(c, carry):
            gm = jnp.dot(xm_ref[0], wup_ref[c], preferred_element_type=F32)
            halo[c] = gm[N_META - 8:, :]
            return carry
        lax.fori_loop(0, nch, meta_halo, 0)

    def project(c, slot):
        gbuf, ubuf = gbufs[slot], ubufs[slot]
        xn = xn_ref[0]
        gate = jnp.dot(xn, wup_ref[c], preferred_element_type=F32)
        gbuf[0:8, :] = halo[c]
        gbuf[8:8 + tm, :] = gate
        halo[c] = gate[tm - 8:, :]
        ubuf[...] = jnp.dot(xn, wup_ref[nch + c], preferred_element_type=F32)

    def activate(c, slot):
        gbuf, ubuf = gbufs[slot], ubufs[slot]
        cw = cw_ref[c]
        conv = cb_ref[c] + cw[2:3, :] * gbuf[8:8 + tm, :]
        conv = conv + cw[1:2, :] * gbuf[7:7 + tm, :] + cw[0:1, :] * gbuf[6:6 + tm, :]
        act_scr[c] = (_gelu_tanh(conv) * ubuf[...]).astype(BF16)

    assert nch % 2 == 1
    project(0, 0)

    def step(k, carry):
        c = 2 * k
        project(c + 1, 1)
        activate(c, 0)
        project(c + 2, 0)
        activate(c + 1, 1)
        return carry

    for k in range(nch // 2):
        step(k, 0)
    activate(nch - 1, 0)
    act = jnp.concatenate([act_scr[c] for c in range(nch)], axis=1)
    down = jnp.dot(act, wd_ref[...], preferred_element_type=F32)
    out_ref[0] = h1_ref[0] + _rms(down, g_ref[...])


def _ffn(xn2, h1, w_up, conv_w, conv_b, w_down, g_post, seq, tm, tf):
    bsz, lp, _ = xn2.shape
    nch = D_FF // tf
    w_up_c = w_up.reshape(D_MODEL, 2 * nch, tf).transpose(1, 0, 2)
    conv_w_c = conv_w.reshape(FFN_CONV, nch, tf).transpose(1, 0, 2)
    conv_b_c = conv_b.reshape(nch, 1, tf)
    xblk = lambda b, i: (b, i, 0)
    meta = lambda b, i: (b, (lp - N_META) // N_META, 0)
    return pl.pallas_call(
        functools.partial(_ffn_kernel, tm=tm, nch=nch),
        grid=(bsz, seq // tm),
        in_specs=[
            pl.BlockSpec((1, tm, D_MODEL), xblk),
            pl.BlockSpec((1, N_META, D_MODEL), meta),
            pl.BlockSpec((1, tm, D_MODEL), xblk),
            _const_spec((2 * nch, D_MODEL, tf)),
            _const_spec((nch, FFN_CONV, tf)),
            _const_spec((nch, 1, tf)),
            _const_spec((D_FF, D_MODEL)),
            _const_spec((1, D_MODEL)),
        ],
        out_specs=pl.BlockSpec((1, tm, D_MODEL), xblk),
        out_shape=jax.ShapeDtypeStruct((bsz, seq, D_MODEL), F32),
        scratch_shapes=[
            pltpu.VMEM((tm + 8, tf), F32),
            pltpu.VMEM((tm + 8, tf), F32),
            pltpu.VMEM((tm, tf), F32),
            pltpu.VMEM((tm, tf), F32),
            pltpu.VMEM((nch, 8, tf), F32),
            pltpu.VMEM((nch, tm, tf), BF16),
        ],
        compiler_params=pltpu.CompilerParams(
            dimension_semantics=("arbitrary", "arbitrary"), vmem_limit_bytes=V7X_VMEM_LIMIT),
        name="ffn",
    )(xn2, xn2, h1, w_up_c, conv_w_c, conv_b_c, w_down, g_post)


def _pad_lanes(v, n):
    return jnp.pad(v, ((0, 0), (0, n - v.shape[-1])))


def kernel(x, meta_tokens, mix_pre_g, w_in, ssd_conv_w, ssd_conv_b, ssd_dt_bias, ssd_a_log, ssd_d,
           ssd_norm_g, sb_norm_g, w_out, mix_post_g, ffn_pre_g, w_up, ffn_conv_w, ffn_conv_b,
           w_down, ffn_post_g):
    bsz, seq, d = x.shape
    lp = seq + BLK
    depth = w_in.shape[0]
    assert depth == 1 and d == D_MODEL and seq % BLK == 0

    meta = jnp.broadcast_to(meta_tokens.astype(x.dtype)[None], (bsz, N_META, d))
    h = jnp.concatenate([x, jnp.zeros((bsz, PAD, d), x.dtype), meta], axis=1)
    h2d = h.reshape(bsz * lp, d)

    l = 0
    off_xbc = SSD_INNER
    off_dt = off_xbc + XBC_WIDTH
    off_q = off_dt + SSD_HEADS
    w = w_in[l]
    w_main = jnp.concatenate([w[:, :off_dt], w[:, off_q:]], axis=1).astype(BF16)
    w_dt = _pad_lanes(w[:, off_dt:off_q], DT_LANES).astype(BF16)

    tm_rows = lp // 8
    z, xbc, q, k, v0, v1, dt_raw = _inproj(h2d, mix_pre_g[l][None], w_main, w_dt, tm_rows)
    to3 = lambda a: a.reshape(bsz, lp, a.shape[-1])

    head_of_col = jnp.arange(SSD_INNER) // SSD_HEAD_DIM
    expand = (jnp.arange(DT_LANES)[:, None] == head_of_col[None, :]).astype(BF16)
    d_exp = jnp.repeat(ssd_d[l].astype(F32), SSD_HEAD_DIM)[None]
    y_ssd = _ssd(to3(xbc), to3(z), to3(dt_raw), ssd_conv_w[l], ssd_conv_b[l][None],
                 _pad_lanes(ssd_dt_bias[l][None], DT_LANES), _pad_lanes(ssd_a_log[l][None], DT_LANES),
                 d_exp, ssd_norm_g[l][None], expand)

    kk = jnp.arange(BLK)
    tt = (kk[:, None] > kk[None, :]).astype(BF16)
    o_sb = _sb_attention(to3(q), to3(k), to3(v0), to3(v1), tt, SB_WIDTH // BLK)

    wo = w_out[l].astype(BF16)
    h1, xn2 = _outproj(y_ssd.reshape(bsz * lp, SSD_INNER), o_sb.reshape(bsz * lp, SB_WIDTH), h2d,
                       wo[:SSD_INNER], wo[SSD_INNER:], sb_norm_g[l][None], mix_post_g[l][None],
                       ffn_pre_g[l][None], 1024)

    return _ffn(to3(xn2), to3(h1), w_up[l].astype(BF16), ffn_conv_w[l], ffn_conv_b[l][None],
                w_down[l].astype(BF16), ffn_post_g[l][None], seq, 512, 256)
```

```python
import functools
import math

import jax
import jax.numpy as jnp
from jax import lax
from jax.experimental import pallas as pl
from jax.experimental.pallas import tpu as pltpu

F32 = jnp.float32
BF16 = jnp.bfloat16

D_MODEL = 1024
N_META = 16
BLK = 128
PAD = BLK - N_META
SSD_HEADS = 16
SSD_HEAD_DIM = 64
SSD_GROUPS = 2
SSD_STATE = 128
SSD_INNER = 1024
SSD_CONV = 4
XBC_WIDTH = SSD_INNER + 2 * SSD_GROUPS * SSD_STATE
SB_WIDTH = 1024
SB_HEAD_DIM = 64
D_FF = 2816
FFN_CONV = 3
EPS = 1e-6
DT_LANES = 128
GROUP_COLS = SSD_INNER // SSD_GROUPS
LOG2E = 1.4426950408889634

V7X_VMEM_LIMIT = 56 * 1024 * 1024

INPROJ_TM = 512
OUTPROJ_TM = 1024
OUTPROJ_SUBTILES = 4
FFN_TM = 512
FFN_TF = 256
SSD_SEQS_PER_STEP = 2
SB_Q_BLOCKS_PER_STEP = 3


def _rms(x, g):
    return x * lax.rsqrt(jnp.mean(x * x, axis=-1, keepdims=True) + EPS) * g


def _const_spec(shape):
    nd = len(shape)
    return pl.BlockSpec(shape, lambda *_: (0,) * nd, pipeline_mode=pl.Buffered(1))


_PROJ_SPLITS = (("z", SSD_INNER), ("xbc", XBC_WIDTH), ("q", SB_WIDTH), ("k", SB_WIDTH), ("v", SB_WIDTH))
_INPROJ_OUT_WIDTHS = (SSD_INNER, XBC_WIDTH, SB_WIDTH, SB_WIDTH, SB_WIDTH, SB_WIDTH, DT_LANES)
_INPROJ_OUT_DTYPES = (BF16, BF16, BF16, BF16, BF16, BF16, F32)


def _inproj_rows(h, g_ref, w_ref, wdt_ref, out_refs):
    z_ref, xbc_ref, q_ref, k_ref, v0_ref, v1_ref, dt_ref = out_refs
    xn = _rms(h, g_ref[...]).astype(BF16)
    lo = 0
    proj = {}
    for name, width in _PROJ_SPLITS:
        proj[name] = jnp.dot(xn, w_ref[:, lo:lo + width], preferred_element_type=F32)
        lo += width
    z_ref[0] = proj["z"].astype(BF16)
    xbc_ref[0] = proj["xbc"].astype(BF16)
    q_ref[0] = (proj["q"] * (LOG2E / math.sqrt(SB_HEAD_DIM))).astype(BF16)
    k_ref[0] = proj["k"].astype(BF16)
    v = proj["v"].astype(BF16)
    lane = lax.broadcasted_iota(jnp.int32, v.shape, 1)
    even_head = (lane & SB_HEAD_DIM) == 0
    zero = jnp.zeros_like(v)
    v0_ref[0] = jnp.where(even_head, v, zero)
    v1_ref[0] = jnp.where(even_head, zero, v)
    dt_ref[0] = jnp.dot(xn, wdt_ref[...], preferred_element_type=F32)


def _inproj_kernel(x_ref, g_ref, w_ref, wdt_ref, *out_refs):
    _inproj_rows(x_ref[...], g_ref, w_ref, wdt_ref, out_refs)


def _inproj_meta_kernel(hm_ref, g_ref, w_ref, wdt_ref, *refs):
    nout = len(_INPROJ_OUT_WIDTHS)
    _inproj_rows(hm_ref[...], g_ref, w_ref, wdt_ref, refs[nout:])


def _inproj(x, hm, g, w_main, w_dt):
    bsz, seq, _ = x.shape
    lp = seq + BLK
    tm = INPROJ_TM
    per_seq = seq // tm
    n_main = w_main.shape[1]
    out_shape = [jax.ShapeDtypeStruct((bsz, lp, width), dt)
                 for width, dt in zip(_INPROJ_OUT_WIDTHS, _INPROJ_OUT_DTYPES)]
    weights = [_const_spec((1, D_MODEL)), _const_spec((D_MODEL, n_main)), _const_spec((D_MODEL, DT_LANES))]
    params = pltpu.CompilerParams(dimension_semantics=("arbitrary",), vmem_limit_bytes=V7X_VMEM_LIMIT)

    tile = lambda t: (t // per_seq, t % per_seq, 0)
    outs = pl.pallas_call(
        _inproj_kernel,
        grid=(bsz * per_seq,),
        in_specs=[pl.BlockSpec((tm, D_MODEL), lambda t: (t, 0))] + weights,
        out_specs=[pl.BlockSpec((1, tm, width), tile) for width in _INPROJ_OUT_WIDTHS],
        out_shape=out_shape,
        compiler_params=params,
        name="inproj",
    )(x.reshape(bsz * seq, D_MODEL), g, w_main, w_dt)

    nout = len(outs)
    last_blk = lambda b: (b, lp // BLK - 1, 0)
    return pl.pallas_call(
        _inproj_meta_kernel,
        grid=(bsz,),
        in_specs=[_const_spec((BLK, D_MODEL))] + weights + [pl.BlockSpec(memory_space=pl.ANY)] * nout,
        out_specs=[pl.BlockSpec((1, BLK, width), last_blk) for width in _INPROJ_OUT_WIDTHS],
        out_shape=out_shape,
        input_output_aliases={4 + j: j for j in range(nout)},
        compiler_params=params,
        name="inproj_meta",
    )(hm, g, w_main, w_dt, *outs)


def _dot_f32_by_01(lhs, rhs, f32_side):
    x = lhs if f32_side == "lhs" else rhs
    p1 = x.astype(BF16)
    r1 = x - p1.astype(F32)
    p2 = r1.astype(BF16)
    p3 = (r1 - p2.astype(F32)).astype(BF16)
    out = None
    for piece in (p1, p2, p3):
        ops = (piece, rhs) if f32_side == "lhs" else (lhs, piece)
        term = jnp.dot(*ops, preferred_element_type=F32)
        out = term if out is None else out + term
    return out


def _ssd_kernel(xbc_ref, z_ref, dt_ref, cw_ref, cb_ref, dtb_ref, alog_ref, dexp_ref, g_ref, e_ref,
                out_ref, prev, state, *, nseq):
    c = pl.program_id(1)

    @pl.when(c == 0)
    def _():
        prev[...] = jnp.zeros_like(prev)
        state[...] = jnp.zeros_like(state)

    seqs = range(nseq)
    groups = range(SSD_GROUPS)
    row = lax.broadcasted_iota(jnp.int32, (BLK, 1), 0)
    real = jnp.logical_or(c > 0, row >= PAD)
    lane = lax.broadcasted_iota(jnp.int32, (BLK, DT_LANES), 1)
    dt_valid = jnp.logical_and(real, lane < SSD_HEADS)
    ri = lax.broadcasted_iota(jnp.int32, (BLK, BLK), 0)
    ci = lax.broadcasted_iota(jnp.int32, (BLK, BLK), 1)
    causal = ri >= ci
    tri = jnp.where(causal, 1.0, 0.0).astype(BF16)
    half = ci < SSD_HEAD_DIM
    cw = cw_ref[...]
    e = e_ref[...]
    a_neg = -jnp.exp(alog_ref[...])

    xc, dt, adt = [], [], []
    for s in seqs:
        x_raw = xbc_ref[s].astype(F32)
        prev[s, 8:8 + BLK, :] = x_raw
        acc = cb_ref[...] + cw[3:4, :] * x_raw
        for tap in range(SSD_CONV - 1):
            acc = acc + cw[tap:tap + 1, :] * prev[s, 5 + tap:5 + tap + BLK, :]
        prev[s, 0:8, :] = x_raw[BLK - 8:, :]
        xc.append(jnp.where(real, acc * jax.nn.sigmoid(acc), 0.0))
        dts = jax.nn.softplus(dt_ref[s] + dtb_ref[...])
        dt.append(jnp.where(dt_valid, dts, 0.0))
        adt.append(dt[s] * a_neg)
    xs = [xc[s][:, :SSD_INNER] for s in seqs]

    acs = [_dot_f32_by_01(tri, adt[s], f32_side="rhs") for s in seqs]
    acs_t = [acs[s].T for s in seqs]
    dt_exp = [_dot_f32_by_01(dt[s], e, f32_side="lhs") for s in seqs]
    acs_exp = [_dot_f32_by_01(acs[s], e, f32_side="lhs") for s in seqs]

    xdt_b, xdec, off_scale, chunk_decay, b_t, c_gb, b_gb = [], [], [], [], [], [], []
    for s in seqs:
        last = acs_exp[s][BLK - 1:BLK, :]
        xdt = xs[s] * dt_exp[s]
        xdec.append((xdt * jnp.exp(last - acs_exp[s])).astype(BF16))
        xdt_b.append(xdt.astype(BF16))
        off_scale.append(jnp.exp(acs_exp[s]))
        chunk_decay.append(jnp.exp(last))
        b_g = [xc[s][:, SSD_INNER + g * SSD_STATE:SSD_INNER + (g + 1) * SSD_STATE] for g in groups]
        c_g = [xc[s][:, SSD_INNER + (SSD_GROUPS + g) * SSD_STATE:
                     SSD_INNER + (SSD_GROUPS + g + 1) * SSD_STATE] for g in groups]
        b_gb.append([x.astype(BF16) for x in b_g])
        c_gb.append([x.astype(BF16) for x in c_g])
        b_t.append([x.T.astype(BF16) for x in b_g])

    gcols = [slice(g * GROUP_COLS, (g + 1) * GROUP_COLS) for g in groups]
    cb_mat = [[lax.dot_general(c_gb[s][g], b_gb[s][g], (((1,), (1,)), ((), ())),
                               preferred_element_type=F32) for g in groups] for s in seqs]
    s_prev = [[state[s, :, gcols[g]] for g in groups] for s in seqs]
    y_off = [[jnp.dot(c_gb[s][g], s_prev[s][g].astype(BF16), preferred_element_type=F32)
              for g in groups] for s in seqs]

    heads_per_group = SSD_HEADS // SSD_GROUPS
    decay_mats = [[None] * SSD_HEADS for _ in seqs]
    for s in seqs:
        for hh in range(SSD_HEADS):
            seg = acs[s][:, hh:hh + 1] - acs_t[s][hh:hh + 1, :]
            ldec = jnp.exp(jnp.where(causal, seg, -1e30))
            decay_mats[s][hh] = (cb_mat[s][hh // heads_per_group] * ldec).astype(BF16)

    y_diag = [[jnp.dot(decay_mats[s][hh], xdt_b[s][:, (hh // 2) * BLK:(hh // 2 + 1) * BLK],
                       preferred_element_type=F32) for hh in range(SSD_HEADS)] for s in seqs]
    new = [[jnp.dot(b_t[s][g], xdec[s][:, gcols[g]], preferred_element_type=F32)
            for g in groups] for s in seqs]

    for s in seqs:
        y_cols = []
        for g in groups:
            state[s, :, gcols[g]] = s_prev[s][g] * chunk_decay[s][:, gcols[g]] + new[s][g]
            yo = y_off[s][g] * off_scale[s][:, gcols[g]]
            for pair in range(GROUP_COLS // BLK):
                h0 = (g * GROUP_COLS + pair * BLK) // SSD_HEAD_DIM
                y_cols.append(jnp.where(half, y_diag[s][h0], y_diag[s][h0 + 1])
                              + yo[:, pair * BLK:(pair + 1) * BLK])
        y = jnp.concatenate(y_cols, axis=1) + xs[s] * dexp_ref[...]
        zf = z_ref[s].astype(F32)
        y = y * (zf * jax.nn.sigmoid(zf))
        out_ref[s] = _rms(y, g_ref[...]).astype(BF16)


def _ssd(xbc, z, dt_raw, conv_w, conv_b, dt_bias, a_log, d_exp, norm_g, expand):
    bsz, lp, _ = xbc.shape
    nblk = lp // BLK
    nseq = SSD_SEQS_PER_STEP
    assert bsz % nseq == 0
    blk = lambda b, c: (b, (c + nblk - 1) % nblk, 0)
    return pl.pallas_call(
        functools.partial(_ssd_kernel, nseq=nseq),
        grid=(bsz // nseq, nblk),
        in_specs=[
            pl.BlockSpec((nseq, BLK, XBC_WIDTH), blk),
            pl.BlockSpec((nseq, BLK, SSD_INNER), blk),
            pl.BlockSpec((nseq, BLK, DT_LANES), blk),
            _const_spec((SSD_CONV, XBC_WIDTH)),
            _const_spec((1, XBC_WIDTH)),
            _const_spec((1, DT_LANES)),
            _const_spec((1, DT_LANES)),
            _const_spec((1, SSD_INNER)),
            _const_spec((1, SSD_INNER)),
            _const_spec((DT_LANES, SSD_INNER)),
        ],
        out_specs=pl.BlockSpec((nseq, BLK, SSD_INNER), blk),
        out_shape=jax.ShapeDtypeStruct((bsz, lp, SSD_INNER), BF16),
        scratch_shapes=[
            pltpu.VMEM((nseq, BLK + 8, XBC_WIDTH), F32),
            pltpu.VMEM((nseq, SSD_STATE, SSD_INNER), F32),
        ],
        compiler_params=pltpu.CompilerParams(
            dimension_semantics=("arbitrary", "arbitrary"), vmem_limit_bytes=V7X_VMEM_LIMIT),
        name="ssd",
    )(xbc, z, dt_raw, conv_w, conv_b, dt_bias, a_log, d_exp, norm_g, expand)


SB_DEAD_LOG2 = -151.0
SB_LEAD = 3


def _sb_kernel(q_ref, k_ref, v0_ref, v1_ref, tt_ref, o_ref, qm_scr, run_scr, acc_scr, worst_scr, *,
               nblk, npairs, qpb):
    lane = lax.broadcasted_iota(jnp.int32, (BLK, BLK), 1)
    sub = lax.broadcasted_iota(jnp.int32, (BLK, BLK), 0)
    first = lane < SB_HEAD_DIM
    strictly_older = lane < sub
    tt = tt_ref[...]

    def query_block(r, carry):
        qi = (pl.program_id(1) * qpb + r + 1) % nblk
        rows = pl.ds(pl.multiple_of(r * BLK, BLK), BLK)
        for p in range(npairs):
            qp = q_ref[0, rows, p * BLK:(p + 1) * BLK]
            zero = jnp.zeros_like(qp)
            qm_scr[p, :BLK, :] = jnp.where(first, qp, zero)
            qm_scr[p, BLK:, :] = jnp.where(first, zero, qp)

        def process(blocks, init, check):
            offs = [pl.multiple_of(((j + nblk - 1) % nblk) * BLK, BLK) for j in blocks]
            heads = range(2 * npairs)
            nb = range(len(blocks))
            scores = [[
                lax.dot_general(qm_scr[p], k_ref[0, pl.ds(offs[b], BLK), p * BLK:(p + 1) * BLK],
                                (((1,), (1,)), ((), ())), preferred_element_type=F32)
                for p in range(npairs)] for b in nb]
            log_betas = [[None] * len(heads) for _ in nb]
            log_keeps = [[None] * len(heads) for _ in nb]
            totals = [[None] * len(heads) for _ in nb]
            for b in nb:
                for h in heads:
                    s = scores[b][h // 2][(h % 2) * BLK:(h % 2 + 1) * BLK, :]
                    neg_part = jnp.minimum(s, 0.0)
                    neg_pos = neg_part - s
                    sp = jnp.log(1.0 + jnp.exp2(neg_part + neg_pos)) * LOG2E
                    log_beta = neg_part - sp
                    log_keep = neg_pos - sp
                    if init and b == 0:
                        log_keep = jnp.where(strictly_older, log_keep, 0.0)
                    log_betas[b][h] = log_beta
                    log_keeps[b][h] = log_keep.astype(BF16)
                    totals[b][h] = jnp.sum(log_keep, axis=1, keepdims=True)
            sums = [jnp.dot(jnp.concatenate(log_keeps[b], axis=0), tt, preferred_element_type=F32)
                    for b in nb]
            weights = [[None] * len(heads) for _ in nb]
            worst = None
            for h in heads:
                run = None if init else run_scr[h]
                for b in nb:
                    after, total = sums[b][h * BLK:(h + 1) * BLK, :], totals[b][h]
                    if run is None:
                        w = jnp.where(strictly_older, jnp.exp2(log_betas[b][h] + after), 0.0)
                        run = jnp.broadcast_to(total, (BLK, BLK))
                    else:
                        w = jnp.exp2(log_betas[b][h] + (after + run))
                        run = run + total
                    weights[b][h] = w.astype(BF16)
                run_scr[h] = run
                if check:
                    worst = run if worst is None else jnp.maximum(worst, run)
            for p in range(npairs):
                pv = None if init else acc_scr[p]
                for b in nb:
                    v_heads = jnp.concatenate(
                        [v0_ref[0, pl.ds(offs[b], BLK), p * BLK:(p + 1) * BLK],
                         v1_ref[0, pl.ds(offs[b], BLK), p * BLK:(p + 1) * BLK]], axis=0)
                    w_heads = jnp.concatenate([weights[b][2 * p], weights[b][2 * p + 1]], axis=1)
                    contrib = jnp.dot(w_heads, v_heads, preferred_element_type=F32)
                    pv = contrib if pv is None else pv + contrib
                acc_scr[p] = pv
            return jnp.max(worst) if check else None

        for lead in range(1, SB_LEAD + 1):
            last = lead == SB_LEAD

            @pl.when(qi >= lead - 1 if last else qi == lead - 1)
            def _(lead=lead, last=last):
                worst = process([qi - i for i in range(lead)], True, last)
                worst_scr[0] = worst if last else jnp.float32(0.0)

        def cond(st):
            return jnp.logical_and(st[0] >= 0, st[1] > SB_DEAD_LOG2)

        def body(st):
            return st[0] - 1, process([st[0]], False, True)

        lax.while_loop(cond, body, (qi - SB_LEAD, worst_scr[0]))
        for p in range(npairs):
            o_ref[0, rows, p * BLK:(p + 1) * BLK] = acc_scr[p].astype(BF16)
        return carry

    lax.fori_loop(0, qpb, query_block, 0)


def _sb_attention(q, k, v0, v1, tt):
    bsz, lp, width = q.shape
    nblk = lp // BLK
    npairs = width // BLK
    qpb = SB_Q_BLOCKS_PER_STEP
    assert nblk % qpb == 0
    qblk = lambda b, g: (b, g, 0)
    kv_spec = pl.BlockSpec((1, lp, width), lambda b, g: (b, 0, 0), pipeline_mode=pl.Buffered(1))
    return pl.pallas_call(
        functools.partial(_sb_kernel, nblk=nblk, npairs=npairs, qpb=qpb),
        grid=(bsz, nblk // qpb),
        in_specs=[
            pl.BlockSpec((1, qpb * BLK, width), qblk),
            kv_spec,
            kv_spec,
            kv_spec,
            _const_spec((BLK, BLK)),
        ],
        out_specs=pl.BlockSpec((1, qpb * BLK, width), qblk),
        out_shape=jax.ShapeDtypeStruct((bsz, lp, width), BF16),
        scratch_shapes=[
            pltpu.VMEM((npairs, 2 * BLK, BLK), BF16),
            pltpu.VMEM((2 * npairs, BLK, BLK), F32),
            pltpu.VMEM((npairs, BLK, BLK), F32),
            pltpu.SMEM((1,), F32),
        ],
        compiler_params=pltpu.CompilerParams(
            dimension_semantics=("arbitrary", "arbitrary"), vmem_limit_bytes=V7X_VMEM_LIMIT),
        name="sb_attention",
    )(q, k, v0, v1, tt)


def _outproj_rows(ys, o, h, wa_ref, wb_ref, gsb_ref, gpost_ref, gffn_ref):
    osb = _rms(o.astype(F32), gsb_ref[...]).astype(BF16)
    mix = jnp.dot(ys, wa_ref[...], preferred_element_type=F32)
    mix = mix + jnp.dot(osb, wb_ref[...], preferred_element_type=F32)
    h1 = h + _rms(mix, gpost_ref[...])
    return h1, _rms(h1, gffn_ref[...]).astype(BF16)


def _outproj_kernel(ys_ref, o_ref, h_ref, wa_ref, wb_ref, gsb_ref, gpost_ref, gffn_ref,
                    h1_ref, xn2_ref, *, nsub):
    sub = h_ref.shape[1] // nsub
    rows = [pl.ds(k * sub, sub) for k in range(nsub)]
    osb = [_rms(o_ref[0, r, :].astype(F32), gsb_ref[...]).astype(BF16) for r in rows]
    mix = [jnp.dot(ys_ref[0, r, :], wa_ref[...], preferred_element_type=F32)
           + jnp.dot(osb[k], wb_ref[...], preferred_element_type=F32) for k, r in enumerate(rows)]
    for k, r in enumerate(rows):
        h1 = h_ref[0, r, :] + _rms(mix[k], gpost_ref[...])
        h1_ref[0, r, :] = h1
        xn2_ref[0, r, :] = _rms(h1, gffn_ref[...]).astype(BF16)


def _outproj_meta_kernel(ys_ref, o_ref, hm_ref, wa_ref, wb_ref, gsb_ref, gpost_ref, gffn_ref, xn2_ref):
    _, xn2 = _outproj_rows(ys_ref[0], o_ref[0], hm_ref[...], wa_ref, wb_ref, gsb_ref, gpost_ref, gffn_ref)
    xn2_ref[0] = xn2


def _outproj(y_ssd, o_sb, x, hm, w_a, w_b, g_sb, g_post, g_ffn):
    bsz, seq, _ = x.shape
    lp = y_ssd.shape[1]
    tm = OUTPROJ_TM
    per_seq = seq // tm
    weights = [
        _const_spec((SSD_INNER, D_MODEL)),
        _const_spec((SB_WIDTH, D_MODEL)),
        _const_spec((1, SB_WIDTH)),
        _const_spec((1, D_MODEL)),
        _const_spec((1, D_MODEL)),
    ]
    params = pltpu.CompilerParams(dimension_semantics=("arbitrary",), vmem_limit_bytes=V7X_VMEM_LIMIT)
    tile = lambda t: (t // per_seq, t % per_seq, 0)
    h1, xn2 = pl.pallas_call(
        functools.partial(_outproj_kernel, nsub=OUTPROJ_SUBTILES),
        grid=(bsz * per_seq,),
        in_specs=[
            pl.BlockSpec((1, tm, SSD_INNER), tile),
            pl.BlockSpec((1, tm, SB_WIDTH), tile),
            pl.BlockSpec((1, tm, D_MODEL), tile),
        ] + weights,
        out_specs=[pl.BlockSpec((1, tm, D_MODEL), tile), pl.BlockSpec((1, tm, D_MODEL), tile)],
        out_shape=[jax.ShapeDtypeStruct((bsz, seq, D_MODEL), F32),
                   jax.ShapeDtypeStruct((bsz, seq, D_MODEL), BF16)],
        compiler_params=params,
        name="outproj",
    )(y_ssd, o_sb, x, w_a, w_b, g_sb, g_post, g_ffn)

    last_blk = lambda b: (b, lp // BLK - 1, 0)
    xn2_meta = pl.pallas_call(
        _outproj_meta_kernel,
        grid=(bsz,),
        in_specs=[
            pl.BlockSpec((1, BLK, SSD_INNER), last_blk),
            pl.BlockSpec((1, BLK, SB_WIDTH), last_blk),
            _const_spec((BLK, D_MODEL)),
        ] + weights,
        out_specs=pl.BlockSpec((1, BLK, D_MODEL), lambda b: (b, 0, 0)),
        out_shape=jax.ShapeDtypeStruct((bsz, BLK, D_MODEL), BF16),
        compiler_params=params,
        name="outproj_meta",
    )(y_ssd, o_sb, hm, w_a, w_b, g_sb, g_post, g_ffn)
    return h1, xn2, xn2_meta


def _gelu_tanh(x):
    return 0.5 * x * (1.0 + jnp.tanh(math.sqrt(2.0 / math.pi) * (x + 0.044715 * (x * x * x))))


def _ffn_kernel(xn_ref, xm_ref, h1_ref, wup_ref, cw_ref, cb_ref, wd_ref, g_ref, out_ref,
                gbuf0, gbuf1, ubuf0, ubuf1, halo, act_scr, *, tm, nch):
    i = pl.program_id(1)
    gbufs, ubufs = (gbuf0, gbuf1), (ubuf0, ubuf1)

    @pl.when(i == 0)
    def _():
        def meta_halo(c, carry):
            gm = jnp.dot(xm_ref[0], wup_ref[c], preferred_element_type=F32)
            halo[c] = gm[N_META - 8:, :]
            return carry
        lax.fori_loop(0, nch, meta_halo, 0)

    def project(c, slot):
        gbuf, ubuf = gbufs[slot], ubufs[slot]
        xn = xn_ref[0]
        gate = jnp.dot(xn, wup_ref[c], preferred_element_type=F32)
        gbuf[0:8, :] = halo[c]
        gbuf[8:8 + tm, :] = gate
        halo[c] = gate[tm - 8:, :]
        ubuf[...] = jnp.dot(xn, wup_ref[nch + c], preferred_element_type=F32)

    def activate(c, slot):
        gbuf, ubuf = gbufs[slot], ubufs[slot]
        cw = cw_ref[c]
        conv = cb_ref[c] + cw[2:3, :] * gbuf[8:8 + tm, :]
        conv = conv + cw[1:2, :] * gbuf[7:7 + tm, :] + cw[0:1, :] * gbuf[6:6 + tm, :]
        act_scr[c] = (_gelu_tanh(conv) * ubuf[...]).astype(BF16)

    project(0, 0)
    for c in range(nch - 1):
        project(c + 1, (c + 1) % 2)
        activate(c, c % 2)
    activate(nch - 1, (nch - 1) % 2)
    act = jnp.concatenate([act_scr[c] for c in range(nch)], axis=1)
    down = jnp.dot(act, wd_ref[...], preferred_element_type=F32)
    out_ref[0] = h1_ref[0] + _rms(down, g_ref[...])


def _ffn(xn2, xn2_meta, h1, w_up, conv_w, conv_b, w_down, g_post):
    bsz, seq, _ = xn2.shape
    tm, tf = FFN_TM, FFN_TF
    nch = D_FF // tf
    w_up_c = w_up.reshape(D_MODEL, 2 * nch, tf).transpose(1, 0, 2)
    conv_w_c = conv_w.reshape(FFN_CONV, nch, tf).transpose(1, 0, 2)
    conv_b_c = conv_b.reshape(nch, 1, tf)
    xblk = lambda b, i: (b, i, 0)
    meta = lambda b, i: (b, BLK // N_META - 1, 0)
    return pl.pallas_call(
        functools.partial(_ffn_kernel, tm=tm, nch=nch),
        grid=(bsz, seq // tm),
        in_specs=[
            pl.BlockSpec((1, tm, D_MODEL), xblk),
            pl.BlockSpec((1, N_META, D_MODEL), meta),
            pl.BlockSpec((1, tm, D_MODEL), xblk),
            _const_spec((2 * nch, D_MODEL, tf)),
            _const_spec((nch, FFN_CONV, tf)),
            _const_spec((nch, 1, tf)),
            _const_spec((D_FF, D_MODEL)),
            _const_spec((1, D_MODEL)),
        ],
        out_specs=pl.BlockSpec((1, tm, D_MODEL), xblk),
        out_shape=jax.ShapeDtypeStruct((bsz, seq, D_MODEL), F32),
        scratch_shapes=[
            pltpu.VMEM((tm + 8, tf), F32),
            pltpu.VMEM((tm + 8, tf), F32),
            pltpu.VMEM((tm, tf), F32),
            pltpu.VMEM((tm, tf), F32),
            pltpu.VMEM((nch, 8, tf), F32),
            pltpu.VMEM((nch, tm, tf), BF16),
        ],
        compiler_params=pltpu.CompilerParams(
            dimension_semantics=("arbitrary", "arbitrary"), vmem_limit_bytes=V7X_VMEM_LIMIT),
        name="ffn",
    )(xn2, xn2_meta, h1, w_up_c, conv_w_c, conv_b_c, w_down, g_post)


def _pad_lanes(v, n):
    return jnp.pad(v, ((0, 0), (0, n - v.shape[-1])))


def kernel(x, meta_tokens, mix_pre_g, w_in, ssd_conv_w, ssd_conv_b, ssd_dt_bias, ssd_a_log, ssd_d,
           ssd_norm_g, sb_norm_g, w_out, mix_post_g, ffn_pre_g, w_up, ffn_conv_w, ffn_conv_b,
           w_down, ffn_post_g):
    bsz, seq, d = x.shape
    depth = w_in.shape[0]
    assert depth == 1 and d == D_MODEL and seq % OUTPROJ_TM == 0

    hm = jnp.concatenate([jnp.zeros((PAD, d), x.dtype), meta_tokens.astype(x.dtype)], axis=0)

    l = 0
    off_xbc = SSD_INNER
    off_dt = off_xbc + XBC_WIDTH
    off_q = off_dt + SSD_HEADS
    w = w_in[l]
    w_main = jnp.concatenate([w[:, :off_dt], w[:, off_q:]], axis=1).astype(BF16)
    w_dt = _pad_lanes(w[:, off_dt:off_q], DT_LANES).astype(BF16)
    z, xbc, q, k, v0, v1, dt_raw = _inproj(x, hm, mix_pre_g[l][None], w_main, w_dt)

    head_of_col = jnp.arange(SSD_INNER) // SSD_HEAD_DIM
    expand = (jnp.arange(DT_LANES)[:, None] == head_of_col[None, :]).astype(BF16)
    d_exp = jnp.repeat(ssd_d[l].astype(F32), SSD_HEAD_DIM)[None]
    y_ssd = _ssd(xbc, z, dt_raw, ssd_conv_w[l], ssd_conv_b[l][None],
                 _pad_lanes(ssd_dt_bias[l][None], DT_LANES), _pad_lanes(ssd_a_log[l][None], DT_LANES),
                 d_exp, ssd_norm_g[l][None], expand)

    kk = jnp.arange(BLK)
    tt = (kk[:, None] > kk[None, :]).astype(BF16)
    o_sb = _sb_attention(q, k, v0, v1, tt)

    wo = w_out[l].astype(BF16)
    h1, xn2, xn2_meta = _outproj(y_ssd, o_sb, x, hm, wo[:SSD_INNER], wo[SSD_INNER:], sb_norm_g[l][None],
                                 mix_post_g[l][None], ffn_pre_g[l][None])

    return _ffn(xn2, xn2_meta, h1, w_up[l].astype(BF16), ffn_conv_w[l], ffn_conv_b[l][None],
                w_down[l].astype(BF16), ffn_post_g[l][None])
```

```python
import functools
import math

import jax
import jax.numpy as jnp
from jax import lax
from jax.experimental import pallas as pl
from jax.experimental.pallas import tpu as pltpu

F32 = jnp.float32
BF16 = jnp.bfloat16

D_MODEL = 1024
N_META = 16
BLK = 128
PAD = BLK - N_META
SSD_HEADS = 16
SSD_HEAD_DIM = 64
SSD_GROUPS = 2
SSD_STATE = 128
SSD_INNER = 1024
SSD_CONV = 4
XBC_WIDTH = SSD_INNER + 2 * SSD_GROUPS * SSD_STATE
SB_WIDTH = 1024
SB_HEAD_DIM = 64
D_FF = 2816
FFN_CONV = 3
EPS = 1e-6
DT_LANES = 128
GROUP_COLS = SSD_INNER // SSD_GROUPS
LOG2E = 1.4426950408889634

V7X_VMEM_LIMIT = 56 * 1024 * 1024

INPROJ_TM = 512
OUTPROJ_TM = 1024
OUTPROJ_SUBTILES = 4
FFN_TM = 512
FFN_TF = 256
SSD_SEQS_PER_STEP = 2
SB_Q_BLOCKS_PER_STEP = 3


def _rms(x, g):
    return x * lax.rsqrt(jnp.mean(x * x, axis=-1, keepdims=True) + EPS) * g


def _const_spec(shape):
    nd = len(shape)
    return pl.BlockSpec(shape, lambda *_: (0,) * nd, pipeline_mode=pl.Buffered(1))


_PROJ_SPLITS = (("z", SSD_INNER), ("xbc", XBC_WIDTH), ("q", SB_WIDTH), ("k", SB_WIDTH), ("v", SB_WIDTH))
_INPROJ_OUT_WIDTHS = (SSD_INNER, XBC_WIDTH, SB_WIDTH, SB_WIDTH, SB_WIDTH, SB_WIDTH, DT_LANES)
_INPROJ_OUT_DTYPES = (BF16, BF16, BF16, BF16, BF16, BF16, F32)


def _inproj_rows(h, g_ref, w_ref, wdt_ref, out_refs):
    z_ref, xbc_ref, q_ref, k_ref, v0_ref, v1_ref, dt_ref = out_refs
    xn = _rms(h, g_ref[...]).astype(BF16)
    lo = 0
    proj = {}
    for name, width in _PROJ_SPLITS:
        proj[name] = jnp.dot(xn, w_ref[:, lo:lo + width], preferred_element_type=F32)
        lo += width
    z_ref[0] = proj["z"].astype(BF16)
    xbc_ref[0] = proj["xbc"].astype(BF16)
    q_ref[0] = (proj["q"] * (LOG2E / math.sqrt(SB_HEAD_DIM))).astype(BF16)
    k_ref[0] = proj["k"].astype(BF16)
    v = proj["v"].astype(BF16)
    lane = lax.broadcasted_iota(jnp.int32, v.shape, 1)
    even_head = (lane & SB_HEAD_DIM) == 0
    zero = jnp.zeros_like(v)
    v0_ref[0] = jnp.where(even_head, v, zero)
    v1_ref[0] = jnp.where(even_head, zero, v)
    dt_ref[0] = jnp.dot(xn, wdt_ref[...], preferred_element_type=F32)


def _inproj_kernel(x_ref, g_ref, w_ref, wdt_ref, *out_refs):
    _inproj_rows(x_ref[...], g_ref, w_ref, wdt_ref, out_refs)


def _inproj_meta_kernel(hm_ref, g_ref, w_ref, wdt_ref, *refs):
    nout = len(_INPROJ_OUT_WIDTHS)
    _inproj_rows(hm_ref[...], g_ref, w_ref, wdt_ref, refs[nout:])


def _inproj(x, hm, g, w_main, w_dt):
    bsz, seq, _ = x.shape
    lp = seq + BLK
    tm = INPROJ_TM
    per_seq = seq // tm
    n_main = w_main.shape[1]
    out_shape = [jax.ShapeDtypeStruct((bsz, lp, width), dt)
                 for width, dt in zip(_INPROJ_OUT_WIDTHS, _INPROJ_OUT_DTYPES)]
    weights = [_const_spec((1, D_MODEL)), _const_spec((D_MODEL, n_main)), _const_spec((D_MODEL, DT_LANES))]
    params = pltpu.CompilerParams(dimension_semantics=("arbitrary",), vmem_limit_bytes=V7X_VMEM_LIMIT)

    tile = lambda t: (t // per_seq, t % per_seq, 0)
    outs = pl.pallas_call(
        _inproj_kernel,
        grid=(bsz * per_seq,),
        in_specs=[pl.BlockSpec((tm, D_MODEL), lambda t: (t, 0))] + weights,
        out_specs=[pl.BlockSpec((1, tm, width), tile) for width in _INPROJ_OUT_WIDTHS],
        out_shape=out_shape,
        compiler_params=params,
        name="inproj",
    )(x.reshape(bsz * seq, D_MODEL), g, w_main, w_dt)

    nout = len(outs)
    last_blk = lambda b: (b, lp // BLK - 1, 0)
    return pl.pallas_call(
        _inproj_meta_kernel,
        grid=(bsz,),
        in_specs=[_const_spec((BLK, D_MODEL))] + weights + [pl.BlockSpec(memory_space=pl.ANY)] * nout,
        out_specs=[pl.BlockSpec((1, BLK, width), last_blk) for width in _INPROJ_OUT_WIDTHS],
        out_shape=out_shape,
        input_output_aliases={4 + j: j for j in range(nout)},
        compiler_params=params,
        name="inproj_meta",
    )(hm, g, w_main, w_dt, *outs)


def _dot_f32_by_01(lhs, rhs, f32_side):
    x = lhs if f32_side == "lhs" else rhs
    p1 = x.astype(BF16)
    r1 = x - p1.astype(F32)
    p2 = r1.astype(BF16)
    p3 = (r1 - p2.astype(F32)).astype(BF16)
    out = None
    for piece in (p1, p2, p3):
        ops = (piece, rhs) if f32_side == "lhs" else (lhs, piece)
        term = jnp.dot(*ops, preferred_element_type=F32)
        out = term if out is None else out + term
    return out


def _ssd_kernel(xbc_ref, z_ref, dt_ref, cw_ref, cb_ref, dtb_ref, alog_ref, dexp_ref, g_ref, e_ref,
                out_ref, prev, state, *, nseq):
    c = pl.program_id(1)

    @pl.when(c == 0)
    def _():
        prev[...] = jnp.zeros_like(prev)
        state[...] = jnp.zeros_like(state)

    seqs = range(nseq)
    groups = range(SSD_GROUPS)
    row = lax.broadcasted_iota(jnp.int32, (BLK, 1), 0)
    real = jnp.logical_or(c > 0, row >= PAD)
    lane = lax.broadcasted_iota(jnp.int32, (BLK, DT_LANES), 1)
    dt_valid = jnp.logical_and(real, lane < SSD_HEADS)
    ri = lax.broadcasted_iota(jnp.int32, (BLK, BLK), 0)
    ci = lax.broadcasted_iota(jnp.int32, (BLK, BLK), 1)
    causal = ri >= ci
    tri = jnp.where(causal, 1.0, 0.0).astype(BF16)
    half = ci < SSD_HEAD_DIM
    cw = cw_ref[...]
    e = e_ref[...]
    a_neg = -jnp.exp(alog_ref[...])

    xc, dt, adt = [], [], []
    for s in seqs:
        x_raw = xbc_ref[s].astype(F32)
        prev[s, 8:8 + BLK, :] = x_raw
        acc = cb_ref[...] + cw[3:4, :] * x_raw
        for tap in range(SSD_CONV - 1):
            acc = acc + cw[tap:tap + 1, :] * prev[s, 5 + tap:5 + tap + BLK, :]
        prev[s, 0:8, :] = x_raw[BLK - 8:, :]
        xc.append(jnp.where(real, acc * jax.nn.sigmoid(acc), 0.0))
        dts = jax.nn.softplus(dt_ref[s] + dtb_ref[...])
        dt.append(jnp.where(dt_valid, dts, 0.0))
        adt.append(dt[s] * a_neg)
    xs = [xc[s][:, :SSD_INNER] for s in seqs]

    acs = [_dot_f32_by_01(tri, adt[s], f32_side="rhs") for s in seqs]
    acs_t = [acs[s].T for s in seqs]
    dt_exp = [_dot_f32_by_01(dt[s], e, f32_side="lhs") for s in seqs]
    acs_exp = [_dot_f32_by_01(acs[s], e, f32_side="lhs") for s in seqs]

    xdt_b, xdec, off_scale, chunk_decay, b_t, c_gb, b_gb = [], [], [], [], [], [], []
    for s in seqs:
        last = acs_exp[s][BLK - 1:BLK, :]
        xdt = xs[s] * dt_exp[s]
        xdec.append((xdt * jnp.exp(last - acs_exp[s])).astype(BF16))
        xdt_b.append(xdt.astype(BF16))
        off_scale.append(jnp.exp(acs_exp[s]))
        chunk_decay.append(jnp.exp(last))
        b_g = [xc[s][:, SSD_INNER + g * SSD_STATE:SSD_INNER + (g + 1) * SSD_STATE] for g in groups]
        c_g = [xc[s][:, SSD_INNER + (SSD_GROUPS + g) * SSD_STATE:
                     SSD_INNER + (SSD_GROUPS + g + 1) * SSD_STATE] for g in groups]
        b_gb.append([x.astype(BF16) for x in b_g])
        c_gb.append([x.astype(BF16) for x in c_g])
        b_t.append([x.T.astype(BF16) for x in b_g])

    gcols = [slice(g * GROUP_COLS, (g + 1) * GROUP_COLS) for g in groups]
    cb_mat = [[lax.dot_general(c_gb[s][g], b_gb[s][g], (((1,), (1,)), ((), ())),
                               preferred_element_type=F32) for g in groups] for s in seqs]
    s_prev = [[state[s, :, gcols[g]] for g in groups] for s in seqs]
    y_off = [[jnp.dot(c_gb[s][g], s_prev[s][g].astype(BF16), preferred_element_type=F32)
              for g in groups] for s in seqs]

    heads_per_group = SSD_HEADS // SSD_GROUPS
    decay_mats = [[None] * SSD_HEADS for _ in seqs]
    for s in seqs:
        for hh in range(SSD_HEADS):
            seg = acs[s][:, hh:hh + 1] - acs_t[s][hh:hh + 1, :]
            ldec = jnp.exp(jnp.where(causal, seg, -1e30))
            decay_mats[s][hh] = (cb_mat[s][hh // heads_per_group] * ldec).astype(BF16)

    y_diag = [[jnp.dot(decay_mats[s][hh], xdt_b[s][:, (hh // 2) * BLK:(hh // 2 + 1) * BLK],
                       preferred_element_type=F32) for hh in range(SSD_HEADS)] for s in seqs]
    new = [[jnp.dot(b_t[s][g], xdec[s][:, gcols[g]], preferred_element_type=F32)
            for g in groups] for s in seqs]

    for s in seqs:
        y_cols = []
        for g in groups:
            state[s, :, gcols[g]] = s_prev[s][g] * chunk_decay[s][:, gcols[g]] + new[s][g]
            yo = y_off[s][g] * off_scale[s][:, gcols[g]]
            for pair in range(GROUP_COLS // BLK):
                h0 = (g * GROUP_COLS + pair * BLK) // SSD_HEAD_DIM
                y_cols.append(jnp.where(half, y_diag[s][h0], y_diag[s][h0 + 1])
                              + yo[:, pair * BLK:(pair + 1) * BLK])
        y = jnp.concatenate(y_cols, axis=1) + xs[s] * dexp_ref[...]
        zf = z_ref[s].astype(F32)
        y = y * (zf * jax.nn.sigmoid(zf))
        out_ref[s] = _rms(y, g_ref[...]).astype(BF16)


def _ssd(xbc, z, dt_raw, conv_w, conv_b, dt_bias, a_log, d_exp, norm_g, expand):
    bsz, lp, _ = xbc.shape
    nblk = lp // BLK
    nseq = SSD_SEQS_PER_STEP
    assert bsz % nseq == 0
    blk = lambda b, c: (b, (c + nblk - 1) % nblk, 0)
    return pl.pallas_call(
        functools.partial(_ssd_kernel, nseq=nseq),
        grid=(bsz // nseq, nblk),
        in_specs=[
            pl.BlockSpec((nseq, BLK, XBC_WIDTH), blk),
            pl.BlockSpec((nseq, BLK, SSD_INNER), blk),
            pl.BlockSpec((nseq, BLK, DT_LANES), blk),
            _const_spec((SSD_CONV, XBC_WIDTH)),
            _const_spec((1, XBC_WIDTH)),
            _const_spec((1, DT_LANES)),
            _const_spec((1, DT_LANES)),
            _const_spec((1, SSD_INNER)),
            _const_spec((1, SSD_INNER)),
            _const_spec((DT_LANES, SSD_INNER)),
        ],
        out_specs=pl.BlockSpec((nseq, BLK, SSD_INNER), blk),
        out_shape=jax.ShapeDtypeStruct((bsz, lp, SSD_INNER), BF16),
        scratch_shapes=[
            pltpu.VMEM((nseq, BLK + 8, XBC_WIDTH), F32),
            pltpu.VMEM((nseq, SSD_STATE, SSD_INNER), F32),
        ],
        compiler_params=pltpu.CompilerParams(
            dimension_semantics=("arbitrary", "arbitrary"), vmem_limit_bytes=V7X_VMEM_LIMIT),
        name="ssd",
    )(xbc, z, dt_raw, conv_w, conv_b, dt_bias, a_log, d_exp, norm_g, expand)


SB_DEAD_LOG2 = -151.0
SB_LEAD = 3
SB_WAVES = 4


def _sb_kernel(q_ref, k_ref, v0_ref, v1_ref, tt_ref, o_ref, qm_scr, run_scr, acc_scr, worst_scr, *,
               nblk, npairs, qpb):
    lane = lax.broadcasted_iota(jnp.int32, (BLK, BLK), 1)
    sub = lax.broadcasted_iota(jnp.int32, (BLK, BLK), 0)
    first = lane < SB_HEAD_DIM
    strictly_older = lane < sub
    tt = tt_ref[...]

    def query_block(r, carry):
        qi = (pl.program_id(1) * qpb + r + 1) % nblk
        rows = pl.ds(pl.multiple_of(r * BLK, BLK), BLK)
        for p in range(npairs):
            qp = q_ref[0, rows, p * BLK:(p + 1) * BLK]
            zero = jnp.zeros_like(qp)
            qm_scr[p, :BLK, :] = jnp.where(first, qp, zero)
            qm_scr[p, BLK:, :] = jnp.where(first, zero, qp)

        def process(blocks, init, check):
            offs = [pl.multiple_of(((j + nblk - 1) % nblk) * BLK, BLK) for j in blocks]
            nb = range(len(blocks))
            worst = None
            per_wave = npairs // SB_WAVES
            for wave in range(SB_WAVES):
                w_max = process_pairs(range(wave * per_wave, (wave + 1) * per_wave), offs, nb, init, check)
                if check:
                    worst = w_max if worst is None else jnp.maximum(worst, w_max)
            return jnp.max(worst) if check else None

        def process_pairs(pairs, offs, nb, init, check):
            heads = [2 * p + hh for p in pairs for hh in range(2)]
            scores = {(b, p): lax.dot_general(
                qm_scr[p], k_ref[0, pl.ds(offs[b], BLK), p * BLK:(p + 1) * BLK],
                (((1,), (1,)), ((), ())), preferred_element_type=F32)
                for b in nb for p in pairs}
            log_betas, log_keeps, totals = {}, {}, {}
            for b in nb:
                for h in heads:
                    s = scores[b, h // 2][(h % 2) * BLK:(h % 2 + 1) * BLK, :]
                    neg_part = jnp.minimum(s, 0.0)
                    neg_pos = neg_part - s
                    sp = jnp.log(1.0 + jnp.exp2(neg_part + neg_pos)) * LOG2E
                    log_beta = neg_part - sp
                    log_keep = neg_pos - sp
                    if init and b == 0:
                        log_keep = jnp.where(strictly_older, log_keep, 0.0)
                    log_betas[b, h] = log_beta
                    log_keeps[b, h] = log_keep.astype(BF16)
                    totals[b, h] = jnp.sum(log_keep, axis=1, keepdims=True)
            sums = [jnp.dot(jnp.concatenate([log_keeps[b, h] for h in heads], axis=0), tt,
                            preferred_element_type=F32) for b in nb]
            weights = {}
            worst = None
            for i, h in enumerate(heads):
                run = None if init else run_scr[h]
                for b in nb:
                    after, total = sums[b][i * BLK:(i + 1) * BLK, :], totals[b, h]
                    if run is None:
                        w = jnp.where(strictly_older, jnp.exp2(log_betas[b, h] + after), 0.0)
                        run = jnp.broadcast_to(total, (BLK, BLK))
                    else:
                        w = jnp.exp2(log_betas[b, h] + (after + run))
                        run = run + total
                    weights[b, h] = w.astype(BF16)
                run_scr[h] = run
                if check:
                    worst = run if worst is None else jnp.maximum(worst, run)
            for p in pairs:
                pv = None if init else acc_scr[p]
                for b in nb:
                    v_heads = jnp.concatenate(
                        [v0_ref[0, pl.ds(offs[b], BLK), p * BLK:(p + 1) * BLK],
                         v1_ref[0, pl.ds(offs[b], BLK), p * BLK:(p + 1) * BLK]], axis=0)
                    w_heads = jnp.concatenate([weights[b, 2 * p], weights[b, 2 * p + 1]], axis=1)
                    contrib = jnp.dot(w_heads, v_heads, preferred_element_type=F32)
                    pv = contrib if pv is None else pv + contrib
                acc_scr[p] = pv
            return worst

        for lead in range(1, SB_LEAD + 1):
            last = lead == SB_LEAD

            @pl.when(qi >= lead - 1 if last else qi == lead - 1)
            def _(lead=lead, last=last):
                worst = process([qi - i for i in range(lead)], True, last)
                worst_scr[0] = worst if last else jnp.float32(0.0)

        def cond(st):
            return jnp.logical_and(st[0] >= 0, st[1] > SB_DEAD_LOG2)

        def body(st):
            return st[0] - 1, process([st[0]], False, True)

        lax.while_loop(cond, body, (qi - SB_LEAD, worst_scr[0]))
        for p in range(npairs):
            o_ref[0, rows, p * BLK:(p + 1) * BLK] = acc_scr[p].astype(BF16)
        return carry

    lax.fori_loop(0, qpb, query_block, 0)


def _sb_attention(q, k, v0, v1, tt):
    bsz, lp, width = q.shape
    nblk = lp // BLK
    npairs = width // BLK
    qpb = SB_Q_BLOCKS_PER_STEP
    assert nblk % qpb == 0
    qblk = lambda b, g: (b, g, 0)
    kv_spec = pl.BlockSpec((1, lp, width), lambda b, g: (b, 0, 0), pipeline_mode=pl.Buffered(1))
    return pl.pallas_call(
        functools.partial(_sb_kernel, nblk=nblk, npairs=npairs, qpb=qpb),
        grid=(bsz, nblk // qpb),
        in_specs=[
            pl.BlockSpec((1, qpb * BLK, width), qblk),
            kv_spec,
            kv_spec,
            kv_spec,
            _const_spec((BLK, BLK)),
        ],
        out_specs=pl.BlockSpec((1, qpb * BLK, width), qblk),
        out_shape=jax.ShapeDtypeStruct((bsz, lp, width), BF16),
        scratch_shapes=[
            pltpu.VMEM((npairs, 2 * BLK, BLK), BF16),
            pltpu.VMEM((2 * npairs, BLK, BLK), F32),
            pltpu.VMEM((npairs, BLK, BLK), F32),
            pltpu.SMEM((1,), F32),
        ],
        compiler_params=pltpu.CompilerParams(
            dimension_semantics=("arbitrary", "arbitrary"), vmem_limit_bytes=V7X_VMEM_LIMIT),
        name="sb_attention",
    )(q, k, v0, v1, tt)


def _outproj_rows(ys, o, h, wa_ref, wb_ref, gsb_ref, gpost_ref, gffn_ref):
    osb = _rms(o.astype(F32), gsb_ref[...]).astype(BF16)
    mix = jnp.dot(ys, wa_ref[...], preferred_element_type=F32)
    mix = mix + jnp.dot(osb, wb_ref[...], preferred_element_type=F32)
    h1 = h + _rms(mix, gpost_ref[...])
    return h1, _rms(h1, gffn_ref[...]).astype(BF16)


def _outproj_kernel(ys_ref, o_ref, h_ref, wa_ref, wb_ref, gsb_ref, gpost_ref, gffn_ref,
                    h1_ref, xn2_ref, *, nsub):
    sub = h_ref.shape[1] // nsub
    rows = [pl.ds(k * sub, sub) for k in range(nsub)]
    osb = [_rms(o_ref[0, r, :].astype(F32), gsb_ref[...]).astype(BF16) for r in rows]
    mix = [jnp.dot(ys_ref[0, r, :], wa_ref[...], preferred_element_type=F32)
           + jnp.dot(osb[k], wb_ref[...], preferred_element_type=F32) for k, r in enumerate(rows)]
    for k, r in enumerate(rows):
        h1 = h_ref[0, r, :] + _rms(mix[k], gpost_ref[...])
        h1_ref[0, r, :] = h1
        xn2_ref[0, r, :] = _rms(h1, gffn_ref[...]).astype(BF16)


def _outproj_meta_kernel(ys_ref, o_ref, hm_ref, wa_ref, wb_ref, gsb_ref, gpost_ref, gffn_ref, xn2_ref):
    _, xn2 = _outproj_rows(ys_ref[0], o_ref[0], hm_ref[...], wa_ref, wb_ref, gsb_ref, gpost_ref, gffn_ref)
    xn2_ref[0] = xn2


def _outproj(y_ssd, o_sb, x, hm, w_a, w_b, g_sb, g_post, g_ffn):
    bsz, seq, _ = x.shape
    lp = y_ssd.shape[1]
    tm = OUTPROJ_TM
    per_seq = seq // tm
    weights = [
        _const_spec((SSD_INNER, D_MODEL)),
        _const_spec((SB_WIDTH, D_MODEL)),
        _const_spec((1, SB_WIDTH)),
        _const_spec((1, D_MODEL)),
        _const_spec((1, D_MODEL)),
    ]
    params = pltpu.CompilerParams(dimension_semantics=("arbitrary",), vmem_limit_bytes=V7X_VMEM_LIMIT)
    tile = lambda t: (t // per_seq, t % per_seq, 0)
    h1, xn2 = pl.pallas_call(
        functools.partial(_outproj_kernel, nsub=OUTPROJ_SUBTILES),
        grid=(bsz * per_seq,),
        in_specs=[
            pl.BlockSpec((1, tm, SSD_INNER), tile),
            pl.BlockSpec((1, tm, SB_WIDTH), tile),
            pl.BlockSpec((1, tm, D_MODEL), tile),
        ] + weights,
        out_specs=[pl.BlockSpec((1, tm, D_MODEL), tile), pl.BlockSpec((1, tm, D_MODEL), tile)],
        out_shape=[jax.ShapeDtypeStruct((bsz, seq, D_MODEL), F32),
                   jax.ShapeDtypeStruct((bsz, seq, D_MODEL), BF16)],
        compiler_params=params,
        name="outproj",
    )(y_ssd, o_sb, x, w_a, w_b, g_sb, g_post, g_ffn)

    last_blk = lambda b: (b, lp // BLK - 1, 0)
    xn2_meta = pl.pallas_call(
        _outproj_meta_kernel,
        grid=(bsz,),
        in_specs=[
            pl.BlockSpec((1, BLK, SSD_INNER), last_blk),
            pl.BlockSpec((1, BLK, SB_WIDTH), last_blk),
            _const_spec((BLK, D_MODEL)),
        ] + weights,
        out_specs=pl.BlockSpec((1, BLK, D_MODEL), lambda b: (b, 0, 0)),
        out_shape=jax.ShapeDtypeStruct((bsz, BLK, D_MODEL), BF16),
        compiler_params=params,
        name="outproj_meta",
    )(y_ssd, o_sb, hm, w_a, w_b, g_sb, g_post, g_ffn)
    return h1, xn2, xn2_meta


def _gelu_tanh(x):
    return 0.5 * x * (1.0 + jnp.tanh(math.sqrt(2.0 / math.pi) * (x + 0.044715 * (x * x * x))))


def _ffn_kernel(xn_ref, xm_ref, h1_ref, wup_ref, cw_ref, cb_ref, wd_ref, g_ref, out_ref,
                gbuf0, gbuf1, ubuf0, ubuf1, halo, act_scr, *, tm, nch):
    i = pl.program_id(1)
    gbufs, ubufs = (gbuf0, gbuf1), (ubuf0, ubuf1)

    @pl.when(i == 0)
    def _():
        for c in range(nch):
            gm = jnp.dot(xm_ref[0], wup_ref[c], preferred_element_type=F32)
            halo[c] = gm[N_META - 8:, :]

    def project(c, slot):
        gbuf, ubuf = gbufs[slot], ubufs[slot]
        xn = xn_ref[0]
        gate = jnp.dot(xn, wup_ref[c], preferred_element_type=F32)
        gbuf[0:8, :] = halo[c]
        gbuf[8:8 + tm, :] = gate
        halo[c] = gate[tm - 8:, :]
        ubuf[...] = jnp.dot(xn, wup_ref[nch + c], preferred_element_type=F32)

    def activate(c, slot):
        gbuf, ubuf = gbufs[slot], ubufs[slot]
        cw = cw_ref[c]
        conv = cb_ref[c] + cw[2:3, :] * gbuf[8:8 + tm, :]
        conv = conv + cw[1:2, :] * gbuf[7:7 + tm, :] + cw[0:1, :] * gbuf[6:6 + tm, :]
        act_scr[c] = (_gelu_tanh(conv) * ubuf[...]).astype(BF16)

    project(0, 0)
    for c in range(nch - 1):
        project(c + 1, (c + 1) % 2)
        activate(c, c % 2)
    activate(nch - 1, (nch - 1) % 2)
    act = jnp.concatenate([act_scr[c] for c in range(nch)], axis=1)
    down = jnp.dot(act, wd_ref[...], preferred_element_type=F32)
    out_ref[0] = h1_ref[0] + _rms(down, g_ref[...])


def _ffn(xn2, xn2_meta, h1, w_up, conv_w, conv_b, w_down, g_post):
    bsz, seq, _ = xn2.shape
    tm, tf = FFN_TM, FFN_TF
    nch = D_FF // tf
    w_up_c = w_up.reshape(D_MODEL, 2 * nch, tf).transpose(1, 0, 2)
    conv_w_c = conv_w.reshape(FFN_CONV, nch, tf).transpose(1, 0, 2)
    conv_b_c = conv_b.reshape(nch, 1, tf)
    xblk = lambda b, i: (b, i, 0)
    meta = lambda b, i: (b, BLK // N_META - 1, 0)
    return pl.pallas_call(
        functools.partial(_ffn_kernel, tm=tm, nch=nch),
        grid=(bsz, seq // tm),
        in_specs=[
            pl.BlockSpec((1, tm, D_MODEL), xblk),
            pl.BlockSpec((1, N_META, D_MODEL), meta),
            pl.BlockSpec((1, tm, D_MODEL), xblk),
            _const_spec((2 * nch, D_MODEL, tf)),
            _const_spec((nch, FFN_CONV, tf)),
            _const_spec((nch, 1, tf)),
            _const_spec((D_FF, D_MODEL)),
            _const_spec((1, D_MODEL)),
        ],
        out_specs=pl.BlockSpec((1, tm, D_MODEL), xblk),
        out_shape=jax.ShapeDtypeStruct((bsz, seq, D_MODEL), F32),
        scratch_shapes=[
            pltpu.VMEM((tm + 8, tf), F32),
            pltpu.VMEM((tm + 8, tf), F32),
            pltpu.VMEM((tm, tf), F32),
            pltpu.VMEM((tm, tf), F32),
            pltpu.VMEM((nch, 8, tf), F32),
            pltpu.VMEM((nch, tm, tf), BF16),
        ],
        compiler_params=pltpu.CompilerParams(
            dimension_semantics=("arbitrary", "arbitrary"), vmem_limit_bytes=V7X_VMEM_LIMIT),
        name="ffn",
    )(xn2, xn2_meta, h1, w_up_c, conv_w_c, conv_b_c, w_down, g_post)


def _pad_lanes(v, n):
    return jnp.pad(v, ((0, 0), (0, n - v.shape[-1])))


def kernel(x, meta_tokens, mix_pre_g, w_in, ssd_conv_w, ssd_conv_b, ssd_dt_bias, ssd_a_log, ssd_d,
           ssd_norm_g, sb_norm_g, w_out, mix_post_g, ffn_pre_g, w_up, ffn_conv_w, ffn_conv_b,
           w_down, ffn_post_g):
    bsz, seq, d = x.shape
    depth = w_in.shape[0]
    assert depth == 1 and d == D_MODEL and seq % OUTPROJ_TM == 0

    hm = jnp.concatenate([jnp.zeros((PAD, d), x.dtype), meta_tokens.astype(x.dtype)], axis=0)

    l = 0
    off_xbc = SSD_INNER
    off_dt = off_xbc + XBC_WIDTH
    off_q = off_dt + SSD_HEADS
    w = w_in[l]
    w_main = jnp.concatenate([w[:, :off_dt], w[:, off_q:]], axis=1).astype(BF16)
    w_dt = _pad_lanes(w[:, off_dt:off_q], DT_LANES).astype(BF16)
    z, xbc, q, k, v0, v1, dt_raw = _inproj(x, hm, mix_pre_g[l][None], w_main, w_dt)

    head_of_col = jnp.arange(SSD_INNER) // SSD_HEAD_DIM
    expand = (jnp.arange(DT_LANES)[:, None] == head_of_col[None, :]).astype(BF16)
    d_exp = jnp.repeat(ssd_d[l].astype(F32), SSD_HEAD_DIM)[None]
    y_ssd = _ssd(xbc, z, dt_raw, ssd_conv_w[l], ssd_conv_b[l][None],
                 _pad_lanes(ssd_dt_bias[l][None], DT_LANES), _pad_lanes(ssd_a_log[l][None], DT_LANES),
                 d_exp, ssd_norm_g[l][None], expand)

    kk = jnp.arange(BLK)
    tt = (kk[:, None] > kk[None, :]).astype(BF16)
    o_sb = _sb_attention(q, k, v0, v1, tt)

    wo = w_out[l].astype(BF16)
    h1, xn2, xn2_meta = _outproj(y_ssd, o_sb, x, hm, wo[:SSD_INNER], wo[SSD_INNER:], sb_norm_g[l][None],
                                 mix_post_g[l][None], ffn_pre_g[l][None])

    return _ffn(xn2, xn2_meta, h1, w_up[l].astype(BF16), ffn_conv_w[l], ffn_conv_b[l][None],
                w_down[l].astype(BF16), ffn_post_g[l][None])
```

```python
import functools
import math

import jax
import jax.numpy as jnp
from jax import lax
from jax.experimental import pallas as pl
from jax.experimental.pallas import tpu as pltpu

F32 = jnp.float32
BF16 = jnp.bfloat16

D_MODEL = 1024
N_META = 16
BLK = 128
PAD = BLK - N_META
SSD_HEADS = 16
SSD_HEAD_DIM = 64
SSD_GROUPS = 2
SSD_STATE = 128
SSD_INNER = 1024
SSD_CONV = 4
XBC_WIDTH = SSD_INNER + 2 * SSD_GROUPS * SSD_STATE
SB_WIDTH = 1024
SB_HEAD_DIM = 64
D_FF = 2816
FFN_CONV = 3
EPS = 1e-6
DT_LANES = 128
GROUP_COLS = SSD_INNER // SSD_GROUPS
LOG2E = 1.4426950408889634

V7X_VMEM_LIMIT = 56 * 1024 * 1024

INPROJ_TM = 512
OUTPROJ_TM = 1024
OUTPROJ_SUBTILES = 4
FFN_TM = 512
FFN_TF = 256
SSD_SEQS_PER_STEP = 2
SB_Q_BLOCKS_PER_STEP = 3


def _rms(x, g):
    return x * lax.rsqrt(jnp.mean(x * x, axis=-1, keepdims=True) + EPS) * g


def _const_spec(shape):
    nd = len(shape)
    return pl.BlockSpec(shape, lambda *_: (0,) * nd, pipeline_mode=pl.Buffered(1))


_PROJ_SPLITS = (("z", SSD_INNER), ("xbc", XBC_WIDTH), ("q", SB_WIDTH), ("k", SB_WIDTH), ("v", SB_WIDTH))
_INPROJ_OUT_WIDTHS = (SSD_INNER, XBC_WIDTH, SB_WIDTH, SB_WIDTH, SB_WIDTH, SB_WIDTH, DT_LANES)
_INPROJ_OUT_DTYPES = (BF16, BF16, BF16, BF16, BF16, BF16, F32)


def _inproj_rows(h, g_ref, w_ref, wdt_ref, out_refs):
    z_ref, xbc_ref, q_ref, k_ref, v0_ref, v1_ref, dt_ref = out_refs
    xn = _rms(h, g_ref[...]).astype(BF16)
    lo = 0
    proj = {}
    for name, width in _PROJ_SPLITS:
        proj[name] = jnp.dot(xn, w_ref[:, lo:lo + width], preferred_element_type=F32)
        lo += width
    z_ref[0] = proj["z"].astype(BF16)
    xbc_ref[0] = proj["xbc"].astype(BF16)
    q_ref[0] = (proj["q"] * (LOG2E / math.sqrt(SB_HEAD_DIM))).astype(BF16)
    k_ref[0] = proj["k"].astype(BF16)
    v = proj["v"].astype(BF16)
    lane = lax.broadcasted_iota(jnp.int32, v.shape, 1)
    even_head = (lane & SB_HEAD_DIM) == 0
    zero = jnp.zeros_like(v)
    v0_ref[0] = jnp.where(even_head, v, zero)
    v1_ref[0] = jnp.where(even_head, zero, v)
    dt_ref[0] = jnp.dot(xn, wdt_ref[...], preferred_element_type=F32)


def _inproj_kernel(x_ref, g_ref, w_ref, wdt_ref, *out_refs):
    _inproj_rows(x_ref[...], g_ref, w_ref, wdt_ref, out_refs)


def _inproj_meta_kernel(hm_ref, g_ref, w_ref, wdt_ref, *refs):
    nout = len(_INPROJ_OUT_WIDTHS)
    out_refs, scratch = refs[nout:2 * nout], refs[2 * nout:]

    @pl.when(pl.program_id(0) == 0)
    def _():
        _inproj_rows(hm_ref[...], g_ref, w_ref, wdt_ref, scratch)

    for out_ref, scr in zip(out_refs, scratch):
        out_ref[...] = scr[...]


def _inproj(x, hm, g, w_main, w_dt):
    bsz, seq, _ = x.shape
    lp = seq + BLK
    tm = INPROJ_TM
    per_seq = seq // tm
    n_main = w_main.shape[1]
    out_shape = [jax.ShapeDtypeStruct((bsz, lp, width), dt)
                 for width, dt in zip(_INPROJ_OUT_WIDTHS, _INPROJ_OUT_DTYPES)]
    weights = [_const_spec((1, D_MODEL)), _const_spec((D_MODEL, n_main)), _const_spec((D_MODEL, DT_LANES))]
    params = pltpu.CompilerParams(dimension_semantics=("arbitrary",), vmem_limit_bytes=V7X_VMEM_LIMIT)

    tile = lambda t: (t // per_seq, t % per_seq, 0)
    outs = pl.pallas_call(
        _inproj_kernel,
        grid=(bsz * per_seq,),
        in_specs=[pl.BlockSpec((tm, D_MODEL), lambda t: (t, 0))] + weights,
        out_specs=[pl.BlockSpec((1, tm, width), tile) for width in _INPROJ_OUT_WIDTHS],
        out_shape=out_shape,
        compiler_params=params,
        name="inproj",
    )(x.reshape(bsz * seq, D_MODEL), g, w_main, w_dt)

    nout = len(outs)
    last_blk = lambda b: (b, lp // BLK - 1, 0)
    return pl.pallas_call(
        _inproj_meta_kernel,
        grid=(bsz,),
        in_specs=[_const_spec((BLK, D_MODEL))] + weights + [pl.BlockSpec(memory_space=pl.ANY)] * nout,
        out_specs=[pl.BlockSpec((1, BLK, width), last_blk) for width in _INPROJ_OUT_WIDTHS],
        out_shape=out_shape,
        input_output_aliases={4 + j: j for j in range(nout)},
        scratch_shapes=[pltpu.VMEM((1, BLK, width), dt)
                        for width, dt in zip(_INPROJ_OUT_WIDTHS, _INPROJ_OUT_DTYPES)],
        compiler_params=params,
        name="inproj_meta",
    )(hm, g, w_main, w_dt, *outs)


def _dot_f32_by_01(lhs, rhs, f32_side):
    x = lhs if f32_side == "lhs" else rhs
    p1 = x.astype(BF16)
    r1 = x - p1.astype(F32)
    p2 = r1.astype(BF16)
    p3 = (r1 - p2.astype(F32)).astype(BF16)
    out = None
    for piece in (p1, p2, p3):
        ops = (piece, rhs) if f32_side == "lhs" else (lhs, piece)
        term = jnp.dot(*ops, preferred_element_type=F32)
        out = term if out is None else out + term
    return out


def _ssd_kernel(xbc_ref, z_ref, dt_ref, cw_ref, cb_ref, dtb_ref, alog_ref, dexp_ref, g_ref, e_ref,
                out_ref, prev, state, *, nseq):
    c = pl.program_id(1)

    @pl.when(c == 0)
    def _():
        prev[...] = jnp.zeros_like(prev)
        state[...] = jnp.zeros_like(state)

    seqs = range(nseq)
    groups = range(SSD_GROUPS)
    row = lax.broadcasted_iota(jnp.int32, (BLK, 1), 0)
    real = jnp.logical_or(c > 0, row >= PAD)
    lane = lax.broadcasted_iota(jnp.int32, (BLK, DT_LANES), 1)
    dt_valid = jnp.logical_and(real, lane < SSD_HEADS)
    ri = lax.broadcasted_iota(jnp.int32, (BLK, BLK), 0)
    ci = lax.broadcasted_iota(jnp.int32, (BLK, BLK), 1)
    causal = ri >= ci
    tri = jnp.where(causal, 1.0, 0.0).astype(BF16)
    half = ci < SSD_HEAD_DIM
    cw = cw_ref[...]
    e = e_ref[...]
    a_neg = -jnp.exp(alog_ref[...])

    xc, dt, adt = [], [], []
    for s in seqs:
        x_raw = xbc_ref[s].astype(F32)
        prev[s, 8:8 + BLK, :] = x_raw
        acc = cb_ref[...] + cw[3:4, :] * x_raw
        for tap in range(SSD_CONV - 1):
            acc = acc + cw[tap:tap + 1, :] * prev[s, 5 + tap:5 + tap + BLK, :]
        prev[s, 0:8, :] = x_raw[BLK - 8:, :]
        xc.append(jnp.where(real, acc * jax.nn.sigmoid(acc), 0.0))
        dts = jax.nn.softplus(dt_ref[s] + dtb_ref[...])
        dt.append(jnp.where(dt_valid, dts, 0.0))
        adt.append(dt[s] * a_neg)
    xs = [xc[s][:, :SSD_INNER] for s in seqs]

    acs = [_dot_f32_by_01(tri, adt[s], f32_side="rhs") for s in seqs]
    acs_t = [acs[s].T for s in seqs]
    dt_exp = [_dot_f32_by_01(dt[s], e, f32_side="lhs") for s in seqs]
    acs_exp = [_dot_f32_by_01(acs[s], e, f32_side="lhs") for s in seqs]

    xdt_b, xdec, off_scale, chunk_decay, b_t, c_gb, b_gb = [], [], [], [], [], [], []
    for s in seqs:
        last = acs_exp[s][BLK - 1:BLK, :]
        xdt = xs[s] * dt_exp[s]
        xdec.append((xdt * jnp.exp(last - acs_exp[s])).astype(BF16))
        xdt_b.append(xdt.astype(BF16))
        off_scale.append(jnp.exp(acs_exp[s]))
        chunk_decay.append(jnp.exp(last))
        b_g = [xc[s][:, SSD_INNER + g * SSD_STATE:SSD_INNER + (g + 1) * SSD_STATE] for g in groups]
        c_g = [xc[s][:, SSD_INNER + (SSD_GROUPS + g) * SSD_STATE:
                     SSD_INNER + (SSD_GROUPS + g + 1) * SSD_STATE] for g in groups]
        b_gb.append([x.astype(BF16) for x in b_g])
        c_gb.append([x.astype(BF16) for x in c_g])
        b_t.append([x.T.astype(BF16) for x in b_g])

    gcols = [slice(g * GROUP_COLS, (g + 1) * GROUP_COLS) for g in groups]
    cb_mat = [[lax.dot_general(c_gb[s][g], b_gb[s][g], (((1,), (1,)), ((), ())),
                               preferred_element_type=F32) for g in groups] for s in seqs]
    s_prev = [[state[s, :, gcols[g]] for g in groups] for s in seqs]
    y_off = [[jnp.dot(c_gb[s][g], s_prev[s][g].astype(BF16), preferred_element_type=F32)
              for g in groups] for s in seqs]

    heads_per_group = SSD_HEADS // SSD_GROUPS
    decay_mats = [[None] * SSD_HEADS for _ in seqs]
    for s in seqs:
        for hh in range(SSD_HEADS):
            seg = acs[s][:, hh:hh + 1] - acs_t[s][hh:hh + 1, :]
            ldec = jnp.exp(jnp.where(causal, seg, -1e30))
            decay_mats[s][hh] = (cb_mat[s][hh // heads_per_group] * ldec).astype(BF16)

    y_diag = [[jnp.dot(decay_mats[s][hh], xdt_b[s][:, (hh // 2) * BLK:(hh // 2 + 1) * BLK],
                       preferred_element_type=F32) for hh in range(SSD_HEADS)] for s in seqs]
    new = [[jnp.dot(b_t[s][g], xdec[s][:, gcols[g]], preferred_element_type=F32)
            for g in groups] for s in seqs]

    for s in seqs:
        y_cols = []
        for g in groups:
            state[s, :, gcols[g]] = s_prev[s][g] * chunk_decay[s][:, gcols[g]] + new[s][g]
            yo = y_off[s][g] * off_scale[s][:, gcols[g]]
            for pair in range(GROUP_COLS // BLK):
                h0 = (g * GROUP_COLS + pair * BLK) // SSD_HEAD_DIM
                y_cols.append(jnp.where(half, y_diag[s][h0], y_diag[s][h0 + 1])
                              + yo[:, pair * BLK:(pair + 1) * BLK])
        y = jnp.concatenate(y_cols, axis=1) + xs[s] * dexp_ref[...]
        zf = z_ref[s].astype(F32)
        y = y * (zf * jax.nn.sigmoid(zf))
        out_ref[s] = _rms(y, g_ref[...]).astype(BF16)


def _ssd(xbc, z, dt_raw, conv_w, conv_b, dt_bias, a_log, d_exp, norm_g, expand):
    bsz, lp, _ = xbc.shape
    nblk = lp // BLK
    nseq = SSD_SEQS_PER_STEP
    assert bsz % nseq == 0
    blk = lambda b, c: (b, (c + nblk - 1) % nblk, 0)
    return pl.pallas_call(
        functools.partial(_ssd_kernel, nseq=nseq),
        grid=(bsz // nseq, nblk),
        in_specs=[
            pl.BlockSpec((nseq, BLK, XBC_WIDTH), blk),
            pl.BlockSpec((nseq, BLK, SSD_INNER), blk),
            pl.BlockSpec((nseq, BLK, DT_LANES), blk),
            _const_spec((SSD_CONV, XBC_WIDTH)),
            _const_spec((1, XBC_WIDTH)),
            _const_spec((1, DT_LANES)),
            _const_spec((1, DT_LANES)),
            _const_spec((1, SSD_INNER)),
            _const_spec((1, SSD_INNER)),
            _const_spec((DT_LANES, SSD_INNER)),
        ],
        out_specs=pl.BlockSpec((nseq, BLK, SSD_INNER), blk),
        out_shape=jax.ShapeDtypeStruct((bsz, lp, SSD_INNER), BF16),
        scratch_shapes=[
            pltpu.VMEM((nseq, BLK + 8, XBC_WIDTH), F32),
            pltpu.VMEM((nseq, SSD_STATE, SSD_INNER), F32),
        ],
        compiler_params=pltpu.CompilerParams(
            dimension_semantics=("arbitrary", "arbitrary"), vmem_limit_bytes=V7X_VMEM_LIMIT),
        name="ssd",
    )(xbc, z, dt_raw, conv_w, conv_b, dt_bias, a_log, d_exp, norm_g, expand)


SB_DEAD_LOG2 = -151.0
SB_LEAD = 3
SB_WAVES = 1


def _sb_kv_copies(hbm_refs, bufs, sems, batch, part, nblk, qpb):
    nparts = nblk // qpb
    seq_rows = (nblk - 1) * BLK
    if part == nparts:
        rows = pl.ds(seq_rows, BLK)
    else:
        start = part * qpb * BLK
        rows = pl.ds(start, min(qpb * BLK, seq_rows - start))
    return [pltpu.make_async_copy(src.at[batch, rows, :], dst.at[rows, :], sems.at[a, part])
            for a, (src, dst) in enumerate(zip(hbm_refs, bufs))]


def _sb_kernel(q_ref, k_hbm, v0_hbm, v1_hbm, tt_ref, o_ref, k_ref, v0_ref, v1_ref, kv_sems,
               qm_scr, run_scr, acc_scr, worst_scr, *, nblk, npairs, qpb):
    batch, step = pl.program_id(0), pl.program_id(1)
    nparts = nblk // qpb
    hbm_refs, bufs = (k_hbm, v0_hbm, v1_hbm), (k_ref, v0_ref, v1_ref)

    @pl.when(step == 0)
    def _():
        for part in [nparts] + list(range(nparts)):
            for cp in _sb_kv_copies(hbm_refs, bufs, kv_sems, batch, part, nblk, qpb):
                cp.start()
        for cp in _sb_kv_copies(hbm_refs, bufs, kv_sems, batch, nparts, nblk, qpb):
            cp.wait()

    for part in range(nparts):
        @pl.when(step == part)
        def _(part=part):
            for cp in _sb_kv_copies(hbm_refs, bufs, kv_sems, batch, part, nblk, qpb):
                cp.wait()

    lane = lax.broadcasted_iota(jnp.int32, (BLK, BLK), 1)
    sub = lax.broadcasted_iota(jnp.int32, (BLK, BLK), 0)
    first = lane < SB_HEAD_DIM
    strictly_older = lane < sub
    tt = tt_ref[...]

    def query_block(r, carry):
        qi = (step * qpb + r + 1) % nblk
        rows = pl.ds(pl.multiple_of(r * BLK, BLK), BLK)
        for p in range(npairs):
            qp = q_ref[0, rows, p * BLK:(p + 1) * BLK]
            zero = jnp.zeros_like(qp)
            qm_scr[p, :BLK, :] = jnp.where(first, qp, zero)
            qm_scr[p, BLK:, :] = jnp.where(first, zero, qp)

        def process(blocks, init, check):
            offs = [pl.multiple_of(((j + nblk - 1) % nblk) * BLK, BLK) for j in blocks]
            nb = range(len(blocks))
            worst = None
            per_wave = npairs // SB_WAVES
            for wave in range(SB_WAVES):
                w_max = process_pairs(range(wave * per_wave, (wave + 1) * per_wave), offs, nb, init, check)
                if check:
                    worst = w_max if worst is None else jnp.maximum(worst, w_max)
            return jnp.max(worst) if check else None

        def process_pairs(pairs, offs, nb, init, check):
            heads = [2 * p + hh for p in pairs for hh in range(2)]
            scores = {(b, p): lax.dot_general(
                qm_scr[p], k_ref[pl.ds(offs[b], BLK), p * BLK:(p + 1) * BLK],
                (((1,), (1,)), ((), ())), preferred_element_type=F32)
                for b in nb for p in pairs}
            log_betas, log_keeps, totals = {}, {}, {}
            for b in nb:
                for h in heads:
                    s = scores[b, h // 2][(h % 2) * BLK:(h % 2 + 1) * BLK, :]
                    neg_part = jnp.minimum(s, 0.0)
                    neg_pos = neg_part - s
                    sp = jnp.log(1.0 + jnp.exp2(neg_part + neg_pos)) * LOG2E
                    log_beta = neg_part - sp
                    log_keep = neg_pos - sp
                    if init and b == 0:
                        log_keep = jnp.where(strictly_older, log_keep, 0.0)
                    log_betas[b, h] = log_beta
                    log_keeps[b, h] = log_keep.astype(BF16)
                    totals[b, h] = jnp.sum(log_keep, axis=1, keepdims=True)
            sums = [jnp.dot(jnp.concatenate([log_keeps[b, h] for h in heads], axis=0), tt,
                            preferred_element_type=F32) for b in nb]
            weights = {}
            worst = None
            for i, h in enumerate(heads):
                run = None if init else run_scr[h]
                for b in nb:
                    after, total = sums[b][i * BLK:(i + 1) * BLK, :], totals[b, h]
                    if run is None:
                        w = jnp.where(strictly_older, jnp.exp2(log_betas[b, h] + after), 0.0)
                        run = jnp.broadcast_to(total, (BLK, BLK))
                    else:
                        w = jnp.exp2(log_betas[b, h] + (after + run))
                        run = run + total
                    weights[b, h] = w.astype(BF16)
                run_scr[h] = run
                if check:
                    worst = run if worst is None else jnp.maximum(worst, run)
            for p in pairs:
                pv = None if init else acc_scr[p]
                for b in nb:
                    v_heads = jnp.concatenate(
                        [v0_ref[pl.ds(offs[b], BLK), p * BLK:(p + 1) * BLK],
                         v1_ref[pl.ds(offs[b], BLK), p * BLK:(p + 1) * BLK]], axis=0)
                    w_heads = jnp.concatenate([weights[b, 2 * p], weights[b, 2 * p + 1]], axis=1)
                    contrib = jnp.dot(w_heads, v_heads, preferred_element_type=F32)
                    pv = contrib if pv is None else pv + contrib
                acc_scr[p] = pv
            return worst

        for lead in range(1, SB_LEAD + 1):
            last = lead == SB_LEAD

            @pl.when(qi >= lead - 1 if last else qi == lead - 1)
            def _(lead=lead, last=last):
                worst = process([qi - i for i in range(lead)], True, last)
                worst_scr[0] = worst if last else jnp.float32(0.0)

        def cond(st):
            return jnp.logical_and(st[0] >= 0, st[1] > SB_DEAD_LOG2)

        def body(st):
            return st[0] - 1, process([st[0]], False, True)

        lax.while_loop(cond, body, (qi - SB_LEAD, worst_scr[0]))
        for p in range(npairs):
            o_ref[0, rows, p * BLK:(p + 1) * BLK] = acc_scr[p].astype(BF16)
        return carry

    lax.fori_loop(0, qpb, query_block, 0)


def _sb_attention(q, k, v0, v1, tt):
    bsz, lp, width = q.shape
    nblk = lp // BLK
    qpb = SB_Q_BLOCKS_PER_STEP
    assert nblk % qpb == 0
    npairs = width // BLK
    nparts = nblk // qpb
    qblk = lambda b, g: (b, g, 0)
    kv_hbm = pl.BlockSpec(memory_space=pl.ANY)
    return pl.pallas_call(
        functools.partial(_sb_kernel, nblk=nblk, npairs=npairs, qpb=qpb),
        grid=(bsz, nparts),
        in_specs=[
            pl.BlockSpec((1, qpb * BLK, width), qblk),
            kv_hbm,
            kv_hbm,
            kv_hbm,
            _const_spec((BLK, BLK)),
        ],
        out_specs=pl.BlockSpec((1, qpb * BLK, width), qblk),
        out_shape=jax.ShapeDtypeStruct((bsz, lp, width), BF16),
        scratch_shapes=[
            pltpu.VMEM((lp, width), BF16),
            pltpu.VMEM((lp, width), BF16),
            pltpu.VMEM((lp, width), BF16),
            pltpu.SemaphoreType.DMA((3, nparts + 1)),
            pltpu.VMEM((npairs, 2 * BLK, BLK), BF16),
            pltpu.VMEM((2 * npairs, BLK, BLK), F32),
            pltpu.VMEM((npairs, BLK, BLK), F32),
            pltpu.SMEM((1,), F32),
        ],
        compiler_params=pltpu.CompilerParams(
            dimension_semantics=("arbitrary", "arbitrary"), vmem_limit_bytes=V7X_VMEM_LIMIT),
        name="sb_attention",
    )(q, k, v0, v1, tt)


def _outproj_rows(ys, o, h, wa_ref, wb_ref, gsb_ref, gpost_ref, gffn_ref):
    osb = _rms(o.astype(F32), gsb_ref[...]).astype(BF16)
    mix = jnp.dot(ys, wa_ref[...], preferred_element_type=F32)
    mix = mix + jnp.dot(osb, wb_ref[...], preferred_element_type=F32)
    h1 = h + _rms(mix, gpost_ref[...])
    return h1, _rms(h1, gffn_ref[...]).astype(BF16)


def _outproj_kernel(ys_ref, o_ref, h_ref, wa_ref, wb_ref, gsb_ref, gpost_ref, gffn_ref,
                    h1_ref, xn2_ref, *, nsub):
    sub = h_ref.shape[1] // nsub
    rows = [pl.ds(k * sub, sub) for k in range(nsub)]
    osb = [_rms(o_ref[0, r, :].astype(F32), gsb_ref[...]).astype(BF16) for r in rows]
    mix = [jnp.dot(ys_ref[0, r, :], wa_ref[...], preferred_element_type=F32)
           + jnp.dot(osb[k], wb_ref[...], preferred_element_type=F32) for k, r in enumerate(rows)]
    for k, r in enumerate(rows):
        h1 = h_ref[0, r, :] + _rms(mix[k], gpost_ref[...])
        h1_ref[0, r, :] = h1
        xn2_ref[0, r, :] = _rms(h1, gffn_ref[...]).astype(BF16)


def _outproj_meta_kernel(ys_ref, o_ref, hm_ref, wa_ref, wb_ref, gsb_ref, gpost_ref, gffn_ref, xn2_ref):
    _, xn2 = _outproj_rows(ys_ref[0], o_ref[0], hm_ref[...], wa_ref, wb_ref, gsb_ref, gpost_ref, gffn_ref)
    xn2_ref[0] = xn2


def _outproj(y_ssd, o_sb, x, hm, w_a, w_b, g_sb, g_post, g_ffn):
    bsz, seq, _ = x.shape
    lp = y_ssd.shape[1]
    tm = OUTPROJ_TM
    per_seq = seq // tm
    weights = [
        _const_spec((SSD_INNER, D_MODEL)),
        _const_spec((SB_WIDTH, D_MODEL)),
        _const_spec((1, SB_WIDTH)),
        _const_spec((1, D_MODEL)),
        _const_spec((1, D_MODEL)),
    ]
    params = pltpu.CompilerParams(dimension_semantics=("arbitrary",), vmem_limit_bytes=V7X_VMEM_LIMIT)
    tile = lambda t: (t // per_seq, t % per_seq, 0)
    h1, xn2 = pl.pallas_call(
        functools.partial(_outproj_kernel, nsub=OUTPROJ_SUBTILES),
        grid=(bsz * per_seq,),
        in_specs=[
            pl.BlockSpec((1, tm, SSD_INNER), tile),
            pl.BlockSpec((1, tm, SB_WIDTH), tile),
            pl.BlockSpec((1, tm, D_MODEL), tile),
        ] + weights,
        out_specs=[pl.BlockSpec((1, tm, D_MODEL), tile), pl.BlockSpec((1, tm, D_MODEL), tile)],
        out_shape=[jax.ShapeDtypeStruct((bsz, seq, D_MODEL), F32),
                   jax.ShapeDtypeStruct((bsz, seq, D_MODEL), BF16)],
        compiler_params=params,
        name="outproj",
    )(y_ssd, o_sb, x, w_a, w_b, g_sb, g_post, g_ffn)

    last_blk = lambda b: (0, lp // BLK - 1, 0)
    xn2_meta = pl.pallas_call(
        _outproj_meta_kernel,
        grid=(1,),
        in_specs=[
            pl.BlockSpec((1, BLK, SSD_INNER), last_blk),
            pl.BlockSpec((1, BLK, SB_WIDTH), last_blk),
            _const_spec((BLK, D_MODEL)),
        ] + weights,
        out_specs=pl.BlockSpec((1, BLK, D_MODEL), lambda b: (0, 0, 0)),
        out_shape=jax.ShapeDtypeStruct((1, BLK, D_MODEL), BF16),
        compiler_params=params,
        name="outproj_meta",
    )(y_ssd, o_sb, hm, w_a, w_b, g_sb, g_post, g_ffn)
    return h1, xn2, xn2_meta


def _gelu_tanh(x):
    return 0.5 * x * (1.0 + jnp.tanh(math.sqrt(2.0 / math.pi) * (x + 0.044715 * (x * x * x))))


def _ffn_kernel(xn_ref, xm_ref, h1_ref, wup_ref, cw_ref, cb_ref, wd_ref, g_ref, out_ref,
                gbuf0, gbuf1, ubuf0, ubuf1, halo, act_scr, *, tm, tf):
    nch = D_FF // tf
    i = pl.program_id(1)
    gbufs, ubufs = (gbuf0, gbuf1), (ubuf0, ubuf1)
    cols = [slice(c * tf, (c + 1) * tf) for c in range(nch)]
    up_cols = [slice(D_FF + c * tf, D_FF + (c + 1) * tf) for c in range(nch)]

    @pl.when(i == 0)
    def _():
        for c in range(nch):
            gm = jnp.dot(xm_ref[0], wup_ref[:, cols[c]], preferred_element_type=F32)
            halo[c] = gm[N_META - 8:, :]

    def project(c, slot):
        gbuf, ubuf = gbufs[slot], ubufs[slot]
        xn = xn_ref[0]
        gate = jnp.dot(xn, wup_ref[:, cols[c]], preferred_element_type=F32)
        gbuf[0:8, :] = halo[c]
        gbuf[8:8 + tm, :] = gate
        halo[c] = gate[tm - 8:, :]
        ubuf[...] = jnp.dot(xn, wup_ref[:, up_cols[c]], preferred_element_type=F32)

    def activate(c, slot):
        gbuf, ubuf = gbufs[slot], ubufs[slot]
        cw = cw_ref[:, cols[c]]
        conv = cb_ref[:, cols[c]] + cw[2:3, :] * gbuf[8:8 + tm, :]
        conv = conv + cw[1:2, :] * gbuf[7:7 + tm, :] + cw[0:1, :] * gbuf[6:6 + tm, :]
        act_scr[c] = (_gelu_tanh(conv) * ubuf[...]).astype(BF16)

    project(0, 0)
    for c in range(nch - 1):
        project(c + 1, (c + 1) % 2)
        activate(c, c % 2)
    activate(nch - 1, (nch - 1) % 2)
    act = jnp.concatenate([act_scr[c] for c in range(nch)], axis=1)
    down = jnp.dot(act, wd_ref[...], preferred_element_type=F32)
    out_ref[0] = h1_ref[0] + _rms(down, g_ref[...])


def _ffn(xn2, xn2_meta, h1, w_up, conv_w, conv_b, w_down, g_post):
    bsz, seq, _ = xn2.shape
    tm, tf = FFN_TM, FFN_TF
    nch = D_FF // tf
    xblk = lambda b, i: (b, i, 0)
    meta = lambda b, i: (0, BLK // N_META - 1, 0)
    return pl.pallas_call(
        functools.partial(_ffn_kernel, tm=tm, tf=tf),
        grid=(bsz, seq // tm),
        in_specs=[
            pl.BlockSpec((1, tm, D_MODEL), xblk),
            pl.BlockSpec((1, N_META, D_MODEL), meta),
            pl.BlockSpec((1, tm, D_MODEL), xblk),
            _const_spec((D_MODEL, 2 * D_FF)),
            _const_spec((FFN_CONV, D_FF)),
            _const_spec((1, D_FF)),
            _const_spec((D_FF, D_MODEL)),
            _const_spec((1, D_MODEL)),
        ],
        out_specs=pl.BlockSpec((1, tm, D_MODEL), xblk),
        out_shape=jax.ShapeDtypeStruct((bsz, seq, D_MODEL), F32),
        scratch_shapes=[
            pltpu.VMEM((tm + 8, tf), F32),
            pltpu.VMEM((tm + 8, tf), F32),
            pltpu.VMEM((tm, tf), F32),
            pltpu.VMEM((tm, tf), F32),
            pltpu.VMEM((nch, 8, tf), F32),
            pltpu.VMEM((nch, tm, tf), BF16),
        ],
        compiler_params=pltpu.CompilerParams(
            dimension_semantics=("arbitrary", "arbitrary"), vmem_limit_bytes=V7X_VMEM_LIMIT),
        name="ffn",
    )(xn2, xn2_meta, h1, w_up, conv_w, conv_b, w_down, g_post)


def _pad_lanes(v, n):
    return jnp.pad(v, ((0, 0), (0, n - v.shape[-1])))


def kernel(x, meta_tokens, mix_pre_g, w_in, ssd_conv_w, ssd_conv_b, ssd_dt_bias, ssd_a_log, ssd_d,
           ssd_norm_g, sb_norm_g, w_out, mix_post_g, ffn_pre_g, w_up, ffn_conv_w, ffn_conv_b,
           w_down, ffn_post_g):
    bsz, seq, d = x.shape
    depth = w_in.shape[0]
    assert depth == 1 and d == D_MODEL and seq % OUTPROJ_TM == 0

    hm = jnp.concatenate([jnp.zeros((PAD, d), x.dtype), meta_tokens.astype(x.dtype)], axis=0)

    l = 0
    off_xbc = SSD_INNER
    off_dt = off_xbc + XBC_WIDTH
    off_q = off_dt + SSD_HEADS
    w = w_in[l]
    w_main = jnp.concatenate([w[:, :off_dt], w[:, off_q:]], axis=1).astype(BF16)
    w_dt = _pad_lanes(w[:, off_dt:off_q], DT_LANES).astype(BF16)
    z, xbc, q, k, v0, v1, dt_raw = _inproj(x, hm, mix_pre_g[l][None], w_main, w_dt)

    head_of_col = jnp.arange(SSD_INNER) // SSD_HEAD_DIM
    expand = (jnp.arange(DT_LANES)[:, None] == head_of_col[None, :]).astype(BF16)
    d_exp = jnp.repeat(ssd_d[l].astype(F32), SSD_HEAD_DIM)[None]
    y_ssd = _ssd(xbc, z, dt_raw, ssd_conv_w[l], ssd_conv_b[l][None],
                 _pad_lanes(ssd_dt_bias[l][None], DT_LANES), _pad_lanes(ssd_a_log[l][None], DT_LANES),
                 d_exp, ssd_norm_g[l][None], expand)

    kk = jnp.arange(BLK)
    tt = (kk[:, None] > kk[None, :]).astype(BF16)
    o_sb = _sb_attention(q, k, v0, v1, tt)

    wo = w_out[l].astype(BF16)
    h1, xn2, xn2_meta = _outproj(y_ssd, o_sb, x, hm, wo[:SSD_INNER], wo[SSD_INNER:], sb_norm_g[l][None],
                                 mix_post_g[l][None], ffn_pre_g[l][None])

    return _ffn(xn2, xn2_meta, h1, w_up[l].astype(BF16), ffn_conv_w[l], ffn_conv_b[l][None],
                w_down[l].astype(BF16), ffn_post_g[l][None])
```

```python
import functools
import math

import jax
import jax.numpy as jnp
from jax import lax
from jax.experimental import pallas as pl
from jax.experimental.pallas import tpu as pltpu

F32 = jnp.float32
BF16 = jnp.bfloat16

D_MODEL = 1024
N_META = 16
BLK = 128
PAD = BLK - N_META
SSD_HEADS = 16
SSD_HEAD_DIM = 64
SSD_GROUPS = 2
SSD_STATE = 128
SSD_INNER = 1024
SSD_CONV = 4
XBC_WIDTH = SSD_INNER + 2 * SSD_GROUPS * SSD_STATE
SB_WIDTH = 1024
SB_HEAD_DIM = 64
D_FF = 2816
FFN_CONV = 3
EPS = 1e-6
DT_LANES = 128
GROUP_COLS = SSD_INNER // SSD_GROUPS
LOG2E = 1.4426950408889634

V7X_VMEM_LIMIT = 56 * 1024 * 1024

INPROJ_TM = 512
OUTPROJ_TM = 1024
OUTPROJ_SUBTILES = 4
FFN_TM = 512
FFN_TF = 256
SSD_SEQS_PER_STEP = 2
SB_Q_BLOCKS_PER_STEP = 3


def _rms(x, g):
    return x * lax.rsqrt(jnp.mean(x * x, axis=-1, keepdims=True) + EPS) * g


def _const_spec(shape):
    nd = len(shape)
    return pl.BlockSpec(shape, lambda *_: (0,) * nd, pipeline_mode=pl.Buffered(1))


_PROJ_SPLITS = (("z", SSD_INNER), ("xbc", XBC_WIDTH), ("q", SB_WIDTH), ("k", SB_WIDTH), ("v", SB_WIDTH))
_INPROJ_OUT_WIDTHS = (SSD_INNER, XBC_WIDTH, SB_WIDTH, SB_WIDTH, SB_WIDTH, SB_WIDTH, DT_LANES)
_INPROJ_OUT_DTYPES = (BF16, BF16, BF16, BF16, BF16, BF16, F32)


def _inproj_rows(h, g_ref, w_ref, wdt_ref, out_refs):
    z_ref, xbc_ref, q_ref, k_ref, v0_ref, v1_ref, dt_ref = out_refs
    xn = _rms(h, g_ref[...]).astype(BF16)
    lo = 0
    proj = {}
    for name, width in _PROJ_SPLITS:
        proj[name] = jnp.dot(xn, w_ref[:, lo:lo + width], preferred_element_type=F32)
        lo += width
    z_ref[0] = proj["z"].astype(BF16)
    xbc_ref[0] = proj["xbc"].astype(BF16)
    q_ref[0] = (proj["q"] * (LOG2E / math.sqrt(SB_HEAD_DIM))).astype(BF16)
    k_ref[0] = proj["k"].astype(BF16)
    v = proj["v"].astype(BF16)
    lane = lax.broadcasted_iota(jnp.int32, v.shape, 1)
    even_head = (lane & SB_HEAD_DIM) == 0
    zero = jnp.zeros_like(v)
    v0_ref[0] = jnp.where(even_head, v, zero)
    v1_ref[0] = jnp.where(even_head, zero, v)
    dt_ref[0] = jnp.dot(xn, wdt_ref[...], preferred_element_type=F32)


def _inproj_kernel(x_ref, g_ref, w_ref, wdt_ref, *out_refs):
    _inproj_rows(x_ref[...], g_ref, w_ref, wdt_ref, out_refs)


def _inproj_meta_kernel(hm_ref, g_ref, w_ref, wdt_ref, *refs):
    nout = len(_INPROJ_OUT_WIDTHS)
    out_refs, scratch = refs[nout:2 * nout], refs[2 * nout:]

    @pl.when(pl.program_id(0) == 0)
    def _():
        _inproj_rows(hm_ref[...], g_ref, w_ref, wdt_ref, scratch)

    for out_ref, scr in zip(out_refs, scratch):
        out_ref[...] = scr[...]


def _inproj(x, hm, g, w_main, w_dt):
    bsz, seq, _ = x.shape
    lp = seq + BLK
    tm = INPROJ_TM
    per_seq = seq // tm
    n_main = w_main.shape[1]
    out_shape = [jax.ShapeDtypeStruct((bsz, lp, width), dt)
                 for width, dt in zip(_INPROJ_OUT_WIDTHS, _INPROJ_OUT_DTYPES)]
    weights = [_const_spec((1, D_MODEL)), _const_spec((D_MODEL, n_main)), _const_spec((D_MODEL, DT_LANES))]
    params = pltpu.CompilerParams(dimension_semantics=("arbitrary",), vmem_limit_bytes=V7X_VMEM_LIMIT)

    tile = lambda t: (t // per_seq, t % per_seq, 0)
    outs = pl.pallas_call(
        _inproj_kernel,
        grid=(bsz * per_seq,),
        in_specs=[pl.BlockSpec((tm, D_MODEL), lambda t: (t, 0))] + weights,
        out_specs=[pl.BlockSpec((1, tm, width), tile) for width in _INPROJ_OUT_WIDTHS],
        out_shape=out_shape,
        compiler_params=params,
        name="inproj",
    )(x.reshape(bsz * seq, D_MODEL), g, w_main, w_dt)

    nout = len(outs)
    last_blk = lambda b: (b, lp // BLK - 1, 0)
    return pl.pallas_call(
        _inproj_meta_kernel,
        grid=(bsz,),
        in_specs=[_const_spec((BLK, D_MODEL))] + weights + [pl.BlockSpec(memory_space=pl.ANY)] * nout,
        out_specs=[pl.BlockSpec((1, BLK, width), last_blk) for width in _INPROJ_OUT_WIDTHS],
        out_shape=out_shape,
        input_output_aliases={4 + j: j for j in range(nout)},
        scratch_shapes=[pltpu.VMEM((1, BLK, width), dt)
                        for width, dt in zip(_INPROJ_OUT_WIDTHS, _INPROJ_OUT_DTYPES)],
        compiler_params=params,
        name="inproj_meta",
    )(hm, g, w_main, w_dt, *outs)


def _dot_f32_by_01(lhs, rhs, f32_side):
    x = lhs if f32_side == "lhs" else rhs
    p1 = x.astype(BF16)
    r1 = x - p1.astype(F32)
    p2 = r1.astype(BF16)
    p3 = (r1 - p2.astype(F32)).astype(BF16)
    out = None
    for piece in (p1, p2, p3):
        ops = (piece, rhs) if f32_side == "lhs" else (lhs, piece)
        term = jnp.dot(*ops, preferred_element_type=F32)
        out = term if out is None else out + term
    return out


def _ssd_kernel(xbc_ref, z_ref, dt_ref, cw_ref, cb_ref, dtb_ref, alog_ref, dexp_ref, g_ref, e_ref,
                out_ref, prev, state, *, nseq):
    c = pl.program_id(1)

    @pl.when(c == 0)
    def _():
        prev[...] = jnp.zeros_like(prev)
        state[...] = jnp.zeros_like(state)

    seqs = range(nseq)
    groups = range(SSD_GROUPS)
    row = lax.broadcasted_iota(jnp.int32, (BLK, 1), 0)
    real = jnp.logical_or(c > 0, row >= PAD)
    lane = lax.broadcasted_iota(jnp.int32, (BLK, DT_LANES), 1)
    dt_valid = jnp.logical_and(real, lane < SSD_HEADS)
    ri = lax.broadcasted_iota(jnp.int32, (BLK, BLK), 0)
    ci = lax.broadcasted_iota(jnp.int32, (BLK, BLK), 1)
    causal = ri >= ci
    tri = jnp.where(causal, 1.0, 0.0).astype(BF16)
    half = ci < SSD_HEAD_DIM
    cw = cw_ref[...]
    e = e_ref[...]
    a_neg = -jnp.exp(alog_ref[...])

    xc, dt, adt = [], [], []
    for s in seqs:
        x_raw = xbc_ref[s].astype(F32)
        prev[s, 8:8 + BLK, :] = x_raw
        acc = cb_ref[...] + cw[3:4, :] * x_raw
        for tap in range(SSD_CONV - 1):
            acc = acc + cw[tap:tap + 1, :] * prev[s, 5 + tap:5 + tap + BLK, :]
        prev[s, 0:8, :] = x_raw[BLK - 8:, :]
        xc.append(jnp.where(real, acc * jax.nn.sigmoid(acc), 0.0))
        dts = jax.nn.softplus(dt_ref[s] + dtb_ref[...])
        dt.append(jnp.where(dt_valid, dts, 0.0))
        adt.append(dt[s] * a_neg)
    xs = [xc[s][:, :SSD_INNER] for s in seqs]

    acs = [_dot_f32_by_01(tri, adt[s], f32_side="rhs") for s in seqs]
    acs_t = [acs[s].T for s in seqs]
    dt_exp = [_dot_f32_by_01(dt[s], e, f32_side="lhs") for s in seqs]
    acs_exp = [_dot_f32_by_01(acs[s], e, f32_side="lhs") for s in seqs]

    xdt_b, xdec, off_scale, chunk_decay, b_t, c_gb, b_gb = [], [], [], [], [], [], []
    for s in seqs:
        last = acs_exp[s][BLK - 1:BLK, :]
        xdt = xs[s] * dt_exp[s]
        xdec.append((xdt * jnp.exp(last - acs_exp[s])).astype(BF16))
        xdt_b.append(xdt.astype(BF16))
        off_scale.append(jnp.exp(acs_exp[s]))
        chunk_decay.append(jnp.exp(last))
        b_g = [xc[s][:, SSD_INNER + g * SSD_STATE:SSD_INNER + (g + 1) * SSD_STATE] for g in groups]
        c_g = [xc[s][:, SSD_INNER + (SSD_GROUPS + g) * SSD_STATE:
                     SSD_INNER + (SSD_GROUPS + g + 1) * SSD_STATE] for g in groups]
        b_gb.append([x.astype(BF16) for x in b_g])
        c_gb.append([x.astype(BF16) for x in c_g])
        b_t.append([x.T.astype(BF16) for x in b_g])

    gcols = [slice(g * GROUP_COLS, (g + 1) * GROUP_COLS) for g in groups]
    cb_mat = [[lax.dot_general(c_gb[s][g], b_gb[s][g], (((1,), (1,)), ((), ())),
                               preferred_element_type=F32) for g in groups] for s in seqs]
    s_prev = [[state[s, :, gcols[g]] for g in groups] for s in seqs]
    y_off = [[jnp.dot(c_gb[s][g], s_prev[s][g].astype(BF16), preferred_element_type=F32)
              for g in groups] for s in seqs]

    heads_per_group = SSD_HEADS // SSD_GROUPS
    decay_mats = [[None] * SSD_HEADS for _ in seqs]
    for s in seqs:
        for hh in range(SSD_HEADS):
            seg = acs[s][:, hh:hh + 1] - acs_t[s][hh:hh + 1, :]
            ldec = jnp.exp(jnp.where(causal, seg, -1e30))
            decay_mats[s][hh] = (cb_mat[s][hh // heads_per_group] * ldec).astype(BF16)

    y_diag = [[jnp.dot(decay_mats[s][hh], xdt_b[s][:, (hh // 2) * BLK:(hh // 2 + 1) * BLK],
                       preferred_element_type=F32) for hh in range(SSD_HEADS)] for s in seqs]
    new = [[jnp.dot(b_t[s][g], xdec[s][:, gcols[g]], preferred_element_type=F32)
            for g in groups] for s in seqs]

    for s in seqs:
        y_cols = []
        for g in groups:
            state[s, :, gcols[g]] = s_prev[s][g] * chunk_decay[s][:, gcols[g]] + new[s][g]
            yo = y_off[s][g] * off_scale[s][:, gcols[g]]
            for pair in range(GROUP_COLS // BLK):
                h0 = (g * GROUP_COLS + pair * BLK) // SSD_HEAD_DIM
                y_cols.append(jnp.where(half, y_diag[s][h0], y_diag[s][h0 + 1])
                              + yo[:, pair * BLK:(pair + 1) * BLK])
        y = jnp.concatenate(y_cols, axis=1) + xs[s] * dexp_ref[...]
        zf = z_ref[s].astype(F32)
        y = y * (zf * jax.nn.sigmoid(zf))
        out_ref[s] = _rms(y, g_ref[...]).astype(BF16)


def _ssd(xbc, z, dt_raw, conv_w, conv_b, dt_bias, a_log, d_exp, norm_g, expand):
    bsz, lp, _ = xbc.shape
    nblk = lp // BLK
    nseq = SSD_SEQS_PER_STEP
    assert bsz % nseq == 0
    blk = lambda b, c: (b, (c + nblk - 1) % nblk, 0)
    return pl.pallas_call(
        functools.partial(_ssd_kernel, nseq=nseq),
        grid=(bsz // nseq, nblk),
        in_specs=[
            pl.BlockSpec((nseq, BLK, XBC_WIDTH), blk),
            pl.BlockSpec((nseq, BLK, SSD_INNER), blk),
            pl.BlockSpec((nseq, BLK, DT_LANES), blk),
            _const_spec((SSD_CONV, XBC_WIDTH)),
            _const_spec((1, XBC_WIDTH)),
            _const_spec((1, DT_LANES)),
            _const_spec((1, DT_LANES)),
            _const_spec((1, SSD_INNER)),
            _const_spec((1, SSD_INNER)),
            _const_spec((DT_LANES, SSD_INNER)),
        ],
        out_specs=pl.BlockSpec((nseq, BLK, SSD_INNER), blk),
        out_shape=jax.ShapeDtypeStruct((bsz, lp, SSD_INNER), BF16),
        scratch_shapes=[
            pltpu.VMEM((nseq, BLK + 8, XBC_WIDTH), F32),
            pltpu.VMEM((nseq, SSD_STATE, SSD_INNER), F32),
        ],
        compiler_params=pltpu.CompilerParams(
            dimension_semantics=("arbitrary", "arbitrary"), vmem_limit_bytes=V7X_VMEM_LIMIT),
        name="ssd",
    )(xbc, z, dt_raw, conv_w, conv_b, dt_bias, a_log, d_exp, norm_g, expand)


SB_DEAD_LOG2 = -151.0
SB_LEAD = 3
FULL, TOP, BOT = slice(0, BLK), slice(0, BLK // 2), slice(BLK // 2, BLK)


def _sb_kv_copies(hbm_refs, bufs, sems, batch, part, nblk, qpb):
    nparts = nblk // qpb
    seq_rows = (nblk - 1) * BLK
    if part == nparts:
        rows = pl.ds(seq_rows, BLK)
    else:
        start = part * qpb * BLK
        rows = pl.ds(start, min(qpb * BLK, seq_rows - start))
    return [pltpu.make_async_copy(src.at[batch, rows, :], dst.at[rows, :], sems.at[a, part])
            for a, (src, dst) in enumerate(zip(hbm_refs, bufs))]


def _sb_kernel(q_ref, k_hbm, v0_hbm, v1_hbm, tt_ref, o_ref, k_ref, v0_ref, v1_ref, kv_sems,
               qm_scr, run_scr, acc_scr, worst_scr, *, nblk, npairs, qpb):
    batch, step = pl.program_id(0), pl.program_id(1)
    nparts = nblk // qpb
    hbm_refs, bufs = (k_hbm, v0_hbm, v1_hbm), (k_ref, v0_ref, v1_ref)

    @pl.when(step == 0)
    def _():
        for part in [nparts] + list(range(nparts)):
            for cp in _sb_kv_copies(hbm_refs, bufs, kv_sems, batch, part, nblk, qpb):
                cp.start()
        for cp in _sb_kv_copies(hbm_refs, bufs, kv_sems, batch, nparts, nblk, qpb):
            cp.wait()

    for part in range(nparts):
        @pl.when(step == part)
        def _(part=part):
            for cp in _sb_kv_copies(hbm_refs, bufs, kv_sems, batch, part, nblk, qpb):
                cp.wait()

    lane = lax.broadcasted_iota(jnp.int32, (BLK, BLK), 1)
    sub = lax.broadcasted_iota(jnp.int32, (BLK, BLK), 0)
    first = lane < SB_HEAD_DIM
    strictly_older = lane < sub
    tt = tt_ref[...]

    def query_block(r, carry):
        qi = (step * qpb + r + 1) % nblk
        rows = pl.ds(pl.multiple_of(r * BLK, BLK), BLK)
        for p in range(npairs):
            qp = q_ref[0, rows, p * BLK:(p + 1) * BLK]
            zero = jnp.zeros_like(qp)
            qm_scr[p, :BLK, :] = jnp.where(first, qp, zero)
            qm_scr[p, BLK:, :] = jnp.where(first, zero, qp)

        def process(blocks, init, check, row_sets=None):
            offs = [pl.multiple_of(((j + nblk - 1) % nblk) * BLK, BLK) for j in blocks]
            nb = range(len(blocks))
            row_sets = row_sets or [FULL] * len(blocks)
            heads = range(2 * npairs)
            scores = {}
            for b in nb:
                rs = row_sets[b]
                for p in range(npairs):
                    if rs is FULL:
                        lhs = qm_scr[p]
                    else:
                        lhs = jnp.concatenate([qm_scr[p, rs, :],
                                               qm_scr[p, BLK + rs.start:BLK + rs.stop, :]], axis=0)
                    scores[b, p] = lax.dot_general(
                        lhs, k_ref[pl.ds(offs[b], BLK), p * BLK:(p + 1) * BLK],
                        (((1,), (1,)), ((), ())), preferred_element_type=F32)
            log_betas, log_keeps, totals = {}, {}, {}
            for b in nb:
                n = row_sets[b].stop - row_sets[b].start
                for h in heads:
                    s = scores[b, h // 2][(h % 2) * n:(h % 2 + 1) * n, :]
                    neg_part = jnp.minimum(s, 0.0)
                    neg_pos = neg_part - s
                    sp = jnp.log(1.0 + jnp.exp2(neg_part + neg_pos)) * LOG2E
                    log_beta = neg_part - sp
                    log_keep = neg_pos - sp
                    if init and b == 0:
                        log_keep = jnp.where(strictly_older, log_keep, 0.0)
                    log_betas[b, h] = log_beta
                    log_keeps[b, h] = log_keep.astype(BF16)
                    totals[b, h] = jnp.sum(log_keep, axis=1, keepdims=True)
            sums = [jnp.dot(jnp.concatenate([log_keeps[b, h] for h in heads], axis=0), tt,
                            preferred_element_type=F32) for b in nb]
            weights = {}
            worst = None
            for h in heads:
                run = None if init else run_scr[h]
                for b in nb:
                    rs = row_sets[b]
                    n = rs.stop - rs.start
                    after, total = sums[b][h * n:(h + 1) * n, :], totals[b, h]
                    if run is None:
                        w = jnp.where(strictly_older, jnp.exp2(log_betas[b, h] + after), 0.0)
                        run = jnp.broadcast_to(total, (BLK, BLK))
                    else:
                        part = run if rs is FULL else run[rs, :]
                        w = jnp.exp2(log_betas[b, h] + (after + part))
                        part = part + total
                        if rs is FULL:
                            run = part
                        elif rs is TOP:
                            run = jnp.concatenate([part, run[BOT, :]], axis=0)
                        else:
                            run = jnp.concatenate([run[TOP, :], part], axis=0)
                    weights[b, h] = w.astype(BF16)
                run_scr[h] = run
                if check:
                    worst = run if worst is None else jnp.maximum(worst, run)
            for p in range(npairs):
                pv = None if init else acc_scr[p]
                for b in nb:
                    rs = row_sets[b]
                    v_heads = jnp.concatenate(
                        [v0_ref[pl.ds(offs[b], BLK), p * BLK:(p + 1) * BLK],
                         v1_ref[pl.ds(offs[b], BLK), p * BLK:(p + 1) * BLK]], axis=0)
                    w_heads = jnp.concatenate([weights[b, 2 * p], weights[b, 2 * p + 1]], axis=1)
                    contrib = jnp.dot(w_heads, v_heads, preferred_element_type=F32)
                    if pv is None:
                        pv = contrib
                    elif rs is FULL:
                        pv = pv + contrib
                    elif rs is TOP:
                        pv = jnp.concatenate([pv[TOP, :] + contrib, pv[BOT, :]], axis=0)
                    else:
                        pv = jnp.concatenate([pv[TOP, :], pv[BOT, :] + contrib], axis=0)
                acc_scr[p] = pv
            if check:
                worst_scr[0] = jnp.max(worst[TOP, :])
                worst_scr[1] = jnp.max(worst[BOT, :])

        for lead in range(1, SB_LEAD + 1):
            last = lead == SB_LEAD

            @pl.when(qi >= lead - 1 if last else qi == lead - 1)
            def _(lead=lead, last=last):
                if last:
                    process([qi - i for i in range(lead)], True, True, [FULL] * (lead - 1) + [TOP])
                else:
                    process([qi - i for i in range(lead)], True, False)
                    worst_scr[0] = jnp.float32(SB_DEAD_LOG2)
                    worst_scr[1] = jnp.float32(SB_DEAD_LOG2)

        @pl.when(jnp.logical_and(qi >= SB_LEAD - 1, worst_scr[1] > SB_DEAD_LOG2))
        def _():
            process([qi - (SB_LEAD - 1)], False, True, [BOT])

        def cond(st):
            return jnp.logical_and(st[0] >= 0, st[1] > SB_DEAD_LOG2)

        def body(st):
            process([st[0]], False, True)
            return st[0] - 1, jnp.maximum(worst_scr[0], worst_scr[1])

        lax.while_loop(cond, body, (qi - SB_LEAD, jnp.maximum(worst_scr[0], worst_scr[1])))
        for p in range(npairs):
            o_ref[0, rows, p * BLK:(p + 1) * BLK] = acc_scr[p].astype(BF16)
        return carry

    lax.fori_loop(0, qpb, query_block, 0)


def _sb_attention(q, k, v0, v1, tt):
    bsz, lp, width = q.shape
    nblk = lp // BLK
    qpb = SB_Q_BLOCKS_PER_STEP
    assert nblk % qpb == 0
    npairs = width // BLK
    nparts = nblk // qpb
    qblk = lambda b, g: (b, g, 0)
    kv_hbm = pl.BlockSpec(memory_space=pl.ANY)
    return pl.pallas_call(
        functools.partial(_sb_kernel, nblk=nblk, npairs=npairs, qpb=qpb),
        grid=(bsz, nparts),
        in_specs=[
            pl.BlockSpec((1, qpb * BLK, width), qblk),
            kv_hbm,
            kv_hbm,
            kv_hbm,
            _const_spec((BLK, BLK)),
        ],
        out_specs=pl.BlockSpec((1, qpb * BLK, width), qblk),
        out_shape=jax.ShapeDtypeStruct((bsz, lp, width), BF16),
        scratch_shapes=[
            pltpu.VMEM((lp, width), BF16),
            pltpu.VMEM((lp, width), BF16),
            pltpu.VMEM((lp, width), BF16),
            pltpu.SemaphoreType.DMA((3, nparts + 1)),
            pltpu.VMEM((npairs, 2 * BLK, BLK), BF16),
            pltpu.VMEM((2 * npairs, BLK, BLK), F32),
            pltpu.VMEM((npairs, BLK, BLK), F32),
            pltpu.SMEM((2,), F32),
        ],
        compiler_params=pltpu.CompilerParams(
            dimension_semantics=("arbitrary", "arbitrary"), vmem_limit_bytes=V7X_VMEM_LIMIT),
        name="sb_attention",
    )(q, k, v0, v1, tt)


def _outproj_rows(ys, o, h, wa_ref, wb_ref, gsb_ref, gpost_ref, gffn_ref):
    osb = _rms(o.astype(F32), gsb_ref[...]).astype(BF16)
    mix = jnp.dot(ys, wa_ref[...], preferred_element_type=F32)
    mix = mix + jnp.dot(osb, wb_ref[...], preferred_element_type=F32)
    h1 = h + _rms(mix, gpost_ref[...])
    return h1, _rms(h1, gffn_ref[...]).astype(BF16)


def _outproj_kernel(ys_ref, o_ref, h_ref, wa_ref, wb_ref, gsb_ref, gpost_ref, gffn_ref,
                    h1_ref, xn2_ref, *, nsub):
    sub = h_ref.shape[1] // nsub
    rows = [pl.ds(k * sub, sub) for k in range(nsub)]
    osb = [_rms(o_ref[0, r, :].astype(F32), gsb_ref[...]).astype(BF16) for r in rows]
    mix = [jnp.dot(ys_ref[0, r, :], wa_ref[...], preferred_element_type=F32)
           + jnp.dot(osb[k], wb_ref[...], preferred_element_type=F32) for k, r in enumerate(rows)]
    for k, r in enumerate(rows):
        h1 = h_ref[0, r, :] + _rms(mix[k], gpost_ref[...])
        h1_ref[0, r, :] = h1
        xn2_ref[0, r, :] = _rms(h1, gffn_ref[...]).astype(BF16)


def _outproj_meta_kernel(ys_ref, o_ref, hm_ref, wa_ref, wb_ref, gsb_ref, gpost_ref, gffn_ref, xn2_ref):
    _, xn2 = _outproj_rows(ys_ref[0], o_ref[0], hm_ref[...], wa_ref, wb_ref, gsb_ref, gpost_ref, gffn_ref)
    xn2_ref[0] = xn2


def _outproj(y_ssd, o_sb, x, hm, w_a, w_b, g_sb, g_post, g_ffn):
    bsz, seq, _ = x.shape
    lp = y_ssd.shape[1]
    tm = OUTPROJ_TM
    per_seq = seq // tm
    weights = [
        _const_spec((SSD_INNER, D_MODEL)),
        _const_spec((SB_WIDTH, D_MODEL)),
        _const_spec((1, SB_WIDTH)),
        _const_spec((1, D_MODEL)),
        _const_spec((1, D_MODEL)),
    ]
    params = pltpu.CompilerParams(dimension_semantics=("arbitrary",), vmem_limit_bytes=V7X_VMEM_LIMIT)
    tile = lambda t: (t // per_seq, t % per_seq, 0)
    h1, xn2 = pl.pallas_call(
        functools.partial(_outproj_kernel, nsub=OUTPROJ_SUBTILES),
        grid=(bsz * per_seq,),
        in_specs=[
            pl.BlockSpec((1, tm, SSD_INNER), tile),
            pl.BlockSpec((1, tm, SB_WIDTH), tile),
            pl.BlockSpec((1, tm, D_MODEL), tile),
        ] + weights,
        out_specs=[pl.BlockSpec((1, tm, D_MODEL), tile), pl.BlockSpec((1, tm, D_MODEL), tile)],
        out_shape=[jax.ShapeDtypeStruct((bsz, seq, D_MODEL), F32),
                   jax.ShapeDtypeStruct((bsz, seq, D_MODEL), BF16)],
        compiler_params=params,
        name="outproj",
    )(y_ssd, o_sb, x, w_a, w_b, g_sb, g_post, g_ffn)

    last_blk = lambda b: (0, lp // BLK - 1, 0)
    xn2_meta = pl.pallas_call(
        _outproj_meta_kernel,
        grid=(1,),
        in_specs=[
            pl.BlockSpec((1, BLK, SSD_INNER), last_blk),
            pl.BlockSpec((1, BLK, SB_WIDTH), last_blk),
            _const_spec((BLK, D_MODEL)),
        ] + weights,
        out_specs=pl.BlockSpec((1, BLK, D_MODEL), lambda b: (0, 0, 0)),
        out_shape=jax.ShapeDtypeStruct((1, BLK, D_MODEL), BF16),
        compiler_params=params,
        name="outproj_meta",
    )(y_ssd, o_sb, hm, w_a, w_b, g_sb, g_post, g_ffn)
    return h1, xn2, xn2_meta


def _gelu_tanh(x):
    return 0.5 * x * (1.0 + jnp.tanh(math.sqrt(2.0 / math.pi) * (x + 0.044715 * (x * x * x))))


def _ffn_kernel(xn_ref, xm_ref, h1_ref, wup_ref, cw_ref, cb_ref, wd_ref, g_ref, out_ref,
                gbuf0, gbuf1, ubuf0, ubuf1, halo, act_scr, *, tm, tf):
    nch = D_FF // tf
    i = pl.program_id(1)
    gbufs, ubufs = (gbuf0, gbuf1), (ubuf0, ubuf1)
    cols = [slice(c * tf, (c + 1) * tf) for c in range(nch)]
    up_cols = [slice(D_FF + c * tf, D_FF + (c + 1) * tf) for c in range(nch)]

    @pl.when(i == 0)
    def _():
        for c in range(nch):
            gm = jnp.dot(xm_ref[0], wup_ref[:, cols[c]], preferred_element_type=F32)
            halo[c] = gm[N_META - 8:, :]

    def project(c, slot):
        gbuf, ubuf = gbufs[slot], ubufs[slot]
        xn = xn_ref[0]
        gate = jnp.dot(xn, wup_ref[:, cols[c]], preferred_element_type=F32)
        gbuf[0:8, :] = halo[c]
        gbuf[8:8 + tm, :] = gate
        halo[c] = gate[tm - 8:, :]
        ubuf[...] = jnp.dot(xn, wup_ref[:, up_cols[c]], preferred_element_type=F32)

    def activate(c, slot):
        gbuf, ubuf = gbufs[slot], ubufs[slot]
        cw = cw_ref[:, cols[c]]
        conv = cb_ref[:, cols[c]] + cw[2:3, :] * gbuf[8:8 + tm, :]
        conv = conv + cw[1:2, :] * gbuf[7:7 + tm, :] + cw[0:1, :] * gbuf[6:6 + tm, :]
        act_scr[c] = (_gelu_tanh(conv) * ubuf[...]).astype(BF16)

    project(0, 0)
    for c in range(nch - 1):
        project(c + 1, (c + 1) % 2)
        activate(c, c % 2)
    activate(nch - 1, (nch - 1) % 2)
    act = jnp.concatenate([act_scr[c] for c in range(nch)], axis=1)
    down = jnp.dot(act, wd_ref[...], preferred_element_type=F32)
    out_ref[0] = h1_ref[0] + _rms(down, g_ref[...])


def _ffn(xn2, xn2_meta, h1, w_up, conv_w, conv_b, w_down, g_post):
    bsz, seq, _ = xn2.shape
    tm, tf = FFN_TM, FFN_TF
    nch = D_FF // tf
    xblk = lambda b, i: (b, i, 0)
    meta = lambda b, i: (0, BLK // N_META - 1, 0)
    return pl.pallas_call(
        functools.partial(_ffn_kernel, tm=tm, tf=tf),
        grid=(bsz, seq // tm),
        in_specs=[
            pl.BlockSpec((1, tm, D_MODEL), xblk),
            pl.BlockSpec((1, N_META, D_MODEL), meta),
            pl.BlockSpec((1, tm, D_MODEL), xblk),
            _const_spec((D_MODEL, 2 * D_FF)),
            _const_spec((FFN_CONV, D_FF)),
            _const_spec((1, D_FF)),
            _const_spec((D_FF, D_MODEL)),
            _const_spec((1, D_MODEL)),
        ],
        out_specs=pl.BlockSpec((1, tm, D_MODEL), xblk),
        out_shape=jax.ShapeDtypeStruct((bsz, seq, D_MODEL), F32),
        scratch_shapes=[
            pltpu.VMEM((tm + 8, tf), F32),
            pltpu.VMEM((tm + 8, tf), F32),
            pltpu.VMEM((tm, tf), F32),
            pltpu.VMEM((tm, tf), F32),
            pltpu.VMEM((nch, 8, tf), F32),
            pltpu.VMEM((nch, tm, tf), BF16),
        ],
        compiler_params=pltpu.CompilerParams(
            dimension_semantics=("arbitrary", "arbitrary"), vmem_limit_bytes=V7X_VMEM_LIMIT),
        name="ffn",
    )(xn2, xn2_meta, h1, w_up, conv_w, conv_b, w_down, g_post)


def _pad_lanes(v, n):
    return jnp.pad(v, ((0, 0), (0, n - v.shape[-1])))


def kernel(x, meta_tokens, mix_pre_g, w_in, ssd_conv_w, ssd_conv_b, ssd_dt_bias, ssd_a_log, ssd_d,
           ssd_norm_g, sb_norm_g, w_out, mix_post_g, ffn_pre_g, w_up, ffn_conv_w, ffn_conv_b,
           w_down, ffn_post_g):
    bsz, seq, d = x.shape
    depth = w_in.shape[0]
    assert depth == 1 and d == D_MODEL and seq % OUTPROJ_TM == 0

    hm = jnp.concatenate([jnp.zeros((PAD, d), x.dtype), meta_tokens.astype(x.dtype)], axis=0)

    l = 0
    off_xbc = SSD_INNER
    off_dt = off_xbc + XBC_WIDTH
    off_q = off_dt + SSD_HEADS
    w = w_in[l]
    w_main = jnp.concatenate([w[:, :off_dt], w[:, off_q:]], axis=1).astype(BF16)
    w_dt = _pad_lanes(w[:, off_dt:off_q], DT_LANES).astype(BF16)
    z, xbc, q, k, v0, v1, dt_raw = _inproj(x, hm, mix_pre_g[l][None], w_main, w_dt)

    head_of_col = jnp.arange(SSD_INNER) // SSD_HEAD_DIM
    expand = (jnp.arange(DT_LANES)[:, None] == head_of_col[None, :]).astype(BF16)
    d_exp = jnp.repeat(ssd_d[l].astype(F32), SSD_HEAD_DIM)[None]
    y_ssd = _ssd(xbc, z, dt_raw, ssd_conv_w[l], ssd_conv_b[l][None],
                 _pad_lanes(ssd_dt_bias[l][None], DT_LANES), _pad_lanes(ssd_a_log[l][None], DT_LANES),
                 d_exp, ssd_norm_g[l][None], expand)

    kk = jnp.arange(BLK)
    tt = (kk[:, None] > kk[None, :]).astype(BF16)
    o_sb = _sb_attention(q, k, v0, v1, tt)

    wo = w_out[l].astype(BF16)
    h1, xn2, xn2_meta = _outproj(y_ssd, o_sb, x, hm, wo[:SSD_INNER], wo[SSD_INNER:], sb_norm_g[l][None],
                                 mix_post_g[l][None], ffn_pre_g[l][None])

    return _ffn(xn2, xn2_meta, h1, w_up[l].astype(BF16), ffn_conv_w[l], ffn_conv_b[l][None],
                w_down[l].astype(BF16), ffn_post_g[l][None])
```

```python
import functools
import math

import jax
import jax.numpy as jnp
from jax import lax
from jax.experimental import pallas as pl
from jax.experimental.pallas import tpu as pltpu

F32 = jnp.float32
BF16 = jnp.bfloat16

D_MODEL = 1024
N_META = 16
BLK = 128
PAD = BLK - N_META
SSD_HEADS = 16
SSD_HEAD_DIM = 64
SSD_GROUPS = 2
SSD_STATE = 128
SSD_INNER = 1024
SSD_CONV = 4
XBC_WIDTH = SSD_INNER + 2 * SSD_GROUPS * SSD_STATE
SB_WIDTH = 1024
SB_HEAD_DIM = 64
D_FF = 2816
FFN_CONV = 3
EPS = 1e-6
DT_LANES = 128
GROUP_COLS = SSD_INNER // SSD_GROUPS
LOG2E = 1.4426950408889634

V7X_VMEM_LIMIT = 56 * 1024 * 1024

INPROJ_TM = 512
OUTPROJ_TM = 1024
OUTPROJ_SUBTILES = 4
FFN_TM = 512
FFN_TF = 256
SSD_SEQS_PER_STEP = 4
SB_Q_BLOCKS_PER_STEP = 3


def _rms(x, g):
    return x * lax.rsqrt(jnp.mean(x * x, axis=-1, keepdims=True) + EPS) * g


def _const_spec(shape):
    nd = len(shape)
    return pl.BlockSpec(shape, lambda *_: (0,) * nd, pipeline_mode=pl.Buffered(1))


_PROJ_SPLITS = (("z", SSD_INNER), ("xbc", XBC_WIDTH), ("q", SB_WIDTH), ("k", SB_WIDTH), ("v", SB_WIDTH))
_INPROJ_OUT_WIDTHS = (SSD_INNER, XBC_WIDTH, SB_WIDTH, SB_WIDTH, SB_WIDTH, SB_WIDTH, DT_LANES)
_INPROJ_OUT_DTYPES = (BF16, BF16, BF16, BF16, BF16, BF16, F32)


def _inproj_rows(h, g_ref, w_ref, wdt_ref, out_refs):
    z_ref, xbc_ref, q_ref, k_ref, v0_ref, v1_ref, dt_ref = out_refs
    xn = _rms(h, g_ref[...]).astype(BF16)
    lo = 0
    proj = {}
    for name, width in _PROJ_SPLITS:
        proj[name] = jnp.dot(xn, w_ref[:, lo:lo + width], preferred_element_type=F32)
        lo += width
    z_ref[0] = proj["z"].astype(BF16)
    xbc_ref[0] = proj["xbc"].astype(BF16)
    q_ref[0] = (proj["q"] * (LOG2E / math.sqrt(SB_HEAD_DIM))).astype(BF16)
    k_ref[0] = proj["k"].astype(BF16)
    v = proj["v"].astype(BF16)
    lane = lax.broadcasted_iota(jnp.int32, v.shape, 1)
    even_head = (lane & SB_HEAD_DIM) == 0
    zero = jnp.zeros_like(v)
    v0_ref[0] = jnp.where(even_head, v, zero)
    v1_ref[0] = jnp.where(even_head, zero, v)
    dt_ref[0] = jnp.dot(xn, wdt_ref[...], preferred_element_type=F32)


def _inproj_kernel(x_ref, g_ref, w_ref, wdt_ref, *out_refs):
    _inproj_rows(x_ref[...], g_ref, w_ref, wdt_ref, out_refs)


def _inproj_meta_kernel(hm_ref, g_ref, w_ref, wdt_ref, *refs):
    nout = len(_INPROJ_OUT_WIDTHS)
    out_refs, scratch = refs[nout:2 * nout], refs[2 * nout:]

    @pl.when(pl.program_id(0) == 0)
    def _():
        _inproj_rows(hm_ref[...], g_ref, w_ref, wdt_ref, scratch)

    for out_ref, scr in zip(out_refs, scratch):
        out_ref[...] = scr[...]


def _inproj(x, hm, g, w_main, w_dt):
    bsz, seq, _ = x.shape
    lp = seq + BLK
    tm = INPROJ_TM
    per_seq = seq // tm
    n_main = w_main.shape[1]
    out_shape = [jax.ShapeDtypeStruct((bsz, lp, width), dt)
                 for width, dt in zip(_INPROJ_OUT_WIDTHS, _INPROJ_OUT_DTYPES)]
    weights = [_const_spec((1, D_MODEL)), _const_spec((D_MODEL, n_main)), _const_spec((D_MODEL, DT_LANES))]
    params = pltpu.CompilerParams(dimension_semantics=("arbitrary",), vmem_limit_bytes=V7X_VMEM_LIMIT)

    tile = lambda t: (t // per_seq, t % per_seq, 0)
    outs = pl.pallas_call(
        _inproj_kernel,
        grid=(bsz * per_seq,),
        in_specs=[pl.BlockSpec((tm, D_MODEL), lambda t: (t, 0))] + weights,
        out_specs=[pl.BlockSpec((1, tm, width), tile) for width in _INPROJ_OUT_WIDTHS],
        out_shape=out_shape,
        compiler_params=params,
        name="inproj",
    )(x.reshape(bsz * seq, D_MODEL), g, w_main, w_dt)

    nout = len(outs)
    last_blk = lambda b: (b, lp // BLK - 1, 0)
    return pl.pallas_call(
        _inproj_meta_kernel,
        grid=(bsz,),
        in_specs=[_const_spec((BLK, D_MODEL))] + weights + [pl.BlockSpec(memory_space=pl.ANY)] * nout,
        out_specs=[pl.BlockSpec((1, BLK, width), last_blk) for width in _INPROJ_OUT_WIDTHS],
        out_shape=out_shape,
        input_output_aliases={4 + j: j for j in range(nout)},
        scratch_shapes=[pltpu.VMEM((1, BLK, width), dt)
                        for width, dt in zip(_INPROJ_OUT_WIDTHS, _INPROJ_OUT_DTYPES)],
        compiler_params=params,
        name="inproj_meta",
    )(hm, g, w_main, w_dt, *outs)


def _dot_f32_by_01(lhs, rhs, f32_side):
    x = lhs if f32_side == "lhs" else rhs
    p1 = x.astype(BF16)
    r1 = x - p1.astype(F32)
    p2 = r1.astype(BF16)
    p3 = (r1 - p2.astype(F32)).astype(BF16)
    out = None
    for piece in (p1, p2, p3):
        ops = (piece, rhs) if f32_side == "lhs" else (lhs, piece)
        term = jnp.dot(*ops, preferred_element_type=F32)
        out = term if out is None else out + term
    return out


def _ssd_kernel(xbc_ref, z_ref, dt_ref, cw_ref, cb_ref, dtb_ref, alog_ref, dexp_ref, g_ref, e_ref,
                out_ref, prev, state, *, nseq):
    c = pl.program_id(1)

    @pl.when(c == 0)
    def _():
        prev[...] = jnp.zeros_like(prev)
        state[...] = jnp.zeros_like(state)

    seqs = range(nseq)
    groups = range(SSD_GROUPS)
    row = lax.broadcasted_iota(jnp.int32, (BLK, 1), 0)
    real = jnp.logical_or(c > 0, row >= PAD)
    lane = lax.broadcasted_iota(jnp.int32, (BLK, DT_LANES), 1)
    dt_valid = jnp.logical_and(real, lane < SSD_HEADS)
    ri = lax.broadcasted_iota(jnp.int32, (BLK, BLK), 0)
    ci = lax.broadcasted_iota(jnp.int32, (BLK, BLK), 1)
    causal = ri >= ci
    tri = jnp.where(causal, 1.0, 0.0).astype(BF16)
    half = ci < SSD_HEAD_DIM
    cw = cw_ref[...]
    e = e_ref[...]
    a_neg = -jnp.exp(alog_ref[...])

    xc, dt, adt = [], [], []
    for s in seqs:
        x_raw = xbc_ref[s].astype(F32)
        prev[s, 8:8 + BLK, :] = x_raw
        acc = cb_ref[...] + cw[3:4, :] * x_raw
        for tap in range(SSD_CONV - 1):
            acc = acc + cw[tap:tap + 1, :] * prev[s, 5 + tap:5 + tap + BLK, :]
        prev[s, 0:8, :] = x_raw[BLK - 8:, :]
        xc.append(jnp.where(real, acc * jax.nn.sigmoid(acc), 0.0))
        dts = jax.nn.softplus(dt_ref[s] + dtb_ref[...])
        dt.append(jnp.where(dt_valid, dts, 0.0))
        adt.append(dt[s] * a_neg)
    xs = [xc[s][:, :SSD_INNER] for s in seqs]

    acs = [_dot_f32_by_01(tri, adt[s], f32_side="rhs") for s in seqs]
    acs_t = [acs[s].T for s in seqs]
    dt_exp = [_dot_f32_by_01(dt[s], e, f32_side="lhs") for s in seqs]
    acs_exp = [_dot_f32_by_01(acs[s], e, f32_side="lhs") for s in seqs]

    xdt_b, xdec, off_scale, chunk_decay, b_t, c_gb, b_gb = [], [], [], [], [], [], []
    for s in seqs:
        last = acs_exp[s][BLK - 1:BLK, :]
        xdt = xs[s] * dt_exp[s]
        xdec.append((xdt * jnp.exp(last - acs_exp[s])).astype(BF16))
        xdt_b.append(xdt.astype(BF16))
        off_scale.append(jnp.exp(acs_exp[s]))
        chunk_decay.append(jnp.exp(last))
        b_g = [xc[s][:, SSD_INNER + g * SSD_STATE:SSD_INNER + (g + 1) * SSD_STATE] for g in groups]
        c_g = [xc[s][:, SSD_INNER + (SSD_GROUPS + g) * SSD_STATE:
                     SSD_INNER + (SSD_GROUPS + g + 1) * SSD_STATE] for g in groups]
        b_gb.append([x.astype(BF16) for x in b_g])
        c_gb.append([x.astype(BF16) for x in c_g])
        b_t.append([x.T.astype(BF16) for x in b_g])

    gcols = [slice(g * GROUP_COLS, (g + 1) * GROUP_COLS) for g in groups]
    cb_mat = [[lax.dot_general(c_gb[s][g], b_gb[s][g], (((1,), (1,)), ((), ())),
                               preferred_element_type=F32) for g in groups] for s in seqs]
    s_prev = [[state[s, :, gcols[g]] for g in groups] for s in seqs]
    y_off = [[jnp.dot(c_gb[s][g], s_prev[s][g].astype(BF16), preferred_element_type=F32)
              for g in groups] for s in seqs]

    heads_per_group = SSD_HEADS // SSD_GROUPS
    decay_mats = [[None] * SSD_HEADS for _ in seqs]
    for s in seqs:
        for hh in range(SSD_HEADS):
            seg = acs[s][:, hh:hh + 1] - acs_t[s][hh:hh + 1, :]
            ldec = jnp.exp(jnp.where(causal, seg, -1e30))
            decay_mats[s][hh] = (cb_mat[s][hh // heads_per_group] * ldec).astype(BF16)

    y_diag = [[jnp.dot(decay_mats[s][hh], xdt_b[s][:, (hh // 2) * BLK:(hh // 2 + 1) * BLK],
                       preferred_element_type=F32) for hh in range(SSD_HEADS)] for s in seqs]
    new = [[jnp.dot(b_t[s][g], xdec[s][:, gcols[g]], preferred_element_type=F32)
            for g in groups] for s in seqs]

    for s in seqs:
        y_cols = []
        for g in groups:
            state[s, :, gcols[g]] = s_prev[s][g] * chunk_decay[s][:, gcols[g]] + new[s][g]
            yo = y_off[s][g] * off_scale[s][:, gcols[g]]
            for pair in range(GROUP_COLS // BLK):
                h0 = (g * GROUP_COLS + pair * BLK) // SSD_HEAD_DIM
                y_cols.append(jnp.where(half, y_diag[s][h0], y_diag[s][h0 + 1])
                              + yo[:, pair * BLK:(pair + 1) * BLK])
        y = jnp.concatenate(y_cols, axis=1) + xs[s] * dexp_ref[...]
        zf = z_ref[s].astype(F32)
        y = y * (zf * jax.nn.sigmoid(zf))
        out_ref[s] = _rms(y, g_ref[...]).astype(BF16)


def _ssd(xbc, z, dt_raw, conv_w, conv_b, dt_bias, a_log, d_exp, norm_g, expand):
    bsz, lp, _ = xbc.shape
    nblk = lp // BLK
    nseq = SSD_SEQS_PER_STEP
    assert bsz % nseq == 0
    blk = lambda b, c: (b, (c + nblk - 1) % nblk, 0)
    return pl.pallas_call(
        functools.partial(_ssd_kernel, nseq=nseq),
        grid=(bsz // nseq, nblk),
        in_specs=[
            pl.BlockSpec((nseq, BLK, XBC_WIDTH), blk),
            pl.BlockSpec((nseq, BLK, SSD_INNER), blk),
            pl.BlockSpec((nseq, BLK, DT_LANES), blk),
            _const_spec((SSD_CONV, XBC_WIDTH)),
            _const_spec((1, XBC_WIDTH)),
            _const_spec((1, DT_LANES)),
            _const_spec((1, DT_LANES)),
            _const_spec((1, SSD_INNER)),
            _const_spec((1, SSD_INNER)),
            _const_spec((DT_LANES, SSD_INNER)),
        ],
        out_specs=pl.BlockSpec((nseq, BLK, SSD_INNER), blk),
        out_shape=jax.ShapeDtypeStruct((bsz, lp, SSD_INNER), BF16),
        scratch_shapes=[
            pltpu.VMEM((nseq, BLK + 8, XBC_WIDTH), F32),
            pltpu.VMEM((nseq, SSD_STATE, SSD_INNER), F32),
        ],
        compiler_params=pltpu.CompilerParams(
            dimension_semantics=("arbitrary", "arbitrary"), vmem_limit_bytes=V7X_VMEM_LIMIT),
        name="ssd",
    )(xbc, z, dt_raw, conv_w, conv_b, dt_bias, a_log, d_exp, norm_g, expand)


SB_DEAD_LOG2 = -151.0
SB_LEAD = 3
SB_TOP_ROWS = 48
FULL, TOP, BOT = slice(0, BLK), slice(0, SB_TOP_ROWS), slice(SB_TOP_ROWS, BLK)


def _sb_kv_copies(hbm_refs, bufs, sems, batch, part, nblk, qpb):
    nparts = nblk // qpb
    seq_rows = (nblk - 1) * BLK
    if part == nparts:
        rows = pl.ds(seq_rows, BLK)
    else:
        start = part * qpb * BLK
        rows = pl.ds(start, min(qpb * BLK, seq_rows - start))
    return [pltpu.make_async_copy(src.at[batch, rows, :], dst.at[rows, :], sems.at[a, part])
            for a, (src, dst) in enumerate(zip(hbm_refs, bufs))]


def _sb_kernel(q_ref, k_hbm, v0_hbm, v1_hbm, tt_ref, o_ref, k_ref, v0_ref, v1_ref, kv_sems,
               qm_scr, run_scr, acc_scr, worst_scr, *, nblk, npairs, qpb):
    batch, step = pl.program_id(0), pl.program_id(1)
    nparts = nblk // qpb
    hbm_refs, bufs = (k_hbm, v0_hbm, v1_hbm), (k_ref, v0_ref, v1_ref)

    @pl.when(step == 0)
    def _():
        for part in [nparts] + list(range(nparts)):
            for cp in _sb_kv_copies(hbm_refs, bufs, kv_sems, batch, part, nblk, qpb):
                cp.start()
        for cp in _sb_kv_copies(hbm_refs, bufs, kv_sems, batch, nparts, nblk, qpb):
            cp.wait()

    for part in range(nparts):
        @pl.when(step == part)
        def _(part=part):
            for cp in _sb_kv_copies(hbm_refs, bufs, kv_sems, batch, part, nblk, qpb):
                cp.wait()

    lane = lax.broadcasted_iota(jnp.int32, (BLK, BLK), 1)
    sub = lax.broadcasted_iota(jnp.int32, (BLK, BLK), 0)
    first = lane < SB_HEAD_DIM
    strictly_older = lane < sub
    tt = tt_ref[...]

    def query_block(r, carry):
        qi = (step * qpb + r + 1) % nblk
        rows = pl.ds(pl.multiple_of(r * BLK, BLK), BLK)
        for p in range(npairs):
            qp = q_ref[0, rows, p * BLK:(p + 1) * BLK]
            zero = jnp.zeros_like(qp)
            qm_scr[p, :BLK, :] = jnp.where(first, qp, zero)
            qm_scr[p, BLK:, :] = jnp.where(first, zero, qp)

        def process(blocks, init, check, row_sets=None):
            offs = [pl.multiple_of(((j + nblk - 1) % nblk) * BLK, BLK) for j in blocks]
            nb = range(len(blocks))
            row_sets = row_sets or [FULL] * len(blocks)
            heads = range(2 * npairs)
            scores = {}
            for b in nb:
                rs = row_sets[b]
                for p in range(npairs):
                    if rs is FULL:
                        lhs = qm_scr[p]
                    else:
                        lhs = jnp.concatenate([qm_scr[p, rs, :],
                                               qm_scr[p, BLK + rs.start:BLK + rs.stop, :]], axis=0)
                    scores[b, p] = lax.dot_general(
                        lhs, k_ref[pl.ds(offs[b], BLK), p * BLK:(p + 1) * BLK],
                        (((1,), (1,)), ((), ())), preferred_element_type=F32)
            log_betas, log_keeps, totals = {}, {}, {}
            for b in nb:
                n = row_sets[b].stop - row_sets[b].start
                for h in heads:
                    s = scores[b, h // 2][(h % 2) * n:(h % 2 + 1) * n, :]
                    neg_part = jnp.minimum(s, 0.0)
                    neg_pos = neg_part - s
                    sp = jnp.log(1.0 + jnp.exp2(neg_part + neg_pos)) * LOG2E
                    log_beta = neg_part - sp
                    log_keep = neg_pos - sp
                    if init and b == 0:
                        log_keep = jnp.where(strictly_older, log_keep, 0.0)
                    log_betas[b, h] = log_beta
                    log_keeps[b, h] = log_keep.astype(BF16)
                    totals[b, h] = jnp.sum(log_keep, axis=1, keepdims=True)
            sums = [jnp.dot(jnp.concatenate([log_keeps[b, h] for h in heads], axis=0), tt,
                            preferred_element_type=F32) for b in nb]
            weights = {}
            worst = None
            for h in heads:
                run = None if init else run_scr[h]
                for b in nb:
                    rs = row_sets[b]
                    n = rs.stop - rs.start
                    after, total = sums[b][h * n:(h + 1) * n, :], totals[b, h]
                    if run is None:
                        w = jnp.where(strictly_older, jnp.exp2(log_betas[b, h] + after), 0.0)
                        run = jnp.broadcast_to(total, (BLK, BLK))
                    else:
                        part = run if rs is FULL else run[rs, :]
                        w = jnp.exp2(log_betas[b, h] + (after + part))
                        part = part + total
                        if rs is FULL:
                            run = part
                        elif rs is TOP:
                            run = jnp.concatenate([part, run[BOT, :]], axis=0)
                        else:
                            run = jnp.concatenate([run[TOP, :], part], axis=0)
                    weights[b, h] = w.astype(BF16)
                run_scr[h] = run
                if check:
                    worst = run if worst is None else jnp.maximum(worst, run)
            for p in range(npairs):
                pv = None if init else acc_scr[p]
                for b in nb:
                    rs = row_sets[b]
                    v_heads = jnp.concatenate(
                        [v0_ref[pl.ds(offs[b], BLK), p * BLK:(p + 1) * BLK],
                         v1_ref[pl.ds(offs[b], BLK), p * BLK:(p + 1) * BLK]], axis=0)
                    w_heads = jnp.concatenate([weights[b, 2 * p], weights[b, 2 * p + 1]], axis=1)
                    contrib = jnp.dot(w_heads, v_heads, preferred_element_type=F32)
                    if pv is None:
                        pv = contrib
                    elif rs is FULL:
                        pv = pv + contrib
                    elif rs is TOP:
                        pv = jnp.concatenate([pv[TOP, :] + contrib, pv[BOT, :]], axis=0)
                    else:
                        pv = jnp.concatenate([pv[TOP, :], pv[BOT, :] + contrib], axis=0)
                acc_scr[p] = pv
            if check:
                worst_scr[0] = jnp.max(worst[TOP, :])
                worst_scr[1] = jnp.max(worst[BOT, :])

        for lead in range(1, SB_LEAD + 1):
            last = lead == SB_LEAD

            @pl.when(qi >= lead - 1 if last else qi == lead - 1)
            def _(lead=lead, last=last):
                if last:
                    process([qi - i for i in range(lead)], True, True, [FULL] * (lead - 1) + [TOP])
                else:
                    process([qi - i for i in range(lead)], True, False)
                    worst_scr[0] = jnp.float32(SB_DEAD_LOG2)
                    worst_scr[1] = jnp.float32(SB_DEAD_LOG2)

        @pl.when(jnp.logical_and(qi >= SB_LEAD - 1, worst_scr[1] > SB_DEAD_LOG2))
        def _():
            process([qi - (SB_LEAD - 1)], False, True, [BOT])

        def cond(st):
            return jnp.logical_and(st[0] >= 0, st[1] > SB_DEAD_LOG2)

        def body(st):
            process([st[0]], False, True)
            return st[0] - 1, jnp.maximum(worst_scr[0], worst_scr[1])

        lax.while_loop(cond, body, (qi - SB_LEAD, jnp.maximum(worst_scr[0], worst_scr[1])))
        for p in range(npairs):
            o_ref[0, rows, p * BLK:(p + 1) * BLK] = acc_scr[p].astype(BF16)
        return carry

    lax.fori_loop(0, qpb, query_block, 0)


def _sb_attention(q, k, v0, v1, tt):
    bsz, lp, width = q.shape
    nblk = lp // BLK
    qpb = SB_Q_BLOCKS_PER_STEP
    assert nblk % qpb == 0
    npairs = width // BLK
    nparts = nblk // qpb
    qblk = lambda b, g: (b, g, 0)
    kv_hbm = pl.BlockSpec(memory_space=pl.ANY)
    return pl.pallas_call(
        functools.partial(_sb_kernel, nblk=nblk, npairs=npairs, qpb=qpb),
        grid=(bsz, nparts),
        in_specs=[
            pl.BlockSpec((1, qpb * BLK, width), qblk),
            kv_hbm,
            kv_hbm,
            kv_hbm,
            _const_spec((BLK, BLK)),
        ],
        out_specs=pl.BlockSpec((1, qpb * BLK, width), qblk),
        out_shape=jax.ShapeDtypeStruct((bsz, lp, width), BF16),
        scratch_shapes=[
            pltpu.VMEM((lp, width), BF16),
            pltpu.VMEM((lp, width), BF16),
            pltpu.VMEM((lp, width), BF16),
            pltpu.SemaphoreType.DMA((3, nparts + 1)),
            pltpu.VMEM((npairs, 2 * BLK, BLK), BF16),
            pltpu.VMEM((2 * npairs, BLK, BLK), F32),
            pltpu.VMEM((npairs, BLK, BLK), F32),
            pltpu.SMEM((2,), F32),
        ],
        compiler_params=pltpu.CompilerParams(
            dimension_semantics=("arbitrary", "arbitrary"), vmem_limit_bytes=V7X_VMEM_LIMIT),
        name="sb_attention",
    )(q, k, v0, v1, tt)


def _outproj_rows(ys, o, h, wa_ref, wb_ref, gsb_ref, gpost_ref, gffn_ref):
    osb = _rms(o.astype(F32), gsb_ref[...]).astype(BF16)
    mix = jnp.dot(ys, wa_ref[...], preferred_element_type=F32)
    mix = mix + jnp.dot(osb, wb_ref[...], preferred_element_type=F32)
    h1 = h + _rms(mix, gpost_ref[...])
    return h1, _rms(h1, gffn_ref[...]).astype(BF16)


def _outproj_kernel(ys_ref, o_ref, h_ref, wa_ref, wb_ref, gsb_ref, gpost_ref, gffn_ref,
                    h1_ref, xn2_ref, *, nsub):
    sub = h_ref.shape[1] // nsub
    rows = [pl.ds(k * sub, sub) for k in range(nsub)]
    osb = [_rms(o_ref[0, r, :].astype(F32), gsb_ref[...]).astype(BF16) for r in rows]
    mix = [jnp.dot(ys_ref[0, r, :], wa_ref[...], preferred_element_type=F32)
           + jnp.dot(osb[k], wb_ref[...], preferred_element_type=F32) for k, r in enumerate(rows)]
    for k, r in enumerate(rows):
        h1 = h_ref[0, r, :] + _rms(mix[k], gpost_ref[...])
        h1_ref[0, r, :] = h1
        xn2_ref[0, r, :] = _rms(h1, gffn_ref[...]).astype(BF16)


def _outproj_meta_kernel(ys_ref, o_ref, hm_ref, wa_ref, wb_ref, gsb_ref, gpost_ref, gffn_ref, xn2_ref):
    _, xn2 = _outproj_rows(ys_ref[0], o_ref[0], hm_ref[...], wa_ref, wb_ref, gsb_ref, gpost_ref, gffn_ref)
    xn2_ref[0] = xn2


def _outproj(y_ssd, o_sb, x, hm, w_a, w_b, g_sb, g_post, g_ffn):
    bsz, seq, _ = x.shape
    lp = y_ssd.shape[1]
    tm = OUTPROJ_TM
    per_seq = seq // tm
    weights = [
        _const_spec((SSD_INNER, D_MODEL)),
        _const_spec((SB_WIDTH, D_MODEL)),
        _const_spec((1, SB_WIDTH)),
        _const_spec((1, D_MODEL)),
        _const_spec((1, D_MODEL)),
    ]
    params = pltpu.CompilerParams(dimension_semantics=("arbitrary",), vmem_limit_bytes=V7X_VMEM_LIMIT)
    tile = lambda t: (t // per_seq, t % per_seq, 0)
    h1, xn2 = pl.pallas_call(
        functools.partial(_outproj_kernel, nsub=OUTPROJ_SUBTILES),
        grid=(bsz * per_seq,),
        in_specs=[
            pl.BlockSpec((1, tm, SSD_INNER), tile),
            pl.BlockSpec((1, tm, SB_WIDTH), tile),
            pl.BlockSpec((1, tm, D_MODEL), tile),
        ] + weights,
        out_specs=[pl.BlockSpec((1, tm, D_MODEL), tile), pl.BlockSpec((1, tm, D_MODEL), tile)],
        out_shape=[jax.ShapeDtypeStruct((bsz, seq, D_MODEL), F32),
                   jax.ShapeDtypeStruct((bsz, seq, D_MODEL), BF16)],
        compiler_params=params,
        name="outproj",
    )(y_ssd, o_sb, x, w_a, w_b, g_sb, g_post, g_ffn)

    last_blk = lambda b: (0, lp // BLK - 1, 0)
    xn2_meta = pl.pallas_call(
        _outproj_meta_kernel,
        grid=(1,),
        in_specs=[
            pl.BlockSpec((1, BLK, SSD_INNER), last_blk),
            pl.BlockSpec((1, BLK, SB_WIDTH), last_blk),
            _const_spec((BLK, D_MODEL)),
        ] + weights,
        out_specs=pl.BlockSpec((1, BLK, D_MODEL), lambda b: (0, 0, 0)),
        out_shape=jax.ShapeDtypeStruct((1, BLK, D_MODEL), BF16),
        compiler_params=params,
        name="outproj_meta",
    )(y_ssd, o_sb, hm, w_a, w_b, g_sb, g_post, g_ffn)
    return h1, xn2, xn2_meta


def _gelu_tanh(x):
    return 0.5 * x * (1.0 + jnp.tanh(math.sqrt(2.0 / math.pi) * (x + 0.044715 * (x * x * x))))


def _ffn_kernel(xn_ref, xm_ref, h1_ref, wup_ref, cw_ref, cb_ref, wd_ref, g_ref, out_ref,
                gbuf0, gbuf1, ubuf0, ubuf1, halo, act_scr, *, tm, tf):
    nch = D_FF // tf
    i = pl.program_id(1)
    gbufs, ubufs = (gbuf0, gbuf1), (ubuf0, ubuf1)
    cols = [slice(c * tf, (c + 1) * tf) for c in range(nch)]
    up_cols = [slice(D_FF + c * tf, D_FF + (c + 1) * tf) for c in range(nch)]

    @pl.when(i == 0)
    def _():
        for c in range(nch):
            gm = jnp.dot(xm_ref[0], wup_ref[:, cols[c]], preferred_element_type=F32)
            halo[c] = gm[N_META - 8:, :]

    def project(c, slot):
        gbuf, ubuf = gbufs[slot], ubufs[slot]
        xn = xn_ref[0]
        gate = jnp.dot(xn, wup_ref[:, cols[c]], preferred_element_type=F32)
        gbuf[0:8, :] = halo[c]
        gbuf[8:8 + tm, :] = gate
        halo[c] = gate[tm - 8:, :]
        ubuf[...] = jnp.dot(xn, wup_ref[:, up_cols[c]], preferred_element_type=F32)

    def activate(c, slot):
        gbuf, ubuf = gbufs[slot], ubufs[slot]
        cw = cw_ref[:, cols[c]]
        conv = cb_ref[:, cols[c]] + cw[2:3, :] * gbuf[8:8 + tm, :]
        conv = conv + cw[1:2, :] * gbuf[7:7 + tm, :] + cw[0:1, :] * gbuf[6:6 + tm, :]
        act_scr[c] = (_gelu_tanh(conv) * ubuf[...]).astype(BF16)

    project(0, 0)
    for c in range(nch - 1):
        project(c + 1, (c + 1) % 2)
        activate(c, c % 2)
    activate(nch - 1, (nch - 1) % 2)
    act = jnp.concatenate([act_scr[c] for c in range(nch)], axis=1)
    down = jnp.dot(act, wd_ref[...], preferred_element_type=F32)
    out_ref[0] = h1_ref[0] + _rms(down, g_ref[...])


def _ffn(xn2, xn2_meta, h1, w_up, conv_w, conv_b, w_down, g_post):
    bsz, seq, _ = xn2.shape
    tm, tf = FFN_TM, FFN_TF
    nch = D_FF // tf
    xblk = lambda b, i: (b, i, 0)
    meta = lambda b, i: (0, BLK // N_META - 1, 0)
    return pl.pallas_call(
        functools.partial(_ffn_kernel, tm=tm, tf=tf),
        grid=(bsz, seq // tm),
        in_specs=[
            pl.BlockSpec((1, tm, D_MODEL), xblk),
            pl.BlockSpec((1, N_META, D_MODEL), meta),
            pl.BlockSpec((1, tm, D_MODEL), xblk),
            _const_spec((D_MODEL, 2 * D_FF)),
            _const_spec((FFN_CONV, D_FF)),
            _const_spec((1, D_FF)),
            _const_spec((D_FF, D_MODEL)),
            _const_spec((1, D_MODEL)),
        ],
        out_specs=pl.BlockSpec((1, tm, D_MODEL), xblk),
        out_shape=jax.ShapeDtypeStruct((bsz, seq, D_MODEL), F32),
        scratch_shapes=[
            pltpu.VMEM((tm + 8, tf), F32),
            pltpu.VMEM((tm + 8, tf), F32),
            pltpu.VMEM((tm, tf), F32),
            pltpu.VMEM((tm, tf), F32),
            pltpu.VMEM((nch, 8, tf), F32),
            pltpu.VMEM((nch, tm, tf), BF16),
        ],
        compiler_params=pltpu.CompilerParams(
            dimension_semantics=("arbitrary", "arbitrary"), vmem_limit_bytes=V7X_VMEM_LIMIT),
        name="ffn",
    )(xn2, xn2_meta, h1, w_up, conv_w, conv_b, w_down, g_post)


def _pad_lanes(v, n):
    return jnp.pad(v, ((0, 0), (0, n - v.shape[-1])))


def kernel(x, meta_tokens, mix_pre_g, w_in, ssd_conv_w, ssd_conv_b, ssd_dt_bias, ssd_a_log, ssd_d,
           ssd_norm_g, sb_norm_g, w_out, mix_post_g, ffn_pre_g, w_up, ffn_conv_w, ffn_conv_b,
           w_down, ffn_post_g):
    bsz, seq, d = x.shape
    depth = w_in.shape[0]
    assert depth == 1 and d == D_MODEL and seq % OUTPROJ_TM == 0

    hm = jnp.concatenate([jnp.zeros((PAD, d), x.dtype), meta_tokens.astype(x.dtype)], axis=0)

    l = 0
    off_xbc = SSD_INNER
    off_dt = off_xbc + XBC_WIDTH
    off_q = off_dt + SSD_HEADS
    w = w_in[l]
    w_main = jnp.concatenate([w[:, :off_dt], w[:, off_q:]], axis=1).astype(BF16)
    w_dt = _pad_lanes(w[:, off_dt:off_q], DT_LANES).astype(BF16)
    z, xbc, q, k, v0, v1, dt_raw = _inproj(x, hm, mix_pre_g[l][None], w_main, w_dt)

    head_of_col = jnp.arange(SSD_INNER) // SSD_HEAD_DIM
    expand = (jnp.arange(DT_LANES)[:, None] == head_of_col[None, :]).astype(BF16)
    d_exp = jnp.repeat(ssd_d[l].astype(F32), SSD_HEAD_DIM)[None]
    y_ssd = _ssd(xbc, z, dt_raw, ssd_conv_w[l], ssd_conv_b[l][None],
                 _pad_lanes(ssd_dt_bias[l][None], DT_LANES), _pad_lanes(ssd_a_log[l][None], DT_LANES),
                 d_exp, ssd_norm_g[l][None], expand)

    kk = jnp.arange(BLK)
    tt = (kk[:, None] > kk[None, :]).astype(BF16)
    o_sb = _sb_attention(q, k, v0, v1, tt)

    wo = w_out[l].astype(BF16)
    h1, xn2, xn2_meta = _outproj(y_ssd, o_sb, x, hm, wo[:SSD_INNER], wo[SSD_INNER:], sb_norm_g[l][None],
                                 mix_post_g[l][None], ffn_pre_g[l][None])

    return _ffn(xn2, xn2_meta, h1, w_up[l].astype(BF16), ffn_conv_w[l], ffn_conv_b[l][None],
                w_down[l].astype(BF16), ffn_post_g[l][None])
```

```python
import functools
import math

import jax
import jax.numpy as jnp
from jax import lax
from jax.experimental import pallas as pl
from jax.experimental.pallas import tpu as pltpu

F32 = jnp.float32
BF16 = jnp.bfloat16

D_MODEL = 1024
N_META = 16
BLK = 128
PAD = BLK - N_META
SSD_HEADS = 16
SSD_HEAD_DIM = 64
SSD_GROUPS = 2
SSD_STATE = 128
SSD_INNER = 1024
SSD_CONV = 4
XBC_WIDTH = SSD_INNER + 2 * SSD_GROUPS * SSD_STATE
SB_WIDTH = 1024
SB_HEAD_DIM = 64
D_FF = 2816
FFN_CONV = 3
EPS = 1e-6
DT_LANES = 128
GROUP_COLS = SSD_INNER // SSD_GROUPS
LOG2E = 1.4426950408889634

V7X_VMEM_LIMIT = 56 * 1024 * 1024

INPROJ_TM = 1024
OUTPROJ_TM = 1024
OUTPROJ_SUBTILES = 4
FFN_TM = 1024
FFN_TF = 256
SSD_SEQS_PER_STEP = 2
SB_Q_BLOCKS_PER_STEP = 3


def _rms(x, g):
    return x * lax.rsqrt(jnp.mean(x * x, axis=-1, keepdims=True) + EPS) * g


def _const_spec(shape):
    nd = len(shape)
    return pl.BlockSpec(shape, lambda *_: (0,) * nd, pipeline_mode=pl.Buffered(1))


_PROJ_SPLITS = (("z", SSD_INNER), ("xbc", XBC_WIDTH), ("q", SB_WIDTH), ("k", SB_WIDTH), ("v", SB_WIDTH))
_INPROJ_OUT_WIDTHS = (SSD_INNER, XBC_WIDTH, SB_WIDTH, SB_WIDTH, SB_WIDTH, SB_WIDTH, DT_LANES)
_INPROJ_OUT_DTYPES = (BF16, BF16, BF16, BF16, BF16, BF16, F32)


def _inproj_rows(h, g_ref, w_ref, wdt_ref, out_refs):
    z_ref, xbc_ref, q_ref, k_ref, v0_ref, v1_ref, dt_ref = out_refs
    xn = _rms(h, g_ref[...]).astype(BF16)
    lo = 0
    proj = {}
    for name, width in _PROJ_SPLITS:
        proj[name] = jnp.dot(xn, w_ref[:, lo:lo + width], preferred_element_type=F32)
        lo += width
    z_ref[0] = proj["z"].astype(BF16)
    xbc_ref[0] = proj["xbc"].astype(BF16)
    q_ref[0] = (proj["q"] * (LOG2E / math.sqrt(SB_HEAD_DIM))).astype(BF16)
    k_ref[0] = proj["k"].astype(BF16)
    v = proj["v"].astype(BF16)
    lane = lax.broadcasted_iota(jnp.int32, v.shape, 1)
    even_head = (lane & SB_HEAD_DIM) == 0
    zero = jnp.zeros_like(v)
    v0_ref[0] = jnp.where(even_head, v, zero)
    v1_ref[0] = jnp.where(even_head, zero, v)
    dt_ref[0] = jnp.dot(xn, wdt_ref[...], preferred_element_type=F32)


def _inproj_kernel(x_ref, g_ref, w_ref, wdt_ref, *out_refs):
    _inproj_rows(x_ref[...], g_ref, w_ref, wdt_ref, out_refs)


def _inproj_meta_kernel(hm_ref, g_ref, w_ref, wdt_ref, *refs):
    nout = len(_INPROJ_OUT_WIDTHS)
    out_refs, scratch = refs[nout:2 * nout], refs[2 * nout:]

    @pl.when(pl.program_id(0) == 0)
    def _():
        _inproj_rows(hm_ref[...], g_ref, w_ref, wdt_ref, scratch)

    for out_ref, scr in zip(out_refs, scratch):
        out_ref[...] = scr[...]


def _inproj(x, hm, g, w_main, w_dt):
    bsz, seq, _ = x.shape
    lp = seq + BLK
    tm = INPROJ_TM
    per_seq = seq // tm
    n_main = w_main.shape[1]
    out_shape = [jax.ShapeDtypeStruct((bsz, lp, width), dt)
                 for width, dt in zip(_INPROJ_OUT_WIDTHS, _INPROJ_OUT_DTYPES)]
    weights = [_const_spec((1, D_MODEL)), _const_spec((D_MODEL, n_main)), _const_spec((D_MODEL, DT_LANES))]
    params = pltpu.CompilerParams(dimension_semantics=("arbitrary",), vmem_limit_bytes=V7X_VMEM_LIMIT)

    tile = lambda t: (t // per_seq, t % per_seq, 0)
    outs = pl.pallas_call(
        _inproj_kernel,
        grid=(bsz * per_seq,),
        in_specs=[pl.BlockSpec((tm, D_MODEL), lambda t: (t, 0))] + weights,
        out_specs=[pl.BlockSpec((1, tm, width), tile) for width in _INPROJ_OUT_WIDTHS],
        out_shape=out_shape,
        compiler_params=params,
        name="inproj",
    )(x.reshape(bsz * seq, D_MODEL), g, w_main, w_dt)

    nout = len(outs)
    last_blk = lambda b: (b, lp // BLK - 1, 0)
    return pl.pallas_call(
        _inproj_meta_kernel,
        grid=(bsz,),
        in_specs=[_const_spec((BLK, D_MODEL))] + weights + [pl.BlockSpec(memory_space=pl.ANY)] * nout,
        out_specs=[pl.BlockSpec((1, BLK, width), last_blk) for width in _INPROJ_OUT_WIDTHS],
        out_shape=out_shape,
        input_output_aliases={4 + j: j for j in range(nout)},
        scratch_shapes=[pltpu.VMEM((1, BLK, width), dt)
                        for width, dt in zip(_INPROJ_OUT_WIDTHS, _INPROJ_OUT_DTYPES)],
        compiler_params=params,
        name="inproj_meta",
    )(hm, g, w_main, w_dt, *outs)


def _dot_f32_by_01(lhs, rhs, f32_side):
    x = lhs if f32_side == "lhs" else rhs
    p1 = x.astype(BF16)
    r1 = x - p1.astype(F32)
    p2 = r1.astype(BF16)
    p3 = (r1 - p2.astype(F32)).astype(BF16)
    out = None
    for piece in (p1, p2, p3):
        ops = (piece, rhs) if f32_side == "lhs" else (lhs, piece)
        term = jnp.dot(*ops, preferred_element_type=F32)
        out = term if out is None else out + term
    return out


def _ssd_kernel(xbc_ref, z_ref, dt_ref, cw_ref, cb_ref, dtb_ref, alog_ref, dexp_ref, g_ref, e_ref,
                out_ref, prev, state, *, nseq):
    c = pl.program_id(1)

    @pl.when(c == 0)
    def _():
        prev[...] = jnp.zeros_like(prev)
        state[...] = jnp.zeros_like(state)

    seqs = range(nseq)
    groups = range(SSD_GROUPS)
    row = lax.broadcasted_iota(jnp.int32, (BLK, 1), 0)
    real = jnp.logical_or(c > 0, row >= PAD)
    lane = lax.broadcasted_iota(jnp.int32, (BLK, DT_LANES), 1)
    dt_valid = jnp.logical_and(real, lane < SSD_HEADS)
    ri = lax.broadcasted_iota(jnp.int32, (BLK, BLK), 0)
    ci = lax.broadcasted_iota(jnp.int32, (BLK, BLK), 1)
    causal = ri >= ci
    tri = jnp.where(causal, 1.0, 0.0).astype(BF16)
    half = ci < SSD_HEAD_DIM
    cw = cw_ref[...]
    e = e_ref[...]
    a_neg = -jnp.exp(alog_ref[...])

    xc, dt, adt = [], [], []
    for s in seqs:
        x_raw = xbc_ref[s].astype(F32)
        prev[s, 8:8 + BLK, :] = x_raw
        acc = cb_ref[...] + cw[3:4, :] * x_raw
        for tap in range(SSD_CONV - 1):
            acc = acc + cw[tap:tap + 1, :] * prev[s, 5 + tap:5 + tap + BLK, :]
        prev[s, 0:8, :] = x_raw[BLK - 8:, :]
        xc.append(jnp.where(real, acc * jax.nn.sigmoid(acc), 0.0))
        dts = jax.nn.softplus(dt_ref[s] + dtb_ref[...])
        dt.append(jnp.where(dt_valid, dts, 0.0))
        adt.append(dt[s] * a_neg)
    xs = [xc[s][:, :SSD_INNER] for s in seqs]

    acs = [_dot_f32_by_01(tri, adt[s], f32_side="rhs") for s in seqs]
    acs_t = [acs[s].T for s in seqs]
    dt_exp = [_dot_f32_by_01(dt[s], e, f32_side="lhs") for s in seqs]
    acs_exp = [_dot_f32_by_01(acs[s], e, f32_side="lhs") for s in seqs]

    xdt_b, xdec, off_scale, chunk_decay, b_t, c_gb, b_gb = [], [], [], [], [], [], []
    for s in seqs:
        last = acs_exp[s][BLK - 1:BLK, :]
        xdt = xs[s] * dt_exp[s]
        xdec.append((xdt * jnp.exp(last - acs_exp[s])).astype(BF16))
        xdt_b.append(xdt.astype(BF16))
        off_scale.append(jnp.exp(acs_exp[s]))
        chunk_decay.append(jnp.exp(last))
        b_g = [xc[s][:, SSD_INNER + g * SSD_STATE:SSD_INNER + (g + 1) * SSD_STATE] for g in groups]
        c_g = [xc[s][:, SSD_INNER + (SSD_GROUPS + g) * SSD_STATE:
                     SSD_INNER + (SSD_GROUPS + g + 1) * SSD_STATE] for g in groups]
        b_gb.append([x.astype(BF16) for x in b_g])
        c_gb.append([x.astype(BF16) for x in c_g])
        b_t.append([x.T.astype(BF16) for x in b_g])

    gcols = [slice(g * GROUP_COLS, (g + 1) * GROUP_COLS) for g in groups]
    cb_mat = [[lax.dot_general(c_gb[s][g], b_gb[s][g], (((1,), (1,)), ((), ())),
                               preferred_element_type=F32) for g in groups] for s in seqs]
    s_prev = [[state[s, :, gcols[g]] for g in groups] for s in seqs]
    y_off = [[jnp.dot(c_gb[s][g], s_prev[s][g].astype(BF16), preferred_element_type=F32)
              for g in groups] for s in seqs]

    heads_per_group = SSD_HEADS // SSD_GROUPS
    decay_mats = [[None] * SSD_HEADS for _ in seqs]
    for s in seqs:
        for hh in range(SSD_HEADS):
            seg = acs[s][:, hh:hh + 1] - acs_t[s][hh:hh + 1, :]
            ldec = jnp.exp(jnp.where(causal, seg, -1e30))
            decay_mats[s][hh] = (cb_mat[s][hh // heads_per_group] * ldec).astype(BF16)

    y_diag = [[jnp.dot(decay_mats[s][hh], xdt_b[s][:, (hh // 2) * BLK:(hh // 2 + 1) * BLK],
                       preferred_element_type=F32) for hh in range(SSD_HEADS)] for s in seqs]
    new = [[jnp.dot(b_t[s][g], xdec[s][:, gcols[g]], preferred_element_type=F32)
            for g in groups] for s in seqs]

    for s in seqs:
        y_cols = []
        for g in groups:
            state[s, :, gcols[g]] = s_prev[s][g] * chunk_decay[s][:, gcols[g]] + new[s][g]
            yo = y_off[s][g] * off_scale[s][:, gcols[g]]
            for pair in range(GROUP_COLS // BLK):
                h0 = (g * GROUP_COLS + pair * BLK) // SSD_HEAD_DIM
                y_cols.append(jnp.where(half, y_diag[s][h0], y_diag[s][h0 + 1])
                              + yo[:, pair * BLK:(pair + 1) * BLK])
        y = jnp.concatenate(y_cols, axis=1) + xs[s] * dexp_ref[...]
        zf = z_ref[s].astype(F32)
        y = y * (zf * jax.nn.sigmoid(zf))
        out_ref[s] = _rms(y, g_ref[...]).astype(BF16)


def _ssd(xbc, z, dt_raw, conv_w, conv_b, dt_bias, a_log, d_exp, norm_g, expand):
    bsz, lp, _ = xbc.shape
    nblk = lp // BLK
    nseq = SSD_SEQS_PER_STEP
    assert bsz % nseq == 0
    blk = lambda b, c: (b, (c + nblk - 1) % nblk, 0)
    return pl.pallas_call(
        functools.partial(_ssd_kernel, nseq=nseq),
        grid=(bsz // nseq, nblk),
        in_specs=[
            pl.BlockSpec((nseq, BLK, XBC_WIDTH), blk),
            pl.BlockSpec((nseq, BLK, SSD_INNER), blk),
            pl.BlockSpec((nseq, BLK, DT_LANES), blk),
            _const_spec((SSD_CONV, XBC_WIDTH)),
            _const_spec((1, XBC_WIDTH)),
            _const_spec((1, DT_LANES)),
            _const_spec((1, DT_LANES)),
            _const_spec((1, SSD_INNER)),
            _const_spec((1, SSD_INNER)),
            _const_spec((DT_LANES, SSD_INNER)),
        ],
        out_specs=pl.BlockSpec((nseq, BLK, SSD_INNER), blk),
        out_shape=jax.ShapeDtypeStruct((bsz, lp, SSD_INNER), BF16),
        scratch_shapes=[
            pltpu.VMEM((nseq, BLK + 8, XBC_WIDTH), F32),
            pltpu.VMEM((nseq, SSD_STATE, SSD_INNER), F32),
        ],
        compiler_params=pltpu.CompilerParams(
            dimension_semantics=("arbitrary", "arbitrary"), vmem_limit_bytes=V7X_VMEM_LIMIT),
        name="ssd",
    )(xbc, z, dt_raw, conv_w, conv_b, dt_bias, a_log, d_exp, norm_g, expand)


SB_DEAD_LOG2 = -151.0
SB_LEAD = 3
SB_TOP_ROWS = 48
FULL, TOP, BOT = slice(0, BLK), slice(0, SB_TOP_ROWS), slice(SB_TOP_ROWS, BLK)


def _sb_kv_copies(hbm_refs, bufs, sems, batch, part, nblk, qpb):
    nparts = nblk // qpb
    seq_rows = (nblk - 1) * BLK
    if part == nparts:
        rows = pl.ds(seq_rows, BLK)
    else:
        start = part * qpb * BLK
        rows = pl.ds(start, min(qpb * BLK, seq_rows - start))
    return [pltpu.make_async_copy(src.at[batch, rows, :], dst.at[rows, :], sems.at[a, part])
            for a, (src, dst) in enumerate(zip(hbm_refs, bufs))]


def _sb_kernel(q_ref, k_hbm, v0_hbm, v1_hbm, tt_ref, o_ref, k_ref, v0_ref, v1_ref, kv_sems,
               qm_scr, run_scr, acc_scr, worst_scr, *, nblk, npairs, qpb):
    batch, step = pl.program_id(0), pl.program_id(1)
    nparts = nblk // qpb
    hbm_refs, bufs = (k_hbm, v0_hbm, v1_hbm), (k_ref, v0_ref, v1_ref)

    @pl.when(step == 0)
    def _():
        for part in [nparts] + list(range(nparts)):
            for cp in _sb_kv_copies(hbm_refs, bufs, kv_sems, batch, part, nblk, qpb):
                cp.start()
        for cp in _sb_kv_copies(hbm_refs, bufs, kv_sems, batch, nparts, nblk, qpb):
            cp.wait()

    for part in range(nparts):
        @pl.when(step == part)
        def _(part=part):
            for cp in _sb_kv_copies(hbm_refs, bufs, kv_sems, batch, part, nblk, qpb):
                cp.wait()

    lane = lax.broadcasted_iota(jnp.int32, (BLK, BLK), 1)
    sub = lax.broadcasted_iota(jnp.int32, (BLK, BLK), 0)
    first = lane < SB_HEAD_DIM
    strictly_older = lane < sub
    tt = tt_ref[...]

    def query_block(r, carry):
        qi = (step * qpb + r + 1) % nblk
        rows = pl.ds(pl.multiple_of(r * BLK, BLK), BLK)
        for p in range(npairs):
            qp = q_ref[0, rows, p * BLK:(p + 1) * BLK]
            zero = jnp.zeros_like(qp)
            qm_scr[p, :BLK, :] = jnp.where(first, qp, zero)
            qm_scr[p, BLK:, :] = jnp.where(first, zero, qp)

        def process(blocks, init, check, row_sets=None):
            offs = [pl.multiple_of(((j + nblk - 1) % nblk) * BLK, BLK) for j in blocks]
            nb = range(len(blocks))
            row_sets = row_sets or [FULL] * len(blocks)
            heads = range(2 * npairs)
            scores = {}
            for b in nb:
                rs = row_sets[b]
                for p in range(npairs):
                    if rs is FULL:
                        lhs = qm_scr[p]
                    else:
                        lhs = jnp.concatenate([qm_scr[p, rs, :],
                                               qm_scr[p, BLK + rs.start:BLK + rs.stop, :]], axis=0)
                    scores[b, p] = lax.dot_general(
                        lhs, k_ref[pl.ds(offs[b], BLK), p * BLK:(p + 1) * BLK],
                        (((1,), (1,)), ((), ())), preferred_element_type=F32)
            log_betas, log_keeps, totals = {}, {}, {}
            for b in nb:
                n = row_sets[b].stop - row_sets[b].start
                for h in heads:
                    s = scores[b, h // 2][(h % 2) * n:(h % 2 + 1) * n, :]
                    neg_part = jnp.minimum(s, 0.0)
                    neg_pos = neg_part - s
                    sp = jnp.log(1.0 + jnp.exp2(neg_part + neg_pos)) * LOG2E
                    log_beta = neg_part - sp
                    log_keep = neg_pos - sp
                    if init and b == 0:
                        log_keep = jnp.where(strictly_older, log_keep, 0.0)
                    log_betas[b, h] = log_beta
                    log_keeps[b, h] = log_keep.astype(BF16)
                    totals[b, h] = jnp.sum(log_keep, axis=1, keepdims=True)
            sums = [jnp.dot(jnp.concatenate([log_keeps[b, h] for h in heads], axis=0), tt,
                            preferred_element_type=F32) for b in nb]
            weights = {}
            worst = None
            for h in heads:
                run = None if init else run_scr[h]
                for b in nb:
                    rs = row_sets[b]
                    n = rs.stop - rs.start
                    after, total = sums[b][h * n:(h + 1) * n, :], totals[b, h]
                    if run is None:
                        w = jnp.where(strictly_older, jnp.exp2(log_betas[b, h] + after), 0.0)
                        run = jnp.broadcast_to(total, (BLK, BLK))
                    else:
                        part = run if rs is FULL else run[rs, :]
                        w = jnp.exp2(log_betas[b, h] + (after + part))
                        part = part + total
                        if rs is FULL:
                            run = part
                        elif rs is TOP:
                            run = jnp.concatenate([part, run[BOT, :]], axis=0)
                        else:
                            run = jnp.concatenate([run[TOP, :], part], axis=0)
                    weights[b, h] = w.astype(BF16)
                run_scr[h] = run
                if check:
                    worst = run if worst is None else jnp.maximum(worst, run)
            for p in range(npairs):
                pv = None if init else acc_scr[p]
                for b in nb:
                    rs = row_sets[b]
                    v_heads = jnp.concatenate(
                        [v0_ref[pl.ds(offs[b], BLK), p * BLK:(p + 1) * BLK],
                         v1_ref[pl.ds(offs[b], BLK), p * BLK:(p + 1) * BLK]], axis=0)
                    w_heads = jnp.concatenate([weights[b, 2 * p], weights[b, 2 * p + 1]], axis=1)
                    contrib = jnp.dot(w_heads, v_heads, preferred_element_type=F32)
                    if pv is None:
                        pv = contrib
                    elif rs is FULL:
                        pv = pv + contrib
                    elif rs is TOP:
                        pv = jnp.concatenate([pv[TOP, :] + contrib, pv[BOT, :]], axis=0)
                    else:
                        pv = jnp.concatenate([pv[TOP, :], pv[BOT, :] + contrib], axis=0)
                acc_scr[p] = pv
            if check:
                worst_scr[0] = jnp.max(worst[TOP, :])
                worst_scr[1] = jnp.max(worst[BOT, :])

        for lead in range(1, SB_LEAD + 1):
            last = lead == SB_LEAD

            @pl.when(qi >= lead - 1 if last else qi == lead - 1)
            def _(lead=lead, last=last):
                if last:
                    process([qi - i for i in range(lead)], True, True, [FULL] * (lead - 1) + [TOP])
                else:
                    process([qi - i for i in range(lead)], True, False)
                    worst_scr[0] = jnp.float32(SB_DEAD_LOG2)
                    worst_scr[1] = jnp.float32(SB_DEAD_LOG2)

        @pl.when(jnp.logical_and(qi >= SB_LEAD - 1, worst_scr[1] > SB_DEAD_LOG2))
        def _():
            process([qi - (SB_LEAD - 1)], False, True, [BOT])

        def cond(st):
            return jnp.logical_and(st[0] >= 0, st[1] > SB_DEAD_LOG2)

        def body(st):
            process([st[0]], False, True)
            return st[0] - 1, jnp.maximum(worst_scr[0], worst_scr[1])

        lax.while_loop(cond, body, (qi - SB_LEAD, jnp.maximum(worst_scr[0], worst_scr[1])))
        for p in range(npairs):
            o_ref[0, rows, p * BLK:(p + 1) * BLK] = acc_scr[p].astype(BF16)
        return carry

    lax.fori_loop(0, qpb, query_block, 0)


def _sb_attention(q, k, v0, v1, tt):
    bsz, lp, width = q.shape
    nblk = lp // BLK
    qpb = SB_Q_BLOCKS_PER_STEP
    assert nblk % qpb == 0
    npairs = width // BLK
    nparts = nblk // qpb
    qblk = lambda b, g: (b, g, 0)
    kv_hbm = pl.BlockSpec(memory_space=pl.ANY)
    return pl.pallas_call(
        functools.partial(_sb_kernel, nblk=nblk, npairs=npairs, qpb=qpb),
        grid=(bsz, nparts),
        in_specs=[
            pl.BlockSpec((1, qpb * BLK, width), qblk),
            kv_hbm,
            kv_hbm,
            kv_hbm,
            _const_spec((BLK, BLK)),
        ],
        out_specs=pl.BlockSpec((1, qpb * BLK, width), qblk),
        out_shape=jax.ShapeDtypeStruct((bsz, lp, width), BF16),
        scratch_shapes=[
            pltpu.VMEM((lp, width), BF16),
            pltpu.VMEM((lp, width), BF16),
            pltpu.VMEM((lp, width), BF16),
            pltpu.SemaphoreType.DMA((3, nparts + 1)),
            pltpu.VMEM((npairs, 2 * BLK, BLK), BF16),
            pltpu.VMEM((2 * npairs, BLK, BLK), F32),
            pltpu.VMEM((npairs, BLK, BLK), F32),
            pltpu.SMEM((2,), F32),
        ],
        compiler_params=pltpu.CompilerParams(
            dimension_semantics=("arbitrary", "arbitrary"), vmem_limit_bytes=V7X_VMEM_LIMIT),
        name="sb_attention",
    )(q, k, v0, v1, tt)


def _outproj_rows(ys, o, h, wa_ref, wb_ref, gsb_ref, gpost_ref, gffn_ref):
    osb = _rms(o.astype(F32), gsb_ref[...]).astype(BF16)
    mix = jnp.dot(ys, wa_ref[...], preferred_element_type=F32)
    mix = mix + jnp.dot(osb, wb_ref[...], preferred_element_type=F32)
    h1 = h + _rms(mix, gpost_ref[...])
    return h1, _rms(h1, gffn_ref[...]).astype(BF16)


def _outproj_kernel(ys_ref, o_ref, h_ref, wa_ref, wb_ref, gsb_ref, gpost_ref, gffn_ref,
                    h1_ref, xn2_ref, *, nsub):
    sub = h_ref.shape[1] // nsub
    rows = [pl.ds(k * sub, sub) for k in range(nsub)]
    osb = [_rms(o_ref[0, r, :].astype(F32), gsb_ref[...]).astype(BF16) for r in rows]
    mix = [jnp.dot(ys_ref[0, r, :], wa_ref[...], preferred_element_type=F32)
           + jnp.dot(osb[k], wb_ref[...], preferred_element_type=F32) for k, r in enumerate(rows)]
    for k, r in enumerate(rows):
        h1 = h_ref[0, r, :] + _rms(mix[k], gpost_ref[...])
        h1_ref[0, r, :] = h1
        xn2_ref[0, r, :] = _rms(h1, gffn_ref[...]).astype(BF16)


def _outproj_meta_kernel(ys_ref, o_ref, hm_ref, wa_ref, wb_ref, gsb_ref, gpost_ref, gffn_ref, xn2_ref):
    _, xn2 = _outproj_rows(ys_ref[0], o_ref[0], hm_ref[...], wa_ref, wb_ref, gsb_ref, gpost_ref, gffn_ref)
    xn2_ref[0] = xn2


def _outproj(y_ssd, o_sb, x, hm, w_a, w_b, g_sb, g_post, g_ffn):
    bsz, seq, _ = x.shape
    lp = y_ssd.shape[1]
    tm = OUTPROJ_TM
    per_seq = seq // tm
    weights = [
        _const_spec((SSD_INNER, D_MODEL)),
        _const_spec((SB_WIDTH, D_MODEL)),
        _const_spec((1, SB_WIDTH)),
        _const_spec((1, D_MODEL)),
        _const_spec((1, D_MODEL)),
    ]
    params = pltpu.CompilerParams(dimension_semantics=("arbitrary",), vmem_limit_bytes=V7X_VMEM_LIMIT)
    tile = lambda t: (t // per_seq, t % per_seq, 0)
    h1, xn2 = pl.pallas_call(
        functools.partial(_outproj_kernel, nsub=OUTPROJ_SUBTILES),
        grid=(bsz * per_seq,),
        in_specs=[
            pl.BlockSpec((1, tm, SSD_INNER), tile),
            pl.BlockSpec((1, tm, SB_WIDTH), tile),
            pl.BlockSpec((1, tm, D_MODEL), tile),
        ] + weights,
        out_specs=[pl.BlockSpec((1, tm, D_MODEL), tile), pl.BlockSpec((1, tm, D_MODEL), tile)],
        out_shape=[jax.ShapeDtypeStruct((bsz, seq, D_MODEL), F32),
                   jax.ShapeDtypeStruct((bsz, seq, D_MODEL), BF16)],
        compiler_params=params,
        name="outproj",
    )(y_ssd, o_sb, x, w_a, w_b, g_sb, g_post, g_ffn)

    last_blk = lambda b: (0, lp // BLK - 1, 0)
    xn2_meta = pl.pallas_call(
        _outproj_meta_kernel,
        grid=(1,),
        in_specs=[
            pl.BlockSpec((1, BLK, SSD_INNER), last_blk),
            pl.BlockSpec((1, BLK, SB_WIDTH), last_blk),
            _const_spec((BLK, D_MODEL)),
        ] + weights,
        out_specs=pl.BlockSpec((1, BLK, D_MODEL), lambda b: (0, 0, 0)),
        out_shape=jax.ShapeDtypeStruct((1, BLK, D_MODEL), BF16),
        compiler_params=params,
        name="outproj_meta",
    )(y_ssd, o_sb, hm, w_a, w_b, g_sb, g_post, g_ffn)
    return h1, xn2, xn2_meta


def _gelu_tanh(x):
    return 0.5 * x * (1.0 + jnp.tanh(math.sqrt(2.0 / math.pi) * (x + 0.044715 * (x * x * x))))


def _ffn_kernel(xn_ref, xm_ref, h1_ref, wup_ref, cw_ref, cb_ref, wd_ref, g_ref, out_ref,
                gbuf0, gbuf1, ubuf0, ubuf1, halo, act_scr, *, tm, tf):
    nch = D_FF // tf
    i = pl.program_id(1)
    gbufs, ubufs = (gbuf0, gbuf1), (ubuf0, ubuf1)
    cols = [slice(c * tf, (c + 1) * tf) for c in range(nch)]
    up_cols = [slice(D_FF + c * tf, D_FF + (c + 1) * tf) for c in range(nch)]

    @pl.when(i == 0)
    def _():
        for c in range(nch):
            gm = jnp.dot(xm_ref[0], wup_ref[:, cols[c]], preferred_element_type=F32)
            halo[c] = gm[N_META - 8:, :]

    def project(c, slot):
        gbuf, ubuf = gbufs[slot], ubufs[slot]
        xn = xn_ref[0]
        gate = jnp.dot(xn, wup_ref[:, cols[c]], preferred_element_type=F32)
        gbuf[0:8, :] = halo[c]
        gbuf[8:8 + tm, :] = gate
        halo[c] = gate[tm - 8:, :]
        ubuf[...] = jnp.dot(xn, wup_ref[:, up_cols[c]], preferred_element_type=F32)

    def activate(c, slot):
        gbuf, ubuf = gbufs[slot], ubufs[slot]
        cw = cw_ref[:, cols[c]]
        conv = cb_ref[:, cols[c]] + cw[2:3, :] * gbuf[8:8 + tm, :]
        conv = conv + cw[1:2, :] * gbuf[7:7 + tm, :] + cw[0:1, :] * gbuf[6:6 + tm, :]
        act_scr[c] = (_gelu_tanh(conv) * ubuf[...]).astype(BF16)

    project(0, 0)
    for c in range(nch - 1):
        project(c + 1, (c + 1) % 2)
        activate(c, c % 2)
    activate(nch - 1, (nch - 1) % 2)
    act = jnp.concatenate([act_scr[c] for c in range(nch)], axis=1)
    down = jnp.dot(act, wd_ref[...], preferred_element_type=F32)
    out_ref[0] = h1_ref[0] + _rms(down, g_ref[...])


def _ffn(xn2, xn2_meta, h1, w_up, conv_w, conv_b, w_down, g_post):
    bsz, seq, _ = xn2.shape
    tm, tf = FFN_TM, FFN_TF
    nch = D_FF // tf
    xblk = lambda b, i: (b, i, 0)
    meta = lambda b, i: (0, BLK // N_META - 1, 0)
    return pl.pallas_call(
        functools.partial(_ffn_kernel, tm=tm, tf=tf),
        grid=(bsz, seq // tm),
        in_specs=[
            pl.BlockSpec((1, tm, D_MODEL), xblk),
            pl.BlockSpec((1, N_META, D_MODEL), meta),
            pl.BlockSpec((1, tm, D_MODEL), xblk),
            _const_spec((D_MODEL, 2 * D_FF)),
            _const_spec((FFN_CONV, D_FF)),
            _const_spec((1, D_FF)),
            _const_spec((D_FF, D_MODEL)),
            _const_spec((1, D_MODEL)),
        ],
        out_specs=pl.BlockSpec((1, tm, D_MODEL), xblk),
        out_shape=jax.ShapeDtypeStruct((bsz, seq, D_MODEL), F32),
        scratch_shapes=[
            pltpu.VMEM((tm + 8, tf), F32),
            pltpu.VMEM((tm + 8, tf), F32),
            pltpu.VMEM((tm, tf), F32),
            pltpu.VMEM((tm, tf), F32),
            pltpu.VMEM((nch, 8, tf), F32),
            pltpu.VMEM((nch, tm, tf), BF16),
        ],
        compiler_params=pltpu.CompilerParams(
            dimension_semantics=("arbitrary", "arbitrary"), vmem_limit_bytes=V7X_VMEM_LIMIT),
        name="ffn",
    )(xn2, xn2_meta, h1, w_up, conv_w, conv_b, w_down, g_post)


def _pad_lanes(v, n):
    return jnp.pad(v, ((0, 0), (0, n - v.shape[-1])))


def kernel(x, meta_tokens, mix_pre_g, w_in, ssd_conv_w, ssd_conv_b, ssd_dt_bias, ssd_a_log, ssd_d,
           ssd_norm_g, sb_norm_g, w_out, mix_post_g, ffn_pre_g, w_up, ffn_conv_w, ffn_conv_b,
           w_down, ffn_post_g):
    bsz, seq, d = x.shape
    depth = w_in.shape[0]
    assert depth == 1 and d == D_MODEL and seq % OUTPROJ_TM == 0

    hm = jnp.concatenate([jnp.zeros((PAD, d), x.dtype), meta_tokens.astype(x.dtype)], axis=0)

    l = 0
    off_xbc = SSD_INNER
    off_dt = off_xbc + XBC_WIDTH
    off_q = off_dt + SSD_HEADS
    w = w_in[l]
    w_main = jnp.concatenate([w[:, :off_dt], w[:, off_q:]], axis=1).astype(BF16)
    w_dt = _pad_lanes(w[:, off_dt:off_q], DT_LANES).astype(BF16)
    z, xbc, q, k, v0, v1, dt_raw = _inproj(x, hm, mix_pre_g[l][None], w_main, w_dt)

    head_of_col = jnp.arange(SSD_INNER) // SSD_HEAD_DIM
    expand = (jnp.arange(DT_LANES)[:, None] == head_of_col[None, :]).astype(BF16)
    d_exp = jnp.repeat(ssd_d[l].astype(F32), SSD_HEAD_DIM)[None]
    y_ssd = _ssd(xbc, z, dt_raw, ssd_conv_w[l], ssd_conv_b[l][None],
                 _pad_lanes(ssd_dt_bias[l][None], DT_LANES), _pad_lanes(ssd_a_log[l][None], DT_LANES),
                 d_exp, ssd_norm_g[l][None], expand)

    kk = jnp.arange(BLK)
    tt = (kk[:, None] > kk[None, :]).astype(BF16)
    o_sb = _sb_attention(q, k, v0, v1, tt)

    wo = w_out[l].astype(BF16)
    h1, xn2, xn2_meta = _outproj(y_ssd, o_sb, x, hm, wo[:SSD_INNER], wo[SSD_INNER:], sb_norm_g[l][None],
                                 mix_post_g[l][None], ffn_pre_g[l][None])

    return _ffn(xn2, xn2_meta, h1, w_up[l].astype(BF16), ffn_conv_w[l], ffn_conv_b[l][None],
                w_down[l].astype(BF16), ffn_post_g[l][None])
```

```python
import functools
import math

import jax
import jax.numpy as jnp
from jax import lax
from jax.experimental import pallas as pl
from jax.experimental.pallas import tpu as pltpu

F32 = jnp.float32
BF16 = jnp.bfloat16

D_MODEL = 1024
N_META = 16
BLK = 128
PAD = BLK - N_META
SSD_HEADS = 16
SSD_HEAD_DIM = 64
SSD_GROUPS = 2
SSD_STATE = 128
SSD_INNER = 1024
SSD_CONV = 4
XBC_WIDTH = SSD_INNER + 2 * SSD_GROUPS * SSD_STATE
SB_WIDTH = 1024
SB_HEAD_DIM = 64
D_FF = 2816
FFN_CONV = 3
EPS = 1e-6
DT_LANES = 128
GROUP_COLS = SSD_INNER // SSD_GROUPS
LOG2E = 1.4426950408889634

V7X_VMEM_LIMIT = 56 * 1024 * 1024

INPROJ_TM = 1024
OUTPROJ_TM = 1024
OUTPROJ_SUBTILES = 8
FFN_TM = 1024
FFN_TF = 256
SSD_SEQS_PER_STEP = 2
SB_Q_BLOCKS_PER_STEP = 3


def _rms(x, g):
    return x * lax.rsqrt(jnp.mean(x * x, axis=-1, keepdims=True) + EPS) * g


def _const_spec(shape):
    nd = len(shape)
    return pl.BlockSpec(shape, lambda *_: (0,) * nd, pipeline_mode=pl.Buffered(1))


_PROJ_SPLITS = (("z", SSD_INNER), ("xbc", XBC_WIDTH), ("q", SB_WIDTH), ("k", SB_WIDTH), ("v", SB_WIDTH))
_INPROJ_OUT_WIDTHS = (SSD_INNER, XBC_WIDTH, SB_WIDTH, SB_WIDTH, SB_WIDTH, SB_WIDTH, DT_LANES)
_INPROJ_OUT_DTYPES = (BF16, BF16, BF16, BF16, BF16, BF16, F32)


def _inproj_rows(h, g_ref, w_ref, wdt_ref, out_refs):
    z_ref, xbc_ref, q_ref, k_ref, v0_ref, v1_ref, dt_ref = out_refs
    xn = _rms(h, g_ref[...]).astype(BF16)
    lo = 0
    proj = {}
    for name, width in _PROJ_SPLITS:
        proj[name] = jnp.dot(xn, w_ref[:, lo:lo + width], preferred_element_type=F32)
        lo += width
    z_ref[0] = (proj["z"] * jax.nn.sigmoid(proj["z"])).astype(BF16)
    xbc_ref[0] = proj["xbc"].astype(BF16)
    q_ref[0] = (proj["q"] * (LOG2E / math.sqrt(SB_HEAD_DIM))).astype(BF16)
    k_ref[0] = proj["k"].astype(BF16)
    v = proj["v"].astype(BF16)
    lane = lax.broadcasted_iota(jnp.int32, v.shape, 1)
    even_head = (lane & SB_HEAD_DIM) == 0
    zero = jnp.zeros_like(v)
    v0_ref[0] = jnp.where(even_head, v, zero)
    v1_ref[0] = jnp.where(even_head, zero, v)
    dt_ref[0] = jnp.dot(xn, wdt_ref[...], preferred_element_type=F32)


def _inproj_kernel(x_ref, g_ref, w_ref, wdt_ref, *out_refs):
    _inproj_rows(x_ref[...], g_ref, w_ref, wdt_ref, out_refs)


def _inproj_meta_kernel(hm_ref, g_ref, w_ref, wdt_ref, *refs):
    nout = len(_INPROJ_OUT_WIDTHS)
    out_refs, scratch = refs[nout:2 * nout], refs[2 * nout:]

    @pl.when(pl.program_id(0) == 0)
    def _():
        _inproj_rows(hm_ref[...], g_ref, w_ref, wdt_ref, scratch)

    for out_ref, scr in zip(out_refs, scratch):
        out_ref[...] = scr[...]


def _inproj(x, hm, g, w_main, w_dt):
    bsz, seq, _ = x.shape
    lp = seq + BLK
    tm = INPROJ_TM
    per_seq = seq // tm
    n_main = w_main.shape[1]
    out_shape = [jax.ShapeDtypeStruct((bsz, lp, width), dt)
                 for width, dt in zip(_INPROJ_OUT_WIDTHS, _INPROJ_OUT_DTYPES)]
    weights = [_const_spec((1, D_MODEL)), _const_spec((D_MODEL, n_main)), _const_spec((D_MODEL, DT_LANES))]
    params = pltpu.CompilerParams(dimension_semantics=("arbitrary",), vmem_limit_bytes=V7X_VMEM_LIMIT)

    tile = lambda t: (t // per_seq, t % per_seq, 0)
    outs = pl.pallas_call(
        _inproj_kernel,
        grid=(bsz * per_seq,),
        in_specs=[pl.BlockSpec((tm, D_MODEL), lambda t: (t, 0))] + weights,
        out_specs=[pl.BlockSpec((1, tm, width), tile) for width in _INPROJ_OUT_WIDTHS],
        out_shape=out_shape,
        compiler_params=params,
        name="inproj",
    )(x.reshape(bsz * seq, D_MODEL), g, w_main, w_dt)

    nout = len(outs)
    last_blk = lambda b: (b, lp // BLK - 1, 0)
    return pl.pallas_call(
        _inproj_meta_kernel,
        grid=(bsz,),
        in_specs=[_const_spec((BLK, D_MODEL))] + weights + [pl.BlockSpec(memory_space=pl.ANY)] * nout,
        out_specs=[pl.BlockSpec((1, BLK, width), last_blk) for width in _INPROJ_OUT_WIDTHS],
        out_shape=out_shape,
        input_output_aliases={4 + j: j for j in range(nout)},
        scratch_shapes=[pltpu.VMEM((1, BLK, width), dt)
                        for width, dt in zip(_INPROJ_OUT_WIDTHS, _INPROJ_OUT_DTYPES)],
        compiler_params=params,
        name="inproj_meta",
    )(hm, g, w_main, w_dt, *outs)


def _dot_f32_by_01(lhs, rhs, f32_side):
    x = lhs if f32_side == "lhs" else rhs
    p1 = x.astype(BF16)
    r1 = x - p1.astype(F32)
    p2 = r1.astype(BF16)
    p3 = (r1 - p2.astype(F32)).astype(BF16)
    out = None
    for piece in (p1, p2, p3):
        ops = (piece, rhs) if f32_side == "lhs" else (lhs, piece)
        term = jnp.dot(*ops, preferred_element_type=F32)
        out = term if out is None else out + term
    return out


def _ssd_kernel(xbc_ref, z_ref, dt_ref, cw_ref, cb_ref, dtb_ref, alog_ref, dexp_ref, g_ref, e_ref,
                out_ref, prev, state, *, nseq):
    c = pl.program_id(1)

    @pl.when(c == 0)
    def _():
        prev[...] = jnp.zeros_like(prev)
        state[...] = jnp.zeros_like(state)

    seqs = range(nseq)
    groups = range(SSD_GROUPS)
    row = lax.broadcasted_iota(jnp.int32, (BLK, 1), 0)
    real = jnp.logical_or(c > 0, row >= PAD)
    lane = lax.broadcasted_iota(jnp.int32, (BLK, DT_LANES), 1)
    dt_valid = jnp.logical_and(real, lane < SSD_HEADS)
    ri = lax.broadcasted_iota(jnp.int32, (BLK, BLK), 0)
    ci = lax.broadcasted_iota(jnp.int32, (BLK, BLK), 1)
    causal = ri >= ci
    tri = jnp.where(causal, 1.0, 0.0).astype(BF16)
    half = ci < SSD_HEAD_DIM
    cw = cw_ref[...]
    e = e_ref[...]
    a_neg = -jnp.exp(alog_ref[...])

    xc, dt, adt = [], [], []
    for s in seqs:
        x_raw = xbc_ref[s].astype(F32)
        prev[s, 8:8 + BLK, :] = x_raw
        acc = cb_ref[...] + cw[3:4, :] * x_raw
        for tap in range(SSD_CONV - 1):
            acc = acc + cw[tap:tap + 1, :] * prev[s, 5 + tap:5 + tap + BLK, :]
        prev[s, 0:8, :] = x_raw[BLK - 8:, :]
        xc.append(jnp.where(real, acc * jax.nn.sigmoid(acc), 0.0))
        dts = jax.nn.softplus(dt_ref[s] + dtb_ref[...])
        dt.append(jnp.where(dt_valid, dts, 0.0))
        adt.append(dt[s] * a_neg)
    xs = [xc[s][:, :SSD_INNER] for s in seqs]

    acs = [_dot_f32_by_01(tri, adt[s], f32_side="rhs") for s in seqs]
    acs_t = [acs[s].T for s in seqs]
    dt_exp = [_dot_f32_by_01(dt[s], e, f32_side="lhs") for s in seqs]
    acs_exp = [_dot_f32_by_01(acs[s], e, f32_side="lhs") for s in seqs]

    xdt_b, xdec, off_scale, chunk_decay, b_t, c_gb, b_gb = [], [], [], [], [], [], []
    for s in seqs:
        last = acs_exp[s][BLK - 1:BLK, :]
        xdt = xs[s] * dt_exp[s]
        xdec.append((xdt * jnp.exp(last - acs_exp[s])).astype(BF16))
        xdt_b.append(xdt.astype(BF16))
        off_scale.append(jnp.exp(acs_exp[s]))
        chunk_decay.append(jnp.exp(last))
        b_g = [xc[s][:, SSD_INNER + g * SSD_STATE:SSD_INNER + (g + 1) * SSD_STATE] for g in groups]
        c_g = [xc[s][:, SSD_INNER + (SSD_GROUPS + g) * SSD_STATE:
                     SSD_INNER + (SSD_GROUPS + g + 1) * SSD_STATE] for g in groups]
        b_gb.append([x.astype(BF16) for x in b_g])
        c_gb.append([x.astype(BF16) for x in c_g])
        b_t.append([x.T.astype(BF16) for x in b_g])

    gcols = [slice(g * GROUP_COLS, (g + 1) * GROUP_COLS) for g in groups]
    cb_mat = [[lax.dot_general(c_gb[s][g], b_gb[s][g], (((1,), (1,)), ((), ())),
                               preferred_element_type=F32) for g in groups] for s in seqs]
    s_prev = [[state[s, :, gcols[g]] for g in groups] for s in seqs]
    y_off = [[jnp.dot(c_gb[s][g], s_prev[s][g].astype(BF16), preferred_element_type=F32)
              for g in groups] for s in seqs]

    heads_per_group = SSD_HEADS // SSD_GROUPS
    decay_mats = [[None] * SSD_HEADS for _ in seqs]
    for s in seqs:
        for hh in range(SSD_HEADS):
            seg = acs[s][:, hh:hh + 1] - acs_t[s][hh:hh + 1, :]
            ldec = jnp.exp(jnp.where(causal, seg, -1e30))
            decay_mats[s][hh] = (cb_mat[s][hh // heads_per_group] * ldec).astype(BF16)

    y_diag = [[jnp.dot(decay_mats[s][hh], xdt_b[s][:, (hh // 2) * BLK:(hh // 2 + 1) * BLK],
                       preferred_element_type=F32) for hh in range(SSD_HEADS)] for s in seqs]
    new = [[jnp.dot(b_t[s][g], xdec[s][:, gcols[g]], preferred_element_type=F32)
            for g in groups] for s in seqs]

    for s in seqs:
        y_cols = []
        for g in groups:
            state[s, :, gcols[g]] = s_prev[s][g] * chunk_decay[s][:, gcols[g]] + new[s][g]
            yo = y_off[s][g] * off_scale[s][:, gcols[g]]
            for pair in range(GROUP_COLS // BLK):
                h0 = (g * GROUP_COLS + pair * BLK) // SSD_HEAD_DIM
                y_cols.append(jnp.where(half, y_diag[s][h0], y_diag[s][h0 + 1])
                              + yo[:, pair * BLK:(pair + 1) * BLK])
        y = jnp.concatenate(y_cols, axis=1) + xs[s] * dexp_ref[...]
        y = y * z_ref[s].astype(F32)
        out_ref[s] = _rms(y, g_ref[...]).astype(BF16)


def _ssd(xbc, z, dt_raw, conv_w, conv_b, dt_bias, a_log, d_exp, norm_g, expand):
    bsz, lp, _ = xbc.shape
    nblk = lp // BLK
    nseq = SSD_SEQS_PER_STEP
    assert bsz % nseq == 0
    blk = lambda b, c: (b, (c + nblk - 1) % nblk, 0)
    return pl.pallas_call(
        functools.partial(_ssd_kernel, nseq=nseq),
        grid=(bsz // nseq, nblk),
        in_specs=[
            pl.BlockSpec((nseq, BLK, XBC_WIDTH), blk),
            pl.BlockSpec((nseq, BLK, SSD_INNER), blk),
            pl.BlockSpec((nseq, BLK, DT_LANES), blk),
            _const_spec((SSD_CONV, XBC_WIDTH)),
            _const_spec((1, XBC_WIDTH)),
            _const_spec((1, DT_LANES)),
            _const_spec((1, DT_LANES)),
            _const_spec((1, SSD_INNER)),
            _const_spec((1, SSD_INNER)),
            _const_spec((DT_LANES, SSD_INNER)),
        ],
        out_specs=pl.BlockSpec((nseq, BLK, SSD_INNER), blk),
        out_shape=jax.ShapeDtypeStruct((bsz, lp, SSD_INNER), BF16),
        scratch_shapes=[
            pltpu.VMEM((nseq, BLK + 8, XBC_WIDTH), F32),
            pltpu.VMEM((nseq, SSD_STATE, SSD_INNER), F32),
        ],
        compiler_params=pltpu.CompilerParams(
            dimension_semantics=("arbitrary", "arbitrary"), vmem_limit_bytes=V7X_VMEM_LIMIT),
        name="ssd",
    )(xbc, z, dt_raw, conv_w, conv_b, dt_bias, a_log, d_exp, norm_g, expand)


SB_DEAD_LOG2 = -151.0
SB_LEAD = 3
SB_TOP_ROWS = 48
FULL, TOP, BOT = slice(0, BLK), slice(0, SB_TOP_ROWS), slice(SB_TOP_ROWS, BLK)


def _sb_kv_copies(hbm_refs, bufs, sems, batch, part, nblk, qpb):
    nparts = nblk // qpb
    seq_rows = (nblk - 1) * BLK
    if part == nparts:
        rows = pl.ds(seq_rows, BLK)
    else:
        start = part * qpb * BLK
        rows = pl.ds(start, min(qpb * BLK, seq_rows - start))
    return [pltpu.make_async_copy(src.at[batch, rows, :], dst.at[rows, :], sems.at[a, part])
            for a, (src, dst) in enumerate(zip(hbm_refs, bufs))]


def _sb_kernel(q_ref, k_hbm, v0_hbm, v1_hbm, tt_ref, o_ref, k_ref, v0_ref, v1_ref, kv_sems,
               qm_scr, run_scr, acc_scr, worst_scr, *, nblk, npairs, qpb):
    batch, step = pl.program_id(0), pl.program_id(1)
    nparts = nblk // qpb
    hbm_refs, bufs = (k_hbm, v0_hbm, v1_hbm), (k_ref, v0_ref, v1_ref)

    @pl.when(step == 0)
    def _():
        for part in [nparts] + list(range(nparts)):
            for cp in _sb_kv_copies(hbm_refs, bufs, kv_sems, batch, part, nblk, qpb):
                cp.start()
        for cp in _sb_kv_copies(hbm_refs, bufs, kv_sems, batch, nparts, nblk, qpb):
            cp.wait()

    for part in range(nparts):
        @pl.when(step == part)
        def _(part=part):
            for cp in _sb_kv_copies(hbm_refs, bufs, kv_sems, batch, part, nblk, qpb):
                cp.wait()

    lane = lax.broadcasted_iota(jnp.int32, (BLK, BLK), 1)
    sub = lax.broadcasted_iota(jnp.int32, (BLK, BLK), 0)
    first = lane < SB_HEAD_DIM
    strictly_older = lane < sub
    tt = tt_ref[...]

    def query_block(r, carry):
        qi = (step * qpb + r + 1) % nblk
        rows = pl.ds(pl.multiple_of(r * BLK, BLK), BLK)
        for p in range(npairs):
            qp = q_ref[0, rows, p * BLK:(p + 1) * BLK]
            zero = jnp.zeros_like(qp)
            qm_scr[p, :BLK, :] = jnp.where(first, qp, zero)
            qm_scr[p, BLK:, :] = jnp.where(first, zero, qp)

        def process(blocks, init, check, row_sets=None):
            offs = [pl.multiple_of(((j + nblk - 1) % nblk) * BLK, BLK) for j in blocks]
            nb = range(len(blocks))
            row_sets = row_sets or [FULL] * len(blocks)
            heads = range(2 * npairs)
            scores = {}
            for b in nb:
                rs = row_sets[b]
                for p in range(npairs):
                    if rs is FULL:
                        lhs = qm_scr[p]
                    else:
                        lhs = jnp.concatenate([qm_scr[p, rs, :],
                                               qm_scr[p, BLK + rs.start:BLK + rs.stop, :]], axis=0)
                    scores[b, p] = lax.dot_general(
                        lhs, k_ref[pl.ds(offs[b], BLK), p * BLK:(p + 1) * BLK],
                        (((1,), (1,)), ((), ())), preferred_element_type=F32)
            log_betas, log_keeps, totals = {}, {}, {}
            for b in nb:
                n = row_sets[b].stop - row_sets[b].start
                for h in heads:
                    s = scores[b, h // 2][(h % 2) * n:(h % 2 + 1) * n, :]
                    neg_part = jnp.minimum(s, 0.0)
                    neg_pos = neg_part - s
                    sp = jnp.log(1.0 + jnp.exp2(neg_part + neg_pos)) * LOG2E
                    log_beta = neg_part - sp
                    log_keep = neg_pos - sp
                    if init and b == 0:
                        log_keep = jnp.where(strictly_older, log_keep, 0.0)
                    log_betas[b, h] = log_beta
                    log_keeps[b, h] = log_keep.astype(BF16)
                    totals[b, h] = jnp.sum(log_keep, axis=1, keepdims=True)
            sums = [jnp.dot(jnp.concatenate([log_keeps[b, h] for h in heads], axis=0), tt,
                            preferred_element_type=F32) for b in nb]
            weights = {}
            worst = None
            for h in heads:
                run = None if init else run_scr[h]
                for b in nb:
                    rs = row_sets[b]
                    n = rs.stop - rs.start
                    after, total = sums[b][h * n:(h + 1) * n, :], totals[b, h]
                    if run is None:
                        w = jnp.where(strictly_older, jnp.exp2(log_betas[b, h] + after), 0.0)
                        run = jnp.broadcast_to(total, (BLK, BLK))
                    else:
                        part = run if rs is FULL else run[rs, :]
                        w = jnp.exp2(log_betas[b, h] + (after + part))
                        part = part + total
                        if rs is FULL:
                            run = part
                        elif rs is TOP:
                            run = jnp.concatenate([part, run[BOT, :]], axis=0)
                        else:
                            run = jnp.concatenate([run[TOP, :], part], axis=0)
                    weights[b, h] = w.astype(BF16)
                run_scr[h] = run
                if check:
                    worst = run if worst is None else jnp.maximum(worst, run)
            for p in range(npairs):
                pv = None if init else acc_scr[p]
                for b in nb:
                    rs = row_sets[b]
                    v_heads = jnp.concatenate(
                        [v0_ref[pl.ds(offs[b], BLK), p * BLK:(p + 1) * BLK],
                         v1_ref[pl.ds(offs[b], BLK), p * BLK:(p + 1) * BLK]], axis=0)
                    w_heads = jnp.concatenate([weights[b, 2 * p], weights[b, 2 * p + 1]], axis=1)
                    contrib = jnp.dot(w_heads, v_heads, preferred_element_type=F32)
                    if pv is None:
                        pv = contrib
                    elif rs is FULL:
                        pv = pv + contrib
                    elif rs is TOP:
                        pv = jnp.concatenate([pv[TOP, :] + contrib, pv[BOT, :]], axis=0)
                    else:
                        pv = jnp.concatenate([pv[TOP, :], pv[BOT, :] + contrib], axis=0)
                acc_scr[p] = pv
            if check:
                worst_scr[0] = jnp.max(worst[TOP, :])
                worst_scr[1] = jnp.max(worst[BOT, :])

        for lead in range(1, SB_LEAD + 1):
            last = lead == SB_LEAD

            @pl.when(qi >= lead - 1 if last else qi == lead - 1)
            def _(lead=lead, last=last):
                if last:
                    process([qi - i for i in range(lead)], True, True, [FULL] * (lead - 1) + [TOP])
                else:
                    process([qi - i for i in range(lead)], True, False)
                    worst_scr[0] = jnp.float32(SB_DEAD_LOG2)
                    worst_scr[1] = jnp.float32(SB_DEAD_LOG2)

        @pl.when(jnp.logical_and(qi >= SB_LEAD - 1, worst_scr[1] > SB_DEAD_LOG2))
        def _():
            process([qi - (SB_LEAD - 1)], False, True, [BOT])

        def cond(st):
            return jnp.logical_and(st[0] >= 0, st[1] > SB_DEAD_LOG2)

        def body(st):
            process([st[0]], False, True)
            return st[0] - 1, jnp.maximum(worst_scr[0], worst_scr[1])

        lax.while_loop(cond, body, (qi - SB_LEAD, jnp.maximum(worst_scr[0], worst_scr[1])))
        for p in range(npairs):
            o_ref[0, rows, p * BLK:(p + 1) * BLK] = acc_scr[p].astype(BF16)
        return carry

    lax.fori_loop(0, qpb, query_block, 0)


def _sb_attention(q, k, v0, v1, tt):
    bsz, lp, width = q.shape
    nblk = lp // BLK
    qpb = SB_Q_BLOCKS_PER_STEP
    assert nblk % qpb == 0
    npairs = width // BLK
    nparts = nblk // qpb
    qblk = lambda b, g: (b, g, 0)
    kv_hbm = pl.BlockSpec(memory_space=pl.ANY)
    return pl.pallas_call(
        functools.partial(_sb_kernel, nblk=nblk, npairs=npairs, qpb=qpb),
        grid=(bsz, nparts),
        in_specs=[
            pl.BlockSpec((1, qpb * BLK, width), qblk),
            kv_hbm,
            kv_hbm,
            kv_hbm,
            _const_spec((BLK, BLK)),
        ],
        out_specs=pl.BlockSpec((1, qpb * BLK, width), qblk),
        out_shape=jax.ShapeDtypeStruct((bsz, lp, width), BF16),
        scratch_shapes=[
            pltpu.VMEM((lp, width), BF16),
            pltpu.VMEM((lp, width), BF16),
            pltpu.VMEM((lp, width), BF16),
            pltpu.SemaphoreType.DMA((3, nparts + 1)),
            pltpu.VMEM((npairs, 2 * BLK, BLK), BF16),
            pltpu.VMEM((2 * npairs, BLK, BLK), F32),
            pltpu.VMEM((npairs, BLK, BLK), F32),
            pltpu.SMEM((2,), F32),
        ],
        compiler_params=pltpu.CompilerParams(
            dimension_semantics=("arbitrary", "arbitrary"), vmem_limit_bytes=V7X_VMEM_LIMIT),
        name="sb_attention",
    )(q, k, v0, v1, tt)


def _outproj_rows(ys, o, h, wa_ref, wb_ref, gsb_ref, gpost_ref, gffn_ref):
    osb = _rms(o.astype(F32), gsb_ref[...]).astype(BF16)
    mix = jnp.dot(ys, wa_ref[...], preferred_element_type=F32)
    mix = mix + jnp.dot(osb, wb_ref[...], preferred_element_type=F32)
    h1 = h + _rms(mix, gpost_ref[...])
    return h1, _rms(h1, gffn_ref[...]).astype(BF16)


def _outproj_kernel(ys_ref, o_ref, h_ref, wa_ref, wb_ref, gsb_ref, gpost_ref, gffn_ref,
                    h1_ref, xn2_ref, *, nsub):
    sub = h_ref.shape[1] // nsub
    rows = [pl.ds(k * sub, sub) for k in range(nsub)]
    osb = [_rms(o_ref[0, r, :].astype(F32), gsb_ref[...]).astype(BF16) for r in rows]
    mix = [jnp.dot(ys_ref[0, r, :], wa_ref[...], preferred_element_type=F32)
           + jnp.dot(osb[k], wb_ref[...], preferred_element_type=F32) for k, r in enumerate(rows)]
    for k, r in enumerate(rows):
        h1 = h_ref[0, r, :] + _rms(mix[k], gpost_ref[...])
        h1_ref[0, r, :] = h1
        xn2_ref[0, r, :] = _rms(h1, gffn_ref[...]).astype(BF16)


def _outproj_meta_kernel(ys_ref, o_ref, hm_ref, wa_ref, wb_ref, gsb_ref, gpost_ref, gffn_ref, xn2_ref):
    _, xn2 = _outproj_rows(ys_ref[0], o_ref[0], hm_ref[...], wa_ref, wb_ref, gsb_ref, gpost_ref, gffn_ref)
    xn2_ref[0] = xn2


def _outproj(y_ssd, o_sb, x, hm, w_a, w_b, g_sb, g_post, g_ffn):
    bsz, seq, _ = x.shape
    lp = y_ssd.shape[1]
    tm = OUTPROJ_TM
    per_seq = seq // tm
    weights = [
        _const_spec((SSD_INNER, D_MODEL)),
        _const_spec((SB_WIDTH, D_MODEL)),
        _const_spec((1, SB_WIDTH)),
        _const_spec((1, D_MODEL)),
        _const_spec((1, D_MODEL)),
    ]
    params = pltpu.CompilerParams(dimension_semantics=("arbitrary",), vmem_limit_bytes=V7X_VMEM_LIMIT)
    tile = lambda t: (t // per_seq, t % per_seq, 0)
    h1, xn2 = pl.pallas_call(
        functools.partial(_outproj_kernel, nsub=OUTPROJ_SUBTILES),
        grid=(bsz * per_seq,),
        in_specs=[
            pl.BlockSpec((1, tm, SSD_INNER), tile),
            pl.BlockSpec((1, tm, SB_WIDTH), tile),
            pl.BlockSpec((1, tm, D_MODEL), tile),
        ] + weights,
        out_specs=[pl.BlockSpec((1, tm, D_MODEL), tile), pl.BlockSpec((1, tm, D_MODEL), tile)],
        out_shape=[jax.ShapeDtypeStruct((bsz, seq, D_MODEL), F32),
                   jax.ShapeDtypeStruct((bsz, seq, D_MODEL), BF16)],
        compiler_params=params,
        name="outproj",
    )(y_ssd, o_sb, x, w_a, w_b, g_sb, g_post, g_ffn)

    last_blk = lambda b: (0, lp // BLK - 1, 0)
    xn2_meta = pl.pallas_call(
        _outproj_meta_kernel,
        grid=(1,),
        in_specs=[
            pl.BlockSpec((1, BLK, SSD_INNER), last_blk),
            pl.BlockSpec((1, BLK, SB_WIDTH), last_blk),
            _const_spec((BLK, D_MODEL)),
        ] + weights,
        out_specs=pl.BlockSpec((1, BLK, D_MODEL), lambda b: (0, 0, 0)),
        out_shape=jax.ShapeDtypeStruct((1, BLK, D_MODEL), BF16),
        compiler_params=params,
        name="outproj_meta",
    )(y_ssd, o_sb, hm, w_a, w_b, g_sb, g_post, g_ffn)
    return h1, xn2, xn2_meta


def _gelu_tanh(x):
    return 0.5 * x * (1.0 + jnp.tanh(math.sqrt(2.0 / math.pi) * (x + 0.044715 * (x * x * x))))


def _ffn_kernel(xn_ref, xm_ref, h1_ref, wup_ref, cw_ref, cb_ref, wd_ref, g_ref, out_ref,
                gbuf0, gbuf1, ubuf0, ubuf1, halo, act_scr, *, tm, tf):
    nch = D_FF // tf
    i = pl.program_id(1)
    gbufs, ubufs = (gbuf0, gbuf1), (ubuf0, ubuf1)
    cols = [slice(c * tf, (c + 1) * tf) for c in range(nch)]
    up_cols = [slice(D_FF + c * tf, D_FF + (c + 1) * tf) for c in range(nch)]

    @pl.when(i == 0)
    def _():
        for c in range(nch):
            gm = jnp.dot(xm_ref[0], wup_ref[:, cols[c]], preferred_element_type=F32)
            halo[c] = gm[N_META - 8:, :]

    def project(c, slot):
        gbuf, ubuf = gbufs[slot], ubufs[slot]
        xn = xn_ref[0]
        gate = jnp.dot(xn, wup_ref[:, cols[c]], preferred_element_type=F32)
        gbuf[0:8, :] = halo[c]
        gbuf[8:8 + tm, :] = gate
        halo[c] = gate[tm - 8:, :]
        ubuf[...] = jnp.dot(xn, wup_ref[:, up_cols[c]], preferred_element_type=F32)

    def activate(c, slot):
        gbuf, ubuf = gbufs[slot], ubufs[slot]
        cw = cw_ref[:, cols[c]]
        conv = cb_ref[:, cols[c]] + cw[2:3, :] * gbuf[8:8 + tm, :]
        conv = conv + cw[1:2, :] * gbuf[7:7 + tm, :] + cw[0:1, :] * gbuf[6:6 + tm, :]
        act_scr[c] = (_gelu_tanh(conv) * ubuf[...]).astype(BF16)

    project(0, 0)
    for c in range(nch - 1):
        project(c + 1, (c + 1) % 2)
        activate(c, c % 2)
    activate(nch - 1, (nch - 1) % 2)
    act = jnp.concatenate([act_scr[c] for c in range(nch)], axis=1)
    down = jnp.dot(act, wd_ref[...], preferred_element_type=F32)
    out_ref[0] = h1_ref[0] + _rms(down, g_ref[...])


def _ffn(xn2, xn2_meta, h1, w_up, conv_w, conv_b, w_down, g_post):
    bsz, seq, _ = xn2.shape
    tm, tf = FFN_TM, FFN_TF
    nch = D_FF // tf
    xblk = lambda b, i: (b, i, 0)
    meta = lambda b, i: (0, BLK // N_META - 1, 0)
    return pl.pallas_call(
        functools.partial(_ffn_kernel, tm=tm, tf=tf),
        grid=(bsz, seq // tm),
        in_specs=[
            pl.BlockSpec((1, tm, D_MODEL), xblk),
            pl.BlockSpec((1, N_META, D_MODEL), meta),
            pl.BlockSpec((1, tm, D_MODEL), xblk),
            _const_spec((D_MODEL, 2 * D_FF)),
            _const_spec((FFN_CONV, D_FF)),
            _const_spec((1, D_FF)),
            _const_spec((D_FF, D_MODEL)),
            _const_spec((1, D_MODEL)),
        ],
        out_specs=pl.BlockSpec((1, tm, D_MODEL), xblk),
        out_shape=jax.ShapeDtypeStruct((bsz, seq, D_MODEL), F32),
        scratch_shapes=[
            pltpu.VMEM((tm + 8, tf), F32),
            pltpu.VMEM((tm + 8, tf), F32),
            pltpu.VMEM((tm, tf), F32),
            pltpu.VMEM((tm, tf), F32),
            pltpu.VMEM((nch, 8, tf), F32),
            pltpu.VMEM((nch, tm, tf), BF16),
        ],
        compiler_params=pltpu.CompilerParams(
            dimension_semantics=("arbitrary", "arbitrary"), vmem_limit_bytes=V7X_VMEM_LIMIT),
        name="ffn",
    )(xn2, xn2_meta, h1, w_up, conv_w, conv_b, w_down, g_post)


def _pad_lanes(v, n):
    return jnp.pad(v, ((0, 0), (0, n - v.shape[-1])))


def kernel(x, meta_tokens, mix_pre_g, w_in, ssd_conv_w, ssd_conv_b, ssd_dt_bias, ssd_a_log, ssd_d,
           ssd_norm_g, sb_norm_g, w_out, mix_post_g, ffn_pre_g, w_up, ffn_conv_w, ffn_conv_b,
           w_down, ffn_post_g):
    bsz, seq, d = x.shape
    depth = w_in.shape[0]
    assert depth == 1 and d == D_MODEL and seq % OUTPROJ_TM == 0

    hm = jnp.concatenate([jnp.zeros((PAD, d), x.dtype), meta_tokens.astype(x.dtype)], axis=0)

    l = 0
    off_xbc = SSD_INNER
    off_dt = off_xbc + XBC_WIDTH
    off_q = off_dt + SSD_HEADS
    w = w_in[l]
    w_main = jnp.concatenate([w[:, :off_dt], w[:, off_q:]], axis=1).astype(BF16)
    w_dt = _pad_lanes(w[:, off_dt:off_q], DT_LANES).astype(BF16)
    z, xbc, q, k, v0, v1, dt_raw = _inproj(x, hm, mix_pre_g[l][None], w_main, w_dt)

    head_of_col = jnp.arange(SSD_INNER) // SSD_HEAD_DIM
    expand = (jnp.arange(DT_LANES)[:, None] == head_of_col[None, :]).astype(BF16)
    d_exp = jnp.repeat(ssd_d[l].astype(F32), SSD_HEAD_DIM)[None]
    y_ssd = _ssd(xbc, z, dt_raw, ssd_conv_w[l], ssd_conv_b[l][None],
                 _pad_lanes(ssd_dt_bias[l][None], DT_LANES), _pad_lanes(ssd_a_log[l][None], DT_LANES),
                 d_exp, ssd_norm_g[l][None], expand)

    kk = jnp.arange(BLK)
    tt = (kk[:, None] > kk[None, :]).astype(BF16)
    o_sb = _sb_attention(q, k, v0, v1, tt)

    wo = w_out[l].astype(BF16)
    h1, xn2, xn2_meta = _outproj(y_ssd, o_sb, x, hm, wo[:SSD_INNER], wo[SSD_INNER:], sb_norm_g[l][None],
                                 mix_post_g[l][None], ffn_pre_g[l][None])

    return _ffn(xn2, xn2_meta, h1, w_up[l].astype(BF16), ffn_conv_w[l], ffn_conv_b[l][None],
                w_down[l].astype(BF16), ffn_post_g[l][None])
```

```python
import functools
import math

import jax
import jax.numpy as jnp
from jax import lax
from jax.experimental import pallas as pl
from jax.experimental.pallas import tpu as pltpu

F32 = jnp.float32
BF16 = jnp.bfloat16

D_MODEL = 1024
N_META = 16
BLK = 128
PAD = BLK - N_META
SSD_HEADS = 16
SSD_HEAD_DIM = 64
SSD_GROUPS = 2
SSD_STATE = 128
SSD_INNER = 1024
SSD_CONV = 4
XBC_WIDTH = SSD_INNER + 2 * SSD_GROUPS * SSD_STATE
SB_WIDTH = 1024
SB_HEAD_DIM = 64
D_FF = 2816
FFN_CONV = 3
EPS = 1e-6
DT_LANES = 128
GROUP_COLS = SSD_INNER // SSD_GROUPS
LOG2E = 1.4426950408889634

V7X_VMEM_LIMIT = 56 * 1024 * 1024

INPROJ_TM = 1024
OUTPROJ_TM = 1024
OUTPROJ_SUBTILES = 8
FFN_TM = 1024
FFN_TF = 256
SSD_SEQS_PER_STEP = 2
SB_Q_BLOCKS_PER_STEP = 3


def _rms(x, g):
    return x * lax.rsqrt(jnp.mean(x * x, axis=-1, keepdims=True) + EPS) * g


def _const_spec(shape):
    nd = len(shape)
    return pl.BlockSpec(shape, lambda *_: (0,) * nd, pipeline_mode=pl.Buffered(1))


_PROJ_SPLITS = (("z", SSD_INNER), ("xbc", XBC_WIDTH), ("q", SB_WIDTH), ("k", SB_WIDTH), ("v", SB_WIDTH))
_INPROJ_OUT_WIDTHS = (SSD_INNER, XBC_WIDTH, SB_WIDTH, SB_WIDTH, SB_WIDTH, SB_WIDTH, DT_LANES)
_INPROJ_OUT_DTYPES = (BF16, BF16, BF16, BF16, BF16, BF16, F32)


def _inproj_rows(h, g_ref, w_ref, wdt_ref, out_refs):
    z_ref, xbc_ref, q_ref, k_ref, v0_ref, v1_ref, dt_ref = out_refs
    xn = _rms(h, g_ref[...]).astype(BF16)
    lo = 0
    proj = {}
    for name, width in _PROJ_SPLITS:
        proj[name] = jnp.dot(xn, w_ref[:, lo:lo + width], preferred_element_type=F32)
        lo += width
    z_ref[0] = (proj["z"] * jax.nn.sigmoid(proj["z"])).astype(BF16)
    xbc_ref[0] = proj["xbc"].astype(BF16)
    q_ref[0] = (proj["q"] * (LOG2E / math.sqrt(SB_HEAD_DIM))).astype(BF16)
    k_ref[0] = proj["k"].astype(BF16)
    v = proj["v"].astype(BF16)
    lane = lax.broadcasted_iota(jnp.int32, v.shape, 1)
    even_head = (lane & SB_HEAD_DIM) == 0
    zero = jnp.zeros_like(v)
    v0_ref[0] = jnp.where(even_head, v, zero)
    v1_ref[0] = jnp.where(even_head, zero, v)
    dt_ref[0] = jnp.dot(xn, wdt_ref[...], preferred_element_type=F32)


def _inproj_kernel(x_ref, g_ref, w_ref, wdt_ref, *out_refs):
    _inproj_rows(x_ref[...], g_ref, w_ref, wdt_ref, out_refs)


def _inproj_meta_kernel(hm_ref, g_ref, w_ref, wdt_ref, *refs):
    nout = len(_INPROJ_OUT_WIDTHS)
    out_refs, scratch = refs[nout:2 * nout], refs[2 * nout:]

    @pl.when(pl.program_id(0) == 0)
    def _():
        _inproj_rows(hm_ref[...], g_ref, w_ref, wdt_ref, scratch)

    for out_ref, scr in zip(out_refs, scratch):
        out_ref[...] = scr[...]


def _inproj(x, hm, g, w_main, w_dt):
    bsz, seq, _ = x.shape
    lp = seq + BLK
    tm = INPROJ_TM
    per_seq = seq // tm
    n_main = w_main.shape[1]
    out_shape = [jax.ShapeDtypeStruct((bsz, lp, width), dt)
                 for width, dt in zip(_INPROJ_OUT_WIDTHS, _INPROJ_OUT_DTYPES)]
    weights = [_const_spec((1, D_MODEL)), _const_spec((D_MODEL, n_main)), _const_spec((D_MODEL, DT_LANES))]
    params = pltpu.CompilerParams(dimension_semantics=("arbitrary",), vmem_limit_bytes=V7X_VMEM_LIMIT)

    tile = lambda t: (t // per_seq, t % per_seq, 0)
    outs = pl.pallas_call(
        _inproj_kernel,
        grid=(bsz * per_seq,),
        in_specs=[pl.BlockSpec((tm, D_MODEL), lambda t: (t, 0))] + weights,
        out_specs=[pl.BlockSpec((1, tm, width), tile) for width in _INPROJ_OUT_WIDTHS],
        out_shape=out_shape,
        compiler_params=params,
        name="inproj",
    )(x.reshape(bsz * seq, D_MODEL), g, w_main, w_dt)

    nout = len(outs)
    last_blk = lambda b: (b, lp // BLK - 1, 0)
    return pl.pallas_call(
        _inproj_meta_kernel,
        grid=(bsz,),
        in_specs=[_const_spec((BLK, D_MODEL))] + weights + [pl.BlockSpec(memory_space=pl.ANY)] * nout,
        out_specs=[pl.BlockSpec((1, BLK, width), last_blk) for width in _INPROJ_OUT_WIDTHS],
        out_shape=out_shape,
        input_output_aliases={4 + j: j for j in range(nout)},
        scratch_shapes=[pltpu.VMEM((1, BLK, width), dt)
                        for width, dt in zip(_INPROJ_OUT_WIDTHS, _INPROJ_OUT_DTYPES)],
        compiler_params=params,
        name="inproj_meta",
    )(hm, g, w_main, w_dt, *outs)


def _dot_f32_by_01(lhs, rhs, f32_side):
    x = lhs if f32_side == "lhs" else rhs
    p1 = x.astype(BF16)
    r1 = x - p1.astype(F32)
    p2 = r1.astype(BF16)
    p3 = (r1 - p2.astype(F32)).astype(BF16)
    out = None
    for piece in (p1, p2, p3):
        ops = (piece, rhs) if f32_side == "lhs" else (lhs, piece)
        term = jnp.dot(*ops, preferred_element_type=F32)
        out = term if out is None else out + term
    return out


def _ssd_kernel(xbc_ref, z_ref, dt_ref, cw_ref, cb_ref, dtb_ref, alog_ref, dexp_ref, g_ref, e_ref,
                out_ref, prev, state, *, nseq):
    c = pl.program_id(1)

    @pl.when(c == 0)
    def _():
        prev[...] = jnp.zeros_like(prev)
        state[...] = jnp.zeros_like(state)

    seqs = range(nseq)
    groups = range(SSD_GROUPS)
    row = lax.broadcasted_iota(jnp.int32, (BLK, 1), 0)
    real = jnp.logical_or(c > 0, row >= PAD)
    lane = lax.broadcasted_iota(jnp.int32, (BLK, DT_LANES), 1)
    dt_valid = jnp.logical_and(real, lane < SSD_HEADS)
    ri = lax.broadcasted_iota(jnp.int32, (BLK, BLK), 0)
    ci = lax.broadcasted_iota(jnp.int32, (BLK, BLK), 1)
    causal = ri >= ci
    tri = jnp.where(causal, 1.0, 0.0).astype(BF16)
    half = ci < SSD_HEAD_DIM
    cw = cw_ref[...]
    e = e_ref[...]
    a_neg = -jnp.exp(alog_ref[...]) * LOG2E

    xc, dt, adt = [], [], []
    for s in seqs:
        x_raw = xbc_ref[s].astype(F32)
        prev[s, 8:8 + BLK, :] = x_raw
        acc = cb_ref[...] + cw[3:4, :] * x_raw
        for tap in range(SSD_CONV - 1):
            acc = acc + cw[tap:tap + 1, :] * prev[s, 5 + tap:5 + tap + BLK, :]
        prev[s, 0:8, :] = x_raw[BLK - 8:, :]
        xc.append(jnp.where(real, acc * jax.nn.sigmoid(acc), 0.0))
        dts = jax.nn.softplus(dt_ref[s] + dtb_ref[...])
        dt.append(jnp.where(dt_valid, dts, 0.0))
        adt.append(dt[s] * a_neg)
    xs = [xc[s][:, :SSD_INNER] for s in seqs]

    acs = [_dot_f32_by_01(tri, adt[s], f32_side="rhs") for s in seqs]
    acs_t = [acs[s].T for s in seqs]
    dt_exp = [_dot_f32_by_01(dt[s], e, f32_side="lhs") for s in seqs]
    acs_exp = [_dot_f32_by_01(acs[s], e, f32_side="lhs") for s in seqs]

    xdt_b, xdec, off_scale, chunk_decay, b_t, c_gb, b_gb = [], [], [], [], [], [], []
    for s in seqs:
        last = acs_exp[s][BLK - 1:BLK, :]
        xdt = xs[s] * dt_exp[s]
        xdec.append((xdt * jnp.exp2(last - acs_exp[s])).astype(BF16))
        xdt_b.append(xdt.astype(BF16))
        off_scale.append(jnp.exp2(acs_exp[s]))
        chunk_decay.append(jnp.exp2(last))
        b_g = [xc[s][:, SSD_INNER + g * SSD_STATE:SSD_INNER + (g + 1) * SSD_STATE] for g in groups]
        c_g = [xc[s][:, SSD_INNER + (SSD_GROUPS + g) * SSD_STATE:
                     SSD_INNER + (SSD_GROUPS + g + 1) * SSD_STATE] for g in groups]
        b_gb.append([x.astype(BF16) for x in b_g])
        c_gb.append([x.astype(BF16) for x in c_g])
        b_t.append([x.T.astype(BF16) for x in b_g])

    gcols = [slice(g * GROUP_COLS, (g + 1) * GROUP_COLS) for g in groups]
    cb_mat = [[lax.dot_general(c_gb[s][g], b_gb[s][g], (((1,), (1,)), ((), ())),
                               preferred_element_type=F32) for g in groups] for s in seqs]
    s_prev = [[state[s, :, gcols[g]] for g in groups] for s in seqs]
    y_off = [[jnp.dot(c_gb[s][g], s_prev[s][g].astype(BF16), preferred_element_type=F32)
              for g in groups] for s in seqs]

    heads_per_group = SSD_HEADS // SSD_GROUPS
    decay_mats = [[None] * SSD_HEADS for _ in seqs]
    for s in seqs:
        for hh in range(SSD_HEADS):
            seg = acs[s][:, hh:hh + 1] - acs_t[s][hh:hh + 1, :]
            ldec = jnp.exp2(jnp.where(causal, seg, -1e30))
            decay_mats[s][hh] = (cb_mat[s][hh // heads_per_group] * ldec).astype(BF16)

    y_diag = [[jnp.dot(decay_mats[s][hh], xdt_b[s][:, (hh // 2) * BLK:(hh // 2 + 1) * BLK],
                       preferred_element_type=F32) for hh in range(SSD_HEADS)] for s in seqs]
    new = [[jnp.dot(b_t[s][g], xdec[s][:, gcols[g]], preferred_element_type=F32)
            for g in groups] for s in seqs]

    for s in seqs:
        y_cols = []
        for g in groups:
            state[s, :, gcols[g]] = s_prev[s][g] * chunk_decay[s][:, gcols[g]] + new[s][g]
            yo = y_off[s][g] * off_scale[s][:, gcols[g]]
            for pair in range(GROUP_COLS // BLK):
                h0 = (g * GROUP_COLS + pair * BLK) // SSD_HEAD_DIM
                y_cols.append(jnp.where(half, y_diag[s][h0], y_diag[s][h0 + 1])
                              + yo[:, pair * BLK:(pair + 1) * BLK])
        y = jnp.concatenate(y_cols, axis=1) + xs[s] * dexp_ref[...]
        y = y * z_ref[s].astype(F32)
        out_ref[s] = _rms(y, g_ref[...]).astype(BF16)


def _ssd(xbc, z, dt_raw, conv_w, conv_b, dt_bias, a_log, d_exp, norm_g, expand):
    bsz, lp, _ = xbc.shape
    nblk = lp // BLK
    nseq = SSD_SEQS_PER_STEP
    assert bsz % nseq == 0
    blk = lambda b, c: (b, (c + nblk - 1) % nblk, 0)
    return pl.pallas_call(
        functools.partial(_ssd_kernel, nseq=nseq),
        grid=(bsz // nseq, nblk),
        in_specs=[
            pl.BlockSpec((nseq, BLK, XBC_WIDTH), blk),
            pl.BlockSpec((nseq, BLK, SSD_INNER), blk),
            pl.BlockSpec((nseq, BLK, DT_LANES), blk),
            _const_spec((SSD_CONV, XBC_WIDTH)),
            _const_spec((1, XBC_WIDTH)),
            _const_spec((1, DT_LANES)),
            _const_spec((1, DT_LANES)),
            _const_spec((1, SSD_INNER)),
            _const_spec((1, SSD_INNER)),
            _const_spec((DT_LANES, SSD_INNER)),
        ],
        out_specs=pl.BlockSpec((nseq, BLK, SSD_INNER), blk),
        out_shape=jax.ShapeDtypeStruct((bsz, lp, SSD_INNER), BF16),
        scratch_shapes=[
            pltpu.VMEM((nseq, BLK + 8, XBC_WIDTH), F32),
            pltpu.VMEM((nseq, SSD_STATE, SSD_INNER), F32),
        ],
        compiler_params=pltpu.CompilerParams(
            dimension_semantics=("arbitrary", "arbitrary"), vmem_limit_bytes=V7X_VMEM_LIMIT),
        name="ssd",
    )(xbc, z, dt_raw, conv_w, conv_b, dt_bias, a_log, d_exp, norm_g, expand)


SB_DEAD_LOG2 = -151.0
SB_LEAD = 3
SB_TOP_ROWS = 48
FULL, TOP, BOT = slice(0, BLK), slice(0, SB_TOP_ROWS), slice(SB_TOP_ROWS, BLK)


def _sb_kv_copies(hbm_refs, bufs, sems, batch, part, nblk, qpb):
    nparts = nblk // qpb
    seq_rows = (nblk - 1) * BLK
    if part == nparts:
        rows = pl.ds(seq_rows, BLK)
    else:
        start = part * qpb * BLK
        rows = pl.ds(start, min(qpb * BLK, seq_rows - start))
    return [pltpu.make_async_copy(src.at[batch, rows, :], dst.at[rows, :], sems.at[a, part])
            for a, (src, dst) in enumerate(zip(hbm_refs, bufs))]


def _sb_kernel(q_ref, k_hbm, v0_hbm, v1_hbm, tt_ref, o_ref, k_ref, v0_ref, v1_ref, kv_sems,
               qm_scr, run_scr, acc_scr, worst_scr, *, nblk, npairs, qpb):
    batch, step = pl.program_id(0), pl.program_id(1)
    nparts = nblk // qpb
    hbm_refs, bufs = (k_hbm, v0_hbm, v1_hbm), (k_ref, v0_ref, v1_ref)

    @pl.when(step == 0)
    def _():
        for part in [nparts] + list(range(nparts)):
            for cp in _sb_kv_copies(hbm_refs, bufs, kv_sems, batch, part, nblk, qpb):
                cp.start()
        for cp in _sb_kv_copies(hbm_refs, bufs, kv_sems, batch, nparts, nblk, qpb):
            cp.wait()

    for part in range(nparts):
        @pl.when(step == part)
        def _(part=part):
            for cp in _sb_kv_copies(hbm_refs, bufs, kv_sems, batch, part, nblk, qpb):
                cp.wait()

    lane = lax.broadcasted_iota(jnp.int32, (BLK, BLK), 1)
    sub = lax.broadcasted_iota(jnp.int32, (BLK, BLK), 0)
    first = lane < SB_HEAD_DIM
    strictly_older = lane < sub
    tt = tt_ref[...]

    def query_block(r, carry):
        qi = (step * qpb + r + 1) % nblk
        rows = pl.ds(pl.multiple_of(r * BLK, BLK), BLK)
        for p in range(npairs):
            qp = q_ref[0, rows, p * BLK:(p + 1) * BLK]
            zero = jnp.zeros_like(qp)
            qm_scr[p, :BLK, :] = jnp.where(first, qp, zero)
            qm_scr[p, BLK:, :] = jnp.where(first, zero, qp)

        def process(blocks, init, check, row_sets=None):
            offs = [pl.multiple_of(((j + nblk - 1) % nblk) * BLK, BLK) for j in blocks]
            nb = range(len(blocks))
            row_sets = row_sets or [FULL] * len(blocks)
            heads = range(2 * npairs)
            scores = {}
            for b in nb:
                rs = row_sets[b]
                for p in range(npairs):
                    if rs is FULL:
                        lhs = qm_scr[p]
                    else:
                        lhs = jnp.concatenate([qm_scr[p, rs, :],
                                               qm_scr[p, BLK + rs.start:BLK + rs.stop, :]], axis=0)
                    scores[b, p] = lax.dot_general(
                        lhs, k_ref[pl.ds(offs[b], BLK), p * BLK:(p + 1) * BLK],
                        (((1,), (1,)), ((), ())), preferred_element_type=F32)
            log_betas, log_keeps, totals = {}, {}, {}
            for b in nb:
                n = row_sets[b].stop - row_sets[b].start
                for h in heads:
                    s = scores[b, h // 2][(h % 2) * n:(h % 2 + 1) * n, :]
                    neg_part = jnp.minimum(s, 0.0)
                    neg_pos = neg_part - s
                    sp = jnp.log(1.0 + jnp.exp2(neg_part + neg_pos)) * LOG2E
                    log_beta = neg_part - sp
                    log_keep = neg_pos - sp
                    if init and b == 0:
                        log_keep = jnp.where(strictly_older, log_keep, 0.0)
                    log_betas[b, h] = log_beta
                    log_keeps[b, h] = log_keep.astype(BF16)
                    totals[b, h] = jnp.sum(log_keep, axis=1, keepdims=True)
            sums = [jnp.dot(jnp.concatenate([log_keeps[b, h] for h in heads], axis=0), tt,
                            preferred_element_type=F32) for b in nb]
            weights = {}
            worst = None
            for h in heads:
                run = None if init else run_scr[h]
                for b in nb:
                    rs = row_sets[b]
                    n = rs.stop - rs.start
                    after, total = sums[b][h * n:(h + 1) * n, :], totals[b, h]
                    if run is None:
                        w = jnp.where(strictly_older, jnp.exp2(log_betas[b, h] + after), 0.0)
                        run = jnp.broadcast_to(total, (BLK, BLK))
                    else:
                        part = run if rs is FULL else run[rs, :]
                        w = jnp.exp2(log_betas[b, h] + (after + part))
                        part = part + total
                        if rs is FULL:
                            run = part
                        elif rs is TOP:
                            run = jnp.concatenate([part, run[BOT, :]], axis=0)
                        else:
                            run = jnp.concatenate([run[TOP, :], part], axis=0)
                    weights[b, h] = w.astype(BF16)
                run_scr[h] = run
                if check:
                    worst = run if worst is None else jnp.maximum(worst, run)
            for p in range(npairs):
                pv = None if init else acc_scr[p]
                for b in nb:
                    rs = row_sets[b]
                    v_heads = jnp.concatenate(
                        [v0_ref[pl.ds(offs[b], BLK), p * BLK:(p + 1) * BLK],
                         v1_ref[pl.ds(offs[b], BLK), p * BLK:(p + 1) * BLK]], axis=0)
                    w_heads = jnp.concatenate([weights[b, 2 * p], weights[b, 2 * p + 1]], axis=1)
                    contrib = jnp.dot(w_heads, v_heads, preferred_element_type=F32)
                    if pv is None:
                        pv = contrib
                    elif rs is FULL:
                        pv = pv + contrib
                    elif rs is TOP:
                        pv = jnp.concatenate([pv[TOP, :] + contrib, pv[BOT, :]], axis=0)
                    else:
                        pv = jnp.concatenate([pv[TOP, :], pv[BOT, :] + contrib], axis=0)
                acc_scr[p] = pv
            if check:
                worst_scr[0] = jnp.max(worst[TOP, :])
                worst_scr[1] = jnp.max(worst[BOT, :])

        for lead in range(1, SB_LEAD + 1):
            last = lead == SB_LEAD

            @pl.when(qi >= lead - 1 if last else qi == lead - 1)
            def _(lead=lead, last=last):
                if last:
                    process([qi - i for i in range(lead)], True, True, [FULL] * (lead - 1) + [TOP])
                else:
                    process([qi - i for i in range(lead)], True, False)
                    worst_scr[0] = jnp.float32(SB_DEAD_LOG2)
                    worst_scr[1] = jnp.float32(SB_DEAD_LOG2)

        @pl.when(jnp.logical_and(qi >= SB_LEAD - 1, worst_scr[1] > SB_DEAD_LOG2))
        def _():
            process([qi - (SB_LEAD - 1)], False, True, [BOT])

        def cond(st):
            return jnp.logical_and(st[0] >= 0, st[1] > SB_DEAD_LOG2)

        def body(st):
            process([st[0]], False, True)
            return st[0] - 1, jnp.maximum(worst_scr[0], worst_scr[1])

        lax.while_loop(cond, body, (qi - SB_LEAD, jnp.maximum(worst_scr[0], worst_scr[1])))
        for p in range(npairs):
            o_ref[0, rows, p * BLK:(p + 1) * BLK] = acc_scr[p].astype(BF16)
        return carry

    lax.fori_loop(0, qpb, query_block, 0)


def _sb_attention(q, k, v0, v1, tt):
    bsz, lp, width = q.shape
    nblk = lp // BLK
    qpb = SB_Q_BLOCKS_PER_STEP
    assert nblk % qpb == 0
    npairs = width // BLK
    nparts = nblk // qpb
    qblk = lambda b, g: (b, g, 0)
    kv_hbm = pl.BlockSpec(memory_space=pl.ANY)
    return pl.pallas_call(
        functools.partial(_sb_kernel, nblk=nblk, npairs=npairs, qpb=qpb),
        grid=(bsz, nparts),
        in_specs=[
            pl.BlockSpec((1, qpb * BLK, width), qblk),
            kv_hbm,
            kv_hbm,
            kv_hbm,
            _const_spec((BLK, BLK)),
        ],
        out_specs=pl.BlockSpec((1, qpb * BLK, width), qblk),
        out_shape=jax.ShapeDtypeStruct((bsz, lp, width), BF16),
        scratch_shapes=[
            pltpu.VMEM((lp, width), BF16),
            pltpu.VMEM((lp, width), BF16),
            pltpu.VMEM((lp, width), BF16),
            pltpu.SemaphoreType.DMA((3, nparts + 1)),
            pltpu.VMEM((npairs, 2 * BLK, BLK), BF16),
            pltpu.VMEM((2 * npairs, BLK, BLK), F32),
            pltpu.VMEM((npairs, BLK, BLK), F32),
            pltpu.SMEM((2,), F32),
        ],
        compiler_params=pltpu.CompilerParams(
            dimension_semantics=("arbitrary", "arbitrary"), vmem_limit_bytes=V7X_VMEM_LIMIT),
        name="sb_attention",
    )(q, k, v0, v1, tt)


def _outproj_rows(ys, o, h, wa_ref, wb_ref, gsb_ref, gpost_ref, gffn_ref):
    osb = _rms(o.astype(F32), gsb_ref[...]).astype(BF16)
    mix = jnp.dot(ys, wa_ref[...], preferred_element_type=F32)
    mix = mix + jnp.dot(osb, wb_ref[...], preferred_element_type=F32)
    h1 = h + _rms(mix, gpost_ref[...])
    return h1, _rms(h1, gffn_ref[...]).astype(BF16)


def _outproj_kernel(ys_ref, o_ref, h_ref, wa_ref, wb_ref, gsb_ref, gpost_ref, gffn_ref,
                    h1_ref, xn2_ref, *, nsub):
    sub = h_ref.shape[1] // nsub
    rows = [pl.ds(k * sub, sub) for k in range(nsub)]
    osb = [_rms(o_ref[0, r, :].astype(F32), gsb_ref[...]).astype(BF16) for r in rows]
    mix = [jnp.dot(ys_ref[0, r, :], wa_ref[...], preferred_element_type=F32)
           + jnp.dot(osb[k], wb_ref[...], preferred_element_type=F32) for k, r in enumerate(rows)]
    for k, r in enumerate(rows):
        h1 = h_ref[0, r, :] + _rms(mix[k], gpost_ref[...])
        h1_ref[0, r, :] = h1
        xn2_ref[0, r, :] = _rms(h1, gffn_ref[...]).astype(BF16)


def _outproj_meta_kernel(ys_ref, o_ref, hm_ref, wa_ref, wb_ref, gsb_ref, gpost_ref, gffn_ref, xn2_ref):
    _, xn2 = _outproj_rows(ys_ref[0], o_ref[0], hm_ref[...], wa_ref, wb_ref, gsb_ref, gpost_ref, gffn_ref)
    xn2_ref[0] = xn2


def _outproj(y_ssd, o_sb, x, hm, w_a, w_b, g_sb, g_post, g_ffn):
    bsz, seq, _ = x.shape
    lp = y_ssd.shape[1]
    tm = OUTPROJ_TM
    per_seq = seq // tm
    weights = [
        _const_spec((SSD_INNER, D_MODEL)),
        _const_spec((SB_WIDTH, D_MODEL)),
        _const_spec((1, SB_WIDTH)),
        _const_spec((1, D_MODEL)),
        _const_spec((1, D_MODEL)),
    ]
    params = pltpu.CompilerParams(dimension_semantics=("arbitrary",), vmem_limit_bytes=V7X_VMEM_LIMIT)
    tile = lambda t: (t // per_seq, t % per_seq, 0)
    h1, xn2 = pl.pallas_call(
        functools.partial(_outproj_kernel, nsub=OUTPROJ_SUBTILES),
        grid=(bsz * per_seq,),
        in_specs=[
            pl.BlockSpec((1, tm, SSD_INNER), tile),
            pl.BlockSpec((1, tm, SB_WIDTH), tile),
            pl.BlockSpec((1, tm, D_MODEL), tile),
        ] + weights,
        out_specs=[pl.BlockSpec((1, tm, D_MODEL), tile), pl.BlockSpec((1, tm, D_MODEL), tile)],
        out_shape=[jax.ShapeDtypeStruct((bsz, seq, D_MODEL), F32),
                   jax.ShapeDtypeStruct((bsz, seq, D_MODEL), BF16)],
        compiler_params=params,
        name="outproj",
    )(y_ssd, o_sb, x, w_a, w_b, g_sb, g_post, g_ffn)

    last_blk = lambda b: (0, lp // BLK - 1, 0)
    xn2_meta = pl.pallas_call(
        _outproj_meta_kernel,
        grid=(1,),
        in_specs=[
            pl.BlockSpec((1, BLK, SSD_INNER), last_blk),
            pl.BlockSpec((1, BLK, SB_WIDTH), last_blk),
            _const_spec((BLK, D_MODEL)),
        ] + weights,
        out_specs=pl.BlockSpec((1, BLK, D_MODEL), lambda b: (0, 0, 0)),
        out_shape=jax.ShapeDtypeStruct((1, BLK, D_MODEL), BF16),
        compiler_params=params,
        name="outproj_meta",
    )(y_ssd, o_sb, hm, w_a, w_b, g_sb, g_post, g_ffn)
    return h1, xn2, xn2_meta


def _gelu_tanh(x):
    return 0.5 * x * (1.0 + jnp.tanh(math.sqrt(2.0 / math.pi) * (x + 0.044715 * (x * x * x))))


def _ffn_kernel(xn_ref, xm_ref, h1_ref, wup_ref, cw_ref, cb_ref, wd_ref, g_ref, out_ref,
                gbuf0, gbuf1, ubuf0, ubuf1, halo, act_scr, *, tm, tf):
    nch = D_FF // tf
    i = pl.program_id(1)
    gbufs, ubufs = (gbuf0, gbuf1), (ubuf0, ubuf1)
    cols = [slice(c * tf, (c + 1) * tf) for c in range(nch)]
    up_cols = [slice(D_FF + c * tf, D_FF + (c + 1) * tf) for c in range(nch)]

    @pl.when(i == 0)
    def _():
        for c in range(nch):
            gm = jnp.dot(xm_ref[0], wup_ref[:, cols[c]], preferred_element_type=F32)
            halo[c] = gm[N_META - 8:, :]

    def project(c, slot):
        gbuf, ubuf = gbufs[slot], ubufs[slot]
        xn = xn_ref[0]
        gate = jnp.dot(xn, wup_ref[:, cols[c]], preferred_element_type=F32)
        gbuf[0:8, :] = halo[c]
        gbuf[8:8 + tm, :] = gate
        halo[c] = gate[tm - 8:, :]
        ubuf[...] = jnp.dot(xn, wup_ref[:, up_cols[c]], preferred_element_type=F32)

    def activate(c, slot):
        gbuf, ubuf = gbufs[slot], ubufs[slot]
        cw = cw_ref[:, cols[c]]
        conv = cb_ref[:, cols[c]] + cw[2:3, :] * gbuf[8:8 + tm, :]
        conv = conv + cw[1:2, :] * gbuf[7:7 + tm, :] + cw[0:1, :] * gbuf[6:6 + tm, :]
        act_scr[c] = (_gelu_tanh(conv) * ubuf[...]).astype(BF16)

    project(0, 0)
    for c in range(nch - 1):
        project(c + 1, (c + 1) % 2)
        activate(c, c % 2)
    activate(nch - 1, (nch - 1) % 2)
    act = jnp.concatenate([act_scr[c] for c in range(nch)], axis=1)
    down = jnp.dot(act, wd_ref[...], preferred_element_type=F32)
    out_ref[0] = h1_ref[0] + _rms(down, g_ref[...])


def _ffn(xn2, xn2_meta, h1, w_up, conv_w, conv_b, w_down, g_post):
    bsz, seq, _ = xn2.shape
    tm, tf = FFN_TM, FFN_TF
    nch = D_FF // tf
    xblk = lambda b, i: (b, i, 0)
    meta = lambda b, i: (0, BLK // N_META - 1, 0)
    return pl.pallas_call(
        functools.partial(_ffn_kernel, tm=tm, tf=tf),
        grid=(bsz, seq // tm),
        in_specs=[
            pl.BlockSpec((1, tm, D_MODEL), xblk),
            pl.BlockSpec((1, N_META, D_MODEL), meta),
            pl.BlockSpec((1, tm, D_MODEL), xblk),
            _const_spec((D_MODEL, 2 * D_FF)),
            _const_spec((FFN_CONV, D_FF)),
            _const_spec((1, D_FF)),
            _const_spec((D_FF, D_MODEL)),
            _const_spec((1, D_MODEL)),
        ],
        out_specs=pl.BlockSpec((1, tm, D_MODEL), xblk),
        out_shape=jax.ShapeDtypeStruct((bsz, seq, D_MODEL), F32),
        scratch_shapes=[
            pltpu.VMEM((tm + 8, tf), F32),
            pltpu.VMEM((tm + 8, tf), F32),
            pltpu.VMEM((tm, tf), F32),
            pltpu.VMEM((tm, tf), F32),
            pltpu.VMEM((nch, 8, tf), F32),
            pltpu.VMEM((nch, tm, tf), BF16),
        ],
        compiler_params=pltpu.CompilerParams(
            dimension_semantics=("arbitrary", "arbitrary"), vmem_limit_bytes=V7X_VMEM_LIMIT),
        name="ffn",
    )(xn2, xn2_meta, h1, w_up, conv_w, conv_b, w_down, g_post)


def _pad_lanes(v, n):
    return jnp.pad(v, ((0, 0), (0, n - v.shape[-1])))


def kernel(x, meta_tokens, mix_pre_g, w_in, ssd_conv_w, ssd_conv_b, ssd_dt_bias, ssd_a_log, ssd_d,
           ssd_norm_g, sb_norm_g, w_out, mix_post_g, ffn_pre_g, w_up, ffn_conv_w, ffn_conv_b,
           w_down, ffn_post_g):
    bsz, seq, d = x.shape
    depth = w_in.shape[0]
    assert depth == 1 and d == D_MODEL and seq % OUTPROJ_TM == 0

    hm = jnp.concatenate([jnp.zeros((PAD, d), x.dtype), meta_tokens.astype(x.dtype)], axis=0)

    l = 0
    off_xbc = SSD_INNER
    off_dt = off_xbc + XBC_WIDTH
    off_q = off_dt + SSD_HEADS
    w = w_in[l]
    w_main = jnp.concatenate([w[:, :off_dt], w[:, off_q:]], axis=1).astype(BF16)
    w_dt = _pad_lanes(w[:, off_dt:off_q], DT_LANES).astype(BF16)
    z, xbc, q, k, v0, v1, dt_raw = _inproj(x, hm, mix_pre_g[l][None], w_main, w_dt)

    head_of_col = jnp.arange(SSD_INNER) // SSD_HEAD_DIM
    expand = (jnp.arange(DT_LANES)[:, None] == head_of_col[None, :]).astype(BF16)
    d_exp = jnp.repeat(ssd_d[l].astype(F32), SSD_HEAD_DIM)[None]
    y_ssd = _ssd(xbc, z, dt_raw, ssd_conv_w[l], ssd_conv_b[l][None],
                 _pad_lanes(ssd_dt_bias[l][None], DT_LANES), _pad_lanes(ssd_a_log[l][None], DT_LANES),
                 d_exp, ssd_norm_g[l][None], expand)

    kk = jnp.arange(BLK)
    tt = (kk[:, None] > kk[None, :]).astype(BF16)
    o_sb = _sb_attention(q, k, v0, v1, tt)

    wo = w_out[l].astype(BF16)
    h1, xn2, xn2_meta = _outproj(y_ssd, o_sb, x, hm, wo[:SSD_INNER], wo[SSD_INNER:], sb_norm_g[l][None],
                                 mix_post_g[l][None], ffn_pre_g[l][None])

    return _ffn(xn2, xn2_meta, h1, w_up[l].astype(BF16), ffn_conv_w[l], ffn_conv_b[l][None],
                w_down[l].astype(BF16), ffn_post_g[l][None])
```

```python
import functools
import math

import jax
import jax.numpy as jnp
from jax import lax
from jax.experimental import pallas as pl
from jax.experimental.pallas import tpu as pltpu

F32 = jnp.float32
BF16 = jnp.bfloat16

D_MODEL = 1024
N_META = 16
BLK = 128
PAD = BLK - N_META
SSD_HEADS = 16
SSD_HEAD_DIM = 64
SSD_GROUPS = 2
SSD_STATE = 128
SSD_INNER = 1024
SSD_CONV = 4
XBC_WIDTH = SSD_INNER + 2 * SSD_GROUPS * SSD_STATE
SB_WIDTH = 1024
SB_HEAD_DIM = 64
D_FF = 2816
FFN_CONV = 3
EPS = 1e-6
DT_LANES = 128
GROUP_COLS = SSD_INNER // SSD_GROUPS
LOG2E = 1.4426950408889634

V7X_VMEM_LIMIT = 56 * 1024 * 1024

INPROJ_TM = 1024
OUTPROJ_TM = 1024
OUTPROJ_SUBTILES = 8
FFN_TM = 1024
FFN_TF = 256
SSD_SEQS_PER_STEP = 2
SB_Q_BLOCKS_PER_STEP = 3


def _rms(x, g):
    return x * lax.rsqrt(jnp.mean(x * x, axis=-1, keepdims=True) + EPS) * g


def _const_spec(shape):
    nd = len(shape)
    return pl.BlockSpec(shape, lambda *_: (0,) * nd, pipeline_mode=pl.Buffered(1))


_PROJ_SPLITS = (("z", SSD_INNER), ("xbc", XBC_WIDTH), ("q", SB_WIDTH), ("k", SB_WIDTH), ("v", SB_WIDTH))
_INPROJ_OUT_WIDTHS = (SSD_INNER, XBC_WIDTH, SB_WIDTH, SB_WIDTH, SB_WIDTH, SB_WIDTH, DT_LANES)
_INPROJ_OUT_DTYPES = (BF16, BF16, BF16, BF16, BF16, BF16, F32)


def _inproj_rows(h, g_ref, w_ref, wdt_ref, out_refs):
    z_ref, xbc_ref, q_ref, k_ref, v0_ref, v1_ref, dt_ref = out_refs
    xn = _rms(h, g_ref[...]).astype(BF16)
    lo = 0
    proj = {}
    for name, width in _PROJ_SPLITS:
        proj[name] = jnp.dot(xn, w_ref[:, lo:lo + width], preferred_element_type=F32)
        lo += width
    z_ref[0] = (proj["z"] * jax.nn.sigmoid(proj["z"])).astype(BF16)
    xbc_ref[0] = proj["xbc"].astype(BF16)
    q_ref[0] = (proj["q"] * (LOG2E / math.sqrt(SB_HEAD_DIM))).astype(BF16)
    k_ref[0] = proj["k"].astype(BF16)
    v = proj["v"].astype(BF16)
    lane = lax.broadcasted_iota(jnp.int32, v.shape, 1)
    even_head = (lane & SB_HEAD_DIM) == 0
    zero = jnp.zeros_like(v)
    v0_ref[0] = jnp.where(even_head, v, zero)
    v1_ref[0] = jnp.where(even_head, zero, v)
    dt_ref[0] = jnp.dot(xn, wdt_ref[...], preferred_element_type=F32)


def _inproj_kernel(x_ref, g_ref, w_ref, wdt_ref, *out_refs):
    _inproj_rows(x_ref[...], g_ref, w_ref, wdt_ref, out_refs)


def _inproj_meta_kernel(hm_ref, g_ref, w_ref, wdt_ref, *refs):
    nout = len(_INPROJ_OUT_WIDTHS)
    out_refs, scratch = refs[nout:2 * nout], refs[2 * nout:]

    @pl.when(pl.program_id(0) == 0)
    def _():
        _inproj_rows(hm_ref[...], g_ref, w_ref, wdt_ref, scratch)

    for out_ref, scr in zip(out_refs, scratch):
        out_ref[...] = scr[...]


def _inproj(x, hm, g, w_main, w_dt):
    bsz, seq, _ = x.shape
    lp = seq + BLK
    tm = INPROJ_TM
    per_seq = seq // tm
    n_main = w_main.shape[1]
    out_shape = [jax.ShapeDtypeStruct((bsz, lp, width), dt)
                 for width, dt in zip(_INPROJ_OUT_WIDTHS, _INPROJ_OUT_DTYPES)]
    weights = [_const_spec((1, D_MODEL)), _const_spec((D_MODEL, n_main)), _const_spec((D_MODEL, DT_LANES))]
    params = pltpu.CompilerParams(dimension_semantics=("arbitrary",), vmem_limit_bytes=V7X_VMEM_LIMIT)

    tile = lambda t: (t // per_seq, t % per_seq, 0)
    outs = pl.pallas_call(
        _inproj_kernel,
        grid=(bsz * per_seq,),
        in_specs=[pl.BlockSpec((tm, D_MODEL), lambda t: (t, 0))] + weights,
        out_specs=[pl.BlockSpec((1, tm, width), tile) for width in _INPROJ_OUT_WIDTHS],
        out_shape=out_shape,
        compiler_params=params,
        name="inproj",
    )(x.reshape(bsz * seq, D_MODEL), g, w_main, w_dt)

    nout = len(outs)
    last_blk = lambda b: (b, lp // BLK - 1, 0)
    return pl.pallas_call(
        _inproj_meta_kernel,
        grid=(bsz,),
        in_specs=[_const_spec((BLK, D_MODEL))] + weights + [pl.BlockSpec(memory_space=pl.ANY)] * nout,
        out_specs=[pl.BlockSpec((1, BLK, width), last_blk) for width in _INPROJ_OUT_WIDTHS],
        out_shape=out_shape,
        input_output_aliases={4 + j: j for j in range(nout)},
        scratch_shapes=[pltpu.VMEM((1, BLK, width), dt)
                        for width, dt in zip(_INPROJ_OUT_WIDTHS, _INPROJ_OUT_DTYPES)],
        compiler_params=params,
        name="inproj_meta",
    )(hm, g, w_main, w_dt, *outs)


def _dot_f32_by_01(lhs, rhs, f32_side):
    x = lhs if f32_side == "lhs" else rhs
    p1 = x.astype(BF16)
    r1 = x - p1.astype(F32)
    p2 = r1.astype(BF16)
    p3 = (r1 - p2.astype(F32)).astype(BF16)
    out = None
    for piece in (p1, p2, p3):
        ops = (piece, rhs) if f32_side == "lhs" else (lhs, piece)
        term = jnp.dot(*ops, preferred_element_type=F32)
        out = term if out is None else out + term
    return out


def _ssd_kernel(xbc_ref, z_ref, dt_ref, cw_ref, cb_ref, dtb_ref, alog_ref, dexp_ref, g_ref, e_ref,
                out_ref, prev, state, *, nseq):
    c = pl.program_id(1)

    @pl.when(c == 0)
    def _():
        prev[...] = jnp.zeros_like(prev)
        state[...] = jnp.zeros_like(state)

    seqs = range(nseq)
    groups = range(SSD_GROUPS)
    row = lax.broadcasted_iota(jnp.int32, (BLK, 1), 0)
    real = jnp.logical_or(c > 0, row >= PAD)
    lane = lax.broadcasted_iota(jnp.int32, (BLK, DT_LANES), 1)
    dt_valid = jnp.logical_and(real, lane < SSD_HEADS)
    ri = lax.broadcasted_iota(jnp.int32, (BLK, BLK), 0)
    ci = lax.broadcasted_iota(jnp.int32, (BLK, BLK), 1)
    causal = ri >= ci
    tri = jnp.where(causal, 1.0, 0.0).astype(BF16)
    half = ci < SSD_HEAD_DIM
    cw = cw_ref[...]
    e = e_ref[...]
    a_neg = -jnp.exp(alog_ref[...]) * LOG2E

    xc, dt, adt = [], [], []
    for s in seqs:
        x_raw = xbc_ref[s].astype(F32)
        prev[s, 8:8 + BLK, :] = x_raw
        acc = cb_ref[...] + cw[3:4, :] * x_raw
        for tap in range(SSD_CONV - 1):
            acc = acc + cw[tap:tap + 1, :] * prev[s, 5 + tap:5 + tap + BLK, :]
        prev[s, 0:8, :] = x_raw[BLK - 8:, :]
        xc.append(jnp.where(real, acc * jax.nn.sigmoid(acc), 0.0))
        dts = jax.nn.softplus(dt_ref[s] + dtb_ref[...])
        dt.append(jnp.where(dt_valid, dts, 0.0))
        adt.append(dt[s] * a_neg)
    xs = [xc[s][:, :SSD_INNER] for s in seqs]

    acs = [_dot_f32_by_01(tri, adt[s], f32_side="rhs") for s in seqs]
    acs_t = [acs[s].T for s in seqs]
    dt_exp = [_dot_f32_by_01(dt[s], e, f32_side="lhs") for s in seqs]
    acs_exp = [_dot_f32_by_01(acs[s], e, f32_side="lhs") for s in seqs]

    xdt_b, xdec, off_scale, chunk_decay, b_t, c_gb, b_gb = [], [], [], [], [], [], []
    for s in seqs:
        last = acs_exp[s][BLK - 1:BLK, :]
        xdt = xs[s] * dt_exp[s]
        xdec.append((xdt * jnp.exp2(last - acs_exp[s])).astype(BF16))
        xdt_b.append(xdt.astype(BF16))
        off_scale.append(jnp.exp2(acs_exp[s]))
        chunk_decay.append(jnp.exp2(last))
        b_g = [xc[s][:, SSD_INNER + g * SSD_STATE:SSD_INNER + (g + 1) * SSD_STATE] for g in groups]
        c_g = [xc[s][:, SSD_INNER + (SSD_GROUPS + g) * SSD_STATE:
                     SSD_INNER + (SSD_GROUPS + g + 1) * SSD_STATE] for g in groups]
        b_gb.append([x.astype(BF16) for x in b_g])
        c_gb.append([x.astype(BF16) for x in c_g])
        b_t.append([x.T.astype(BF16) for x in b_g])

    gcols = [slice(g * GROUP_COLS, (g + 1) * GROUP_COLS) for g in groups]
    cb_mat = [[lax.dot_general(c_gb[s][g], b_gb[s][g], (((1,), (1,)), ((), ())),
                               preferred_element_type=F32) for g in groups] for s in seqs]
    s_prev = [[state[s, :, gcols[g]] for g in groups] for s in seqs]
    y_off = [[jnp.dot(c_gb[s][g], s_prev[s][g].astype(BF16), preferred_element_type=F32)
              for g in groups] for s in seqs]

    heads_per_group = SSD_HEADS // SSD_GROUPS
    decay_mats = [[None] * SSD_HEADS for _ in seqs]
    for s in seqs:
        for hh in range(SSD_HEADS):
            seg = acs[s][:, hh:hh + 1] - acs_t[s][hh:hh + 1, :]
            ldec = jnp.exp2(jnp.where(causal, seg, -1e30))
            decay_mats[s][hh] = (cb_mat[s][hh // heads_per_group] * ldec).astype(BF16)

    y_diag = [[jnp.dot(decay_mats[s][hh], xdt_b[s][:, (hh // 2) * BLK:(hh // 2 + 1) * BLK],
                       preferred_element_type=F32) for hh in range(SSD_HEADS)] for s in seqs]
    new = [[jnp.dot(b_t[s][g], xdec[s][:, gcols[g]], preferred_element_type=F32)
            for g in groups] for s in seqs]

    for s in seqs:
        y_cols = []
        for g in groups:
            state[s, :, gcols[g]] = s_prev[s][g] * chunk_decay[s][:, gcols[g]] + new[s][g]
            yo = y_off[s][g] * off_scale[s][:, gcols[g]]
            for pair in range(GROUP_COLS // BLK):
                h0 = (g * GROUP_COLS + pair * BLK) // SSD_HEAD_DIM
                y_cols.append(jnp.where(half, y_diag[s][h0], y_diag[s][h0 + 1])
                              + yo[:, pair * BLK:(pair + 1) * BLK])
        y = jnp.concatenate(y_cols, axis=1) + xs[s] * dexp_ref[...]
        y = y * z_ref[s].astype(F32)
        out_ref[s] = _rms(y, g_ref[...]).astype(BF16)


def _ssd(xbc, z, dt_raw, conv_w, conv_b, dt_bias, a_log, d_exp, norm_g, expand):
    bsz, lp, _ = xbc.shape
    nblk = lp // BLK
    nseq = SSD_SEQS_PER_STEP
    assert bsz % nseq == 0
    blk = lambda b, c: (b, (c + nblk - 1) % nblk, 0)
    return pl.pallas_call(
        functools.partial(_ssd_kernel, nseq=nseq),
        grid=(bsz // nseq, nblk),
        in_specs=[
            pl.BlockSpec((nseq, BLK, XBC_WIDTH), blk),
            pl.BlockSpec((nseq, BLK, SSD_INNER), blk),
            pl.BlockSpec((nseq, BLK, DT_LANES), blk),
            _const_spec((SSD_CONV, XBC_WIDTH)),
            _const_spec((1, XBC_WIDTH)),
            _const_spec((1, DT_LANES)),
            _const_spec((1, DT_LANES)),
            _const_spec((1, SSD_INNER)),
            _const_spec((1, SSD_INNER)),
            _const_spec((DT_LANES, SSD_INNER)),
        ],
        out_specs=pl.BlockSpec((nseq, BLK, SSD_INNER), blk),
        out_shape=jax.ShapeDtypeStruct((bsz, lp, SSD_INNER), BF16),
        scratch_shapes=[
            pltpu.VMEM((nseq, BLK + 8, XBC_WIDTH), F32),
            pltpu.VMEM((nseq, SSD_STATE, SSD_INNER), F32),
        ],
        compiler_params=pltpu.CompilerParams(
            dimension_semantics=("arbitrary", "arbitrary"), vmem_limit_bytes=V7X_VMEM_LIMIT),
        name="ssd",
    )(xbc, z, dt_raw, conv_w, conv_b, dt_bias, a_log, d_exp, norm_g, expand)


SB_DEAD_LOG2 = -151.0
SB_LEAD = 3
SB_TOP_ROWS = 32
FULL, TOP, BOT = slice(0, BLK), slice(0, SB_TOP_ROWS), slice(SB_TOP_ROWS, BLK)


def _sb_kv_copies(hbm_refs, bufs, sems, batch, part, nblk, qpb):
    nparts = nblk // qpb
    seq_rows = (nblk - 1) * BLK
    if part == nparts:
        rows = pl.ds(seq_rows, BLK)
    else:
        start = part * qpb * BLK
        rows = pl.ds(start, min(qpb * BLK, seq_rows - start))
    return [pltpu.make_async_copy(src.at[batch, rows, :], dst.at[rows, :], sems.at[a, part])
            for a, (src, dst) in enumerate(zip(hbm_refs, bufs))]


def _sb_kernel(q_ref, k_hbm, v0_hbm, v1_hbm, tt_ref, o_ref, k_ref, v0_ref, v1_ref, kv_sems,
               qm_scr, run_scr, acc_scr, worst_scr, *, nblk, npairs, qpb):
    batch, step = pl.program_id(0), pl.program_id(1)
    nparts = nblk // qpb
    hbm_refs, bufs = (k_hbm, v0_hbm, v1_hbm), (k_ref, v0_ref, v1_ref)

    @pl.when(step == 0)
    def _():
        for part in [nparts] + list(range(nparts)):
            for cp in _sb_kv_copies(hbm_refs, bufs, kv_sems, batch, part, nblk, qpb):
                cp.start()
        for cp in _sb_kv_copies(hbm_refs, bufs, kv_sems, batch, nparts, nblk, qpb):
            cp.wait()

    for part in range(nparts):
        @pl.when(step == part)
        def _(part=part):
            for cp in _sb_kv_copies(hbm_refs, bufs, kv_sems, batch, part, nblk, qpb):
                cp.wait()

    lane = lax.broadcasted_iota(jnp.int32, (BLK, BLK), 1)
    sub = lax.broadcasted_iota(jnp.int32, (BLK, BLK), 0)
    first = lane < SB_HEAD_DIM
    strictly_older = lane < sub
    tt = tt_ref[...]

    def query_block(r, carry):
        qi = (step * qpb + r + 1) % nblk
        rows = pl.ds(pl.multiple_of(r * BLK, BLK), BLK)
        for p in range(npairs):
            qp = q_ref[0, rows, p * BLK:(p + 1) * BLK]
            zero = jnp.zeros_like(qp)
            qm_scr[p, :BLK, :] = jnp.where(first, qp, zero)
            qm_scr[p, BLK:, :] = jnp.where(first, zero, qp)

        def process(blocks, init, check, row_sets=None):
            offs = [pl.multiple_of(((j + nblk - 1) % nblk) * BLK, BLK) for j in blocks]
            nb = range(len(blocks))
            row_sets = row_sets or [FULL] * len(blocks)
            heads = range(2 * npairs)
            scores = {}
            for b in nb:
                rs = row_sets[b]
                for p in range(npairs):
                    if rs is FULL:
                        lhs = qm_scr[p]
                    else:
                        lhs = jnp.concatenate([qm_scr[p, rs, :],
                                               qm_scr[p, BLK + rs.start:BLK + rs.stop, :]], axis=0)
                    scores[b, p] = lax.dot_general(
                        lhs, k_ref[pl.ds(offs[b], BLK), p * BLK:(p + 1) * BLK],
                        (((1,), (1,)), ((), ())), preferred_element_type=F32)
            log_betas, log_keeps, totals = {}, {}, {}
            for b in nb:
                n = row_sets[b].stop - row_sets[b].start
                for h in heads:
                    s = scores[b, h // 2][(h % 2) * n:(h % 2 + 1) * n, :]
                    neg_part = jnp.minimum(s, 0.0)
                    neg_pos = neg_part - s
                    sp = jnp.log(1.0 + jnp.exp2(neg_part + neg_pos)) * LOG2E
                    log_beta = neg_part - sp
                    log_keep = neg_pos - sp
                    if init and b == 0:
                        log_keep = jnp.where(strictly_older, log_keep, 0.0)
                    log_betas[b, h] = log_beta
                    log_keeps[b, h] = log_keep.astype(BF16)
                    totals[b, h] = jnp.sum(log_keep, axis=1, keepdims=True)
            sums = [jnp.dot(jnp.concatenate([log_keeps[b, h] for h in heads], axis=0), tt,
                            preferred_element_type=F32) for b in nb]
            weights = {}
            worst = None
            for h in heads:
                run = None if init else run_scr[h]
                for b in nb:
                    rs = row_sets[b]
                    n = rs.stop - rs.start
                    after, total = sums[b][h * n:(h + 1) * n, :], totals[b, h]
                    if run is None:
                        w = jnp.where(strictly_older, jnp.exp2(log_betas[b, h] + after), 0.0)
                        run = jnp.broadcast_to(total, (BLK, BLK))
                    else:
                        part = run if rs is FULL else run[rs, :]
                        w = jnp.exp2(log_betas[b, h] + (after + part))
                        part = part + total
                        if rs is FULL:
                            run = part
                        elif rs is TOP:
                            run = jnp.concatenate([part, run[BOT, :]], axis=0)
                        else:
                            run = jnp.concatenate([run[TOP, :], part], axis=0)
                    weights[b, h] = w.astype(BF16)
                run_scr[h] = run
                if check:
                    worst = run if worst is None else jnp.maximum(worst, run)
            for p in range(npairs):
                pv = None if init else acc_scr[p]
                for b in nb:
                    rs = row_sets[b]
                    v_heads = jnp.concatenate(
                        [v0_ref[pl.ds(offs[b], BLK), p * BLK:(p + 1) * BLK],
                         v1_ref[pl.ds(offs[b], BLK), p * BLK:(p + 1) * BLK]], axis=0)
                    w_heads = jnp.concatenate([weights[b, 2 * p], weights[b, 2 * p + 1]], axis=1)
                    contrib = jnp.dot(w_heads, v_heads, preferred_element_type=F32)
                    if pv is None:
                        pv = contrib
                    elif rs is FULL:
                        pv = pv + contrib
                    elif rs is TOP:
                        pv = jnp.concatenate([pv[TOP, :] + contrib, pv[BOT, :]], axis=0)
                    else:
                        pv = jnp.concatenate([pv[TOP, :], pv[BOT, :] + contrib], axis=0)
                acc_scr[p] = pv
            if check:
                worst_scr[0] = jnp.max(worst[TOP, :])
                worst_scr[1] = jnp.max(worst[BOT, :])

        for lead in range(1, SB_LEAD + 1):
            last = lead == SB_LEAD

            @pl.when(qi >= lead - 1 if last else qi == lead - 1)
            def _(lead=lead, last=last):
                if last:
                    process([qi - i for i in range(lead)], True, True, [FULL] * (lead - 1) + [TOP])
                else:
                    process([qi - i for i in range(lead)], True, False)
                    worst_scr[0] = jnp.float32(SB_DEAD_LOG2)
                    worst_scr[1] = jnp.float32(SB_DEAD_LOG2)

        @pl.when(jnp.logical_and(qi >= SB_LEAD - 1, worst_scr[1] > SB_DEAD_LOG2))
        def _():
            process([qi - (SB_LEAD - 1)], False, True, [BOT])

        def cond(st):
            return jnp.logical_and(st[0] >= 0, st[1] > SB_DEAD_LOG2)

        def body(st):
            process([st[0]], False, True)
            return st[0] - 1, jnp.maximum(worst_scr[0], worst_scr[1])

        lax.while_loop(cond, body, (qi - SB_LEAD, jnp.maximum(worst_scr[0], worst_scr[1])))
        for p in range(npairs):
            o_ref[0, rows, p * BLK:(p + 1) * BLK] = acc_scr[p].astype(BF16)
        return carry

    lax.fori_loop(0, qpb, query_block, 0)


def _sb_attention(q, k, v0, v1, tt):
    bsz, lp, width = q.shape
    nblk = lp // BLK
    qpb = SB_Q_BLOCKS_PER_STEP
    assert nblk % qpb == 0
    npairs = width // BLK
    nparts = nblk // qpb
    qblk = lambda b, g: (b, g, 0)
    kv_hbm = pl.BlockSpec(memory_space=pl.ANY)
    return pl.pallas_call(
        functools.partial(_sb_kernel, nblk=nblk, npairs=npairs, qpb=qpb),
        grid=(bsz, nparts),
        in_specs=[
            pl.BlockSpec((1, qpb * BLK, width), qblk),
            kv_hbm,
            kv_hbm,
            kv_hbm,
            _const_spec((BLK, BLK)),
        ],
        out_specs=pl.BlockSpec((1, qpb * BLK, width), qblk),
        out_shape=jax.ShapeDtypeStruct((bsz, lp, width), BF16),
        scratch_shapes=[
            pltpu.VMEM((lp, width), BF16),
            pltpu.VMEM((lp, width), BF16),
            pltpu.VMEM((lp, width), BF16),
            pltpu.SemaphoreType.DMA((3, nparts + 1)),
            pltpu.VMEM((npairs, 2 * BLK, BLK), BF16),
            pltpu.VMEM((2 * npairs, BLK, BLK), F32),
            pltpu.VMEM((npairs, BLK, BLK), F32),
            pltpu.SMEM((2,), F32),
        ],
        compiler_params=pltpu.CompilerParams(
            dimension_semantics=("arbitrary", "arbitrary"), vmem_limit_bytes=V7X_VMEM_LIMIT),
        name="sb_attention",
    )(q, k, v0, v1, tt)


def _outproj_rows(ys, o, h, wa_ref, wb_ref, gsb_ref, gpost_ref, gffn_ref):
    osb = _rms(o.astype(F32), gsb_ref[...]).astype(BF16)
    mix = jnp.dot(ys, wa_ref[...], preferred_element_type=F32)
    mix = mix + jnp.dot(osb, wb_ref[...], preferred_element_type=F32)
    h1 = h + _rms(mix, gpost_ref[...])
    return h1, _rms(h1, gffn_ref[...]).astype(BF16)


def _outproj_kernel(ys_ref, o_ref, h_ref, wa_ref, wb_ref, gsb_ref, gpost_ref, gffn_ref,
                    h1_ref, xn2_ref, *, nsub):
    sub = h_ref.shape[1] // nsub
    rows = [pl.ds(k * sub, sub) for k in range(nsub)]
    osb = [_rms(o_ref[0, r, :].astype(F32), gsb_ref[...]).astype(BF16) for r in rows]
    mix = [jnp.dot(ys_ref[0, r, :], wa_ref[...], preferred_element_type=F32)
           + jnp.dot(osb[k], wb_ref[...], preferred_element_type=F32) for k, r in enumerate(rows)]
    for k, r in enumerate(rows):
        h1 = h_ref[0, r, :] + _rms(mix[k], gpost_ref[...])
        h1_ref[0, r, :] = h1
        xn2_ref[0, r, :] = _rms(h1, gffn_ref[...]).astype(BF16)


def _outproj_meta_kernel(ys_ref, o_ref, hm_ref, wa_ref, wb_ref, gsb_ref, gpost_ref, gffn_ref, xn2_ref):
    _, xn2 = _outproj_rows(ys_ref[0], o_ref[0], hm_ref[...], wa_ref, wb_ref, gsb_ref, gpost_ref, gffn_ref)
    xn2_ref[0] = xn2


def _outproj(y_ssd, o_sb, x, hm, w_a, w_b, g_sb, g_post, g_ffn):
    bsz, seq, _ = x.shape
    lp = y_ssd.shape[1]
    tm = OUTPROJ_TM
    per_seq = seq // tm
    weights = [
        _const_spec((SSD_INNER, D_MODEL)),
        _const_spec((SB_WIDTH, D_MODEL)),
        _const_spec((1, SB_WIDTH)),
        _const_spec((1, D_MODEL)),
        _const_spec((1, D_MODEL)),
    ]
    params = pltpu.CompilerParams(dimension_semantics=("arbitrary",), vmem_limit_bytes=V7X_VMEM_LIMIT)
    tile = lambda t: (t // per_seq, t % per_seq, 0)
    h1, xn2 = pl.pallas_call(
        functools.partial(_outproj_kernel, nsub=OUTPROJ_SUBTILES),
        grid=(bsz * per_seq,),
        in_specs=[
            pl.BlockSpec((1, tm, SSD_INNER), tile),
            pl.BlockSpec((1, tm, SB_WIDTH), tile),
            pl.BlockSpec((1, tm, D_MODEL), tile),
        ] + weights,
        out_specs=[pl.BlockSpec((1, tm, D_MODEL), tile), pl.BlockSpec((1, tm, D_MODEL), tile)],
        out_shape=[jax.ShapeDtypeStruct((bsz, seq, D_MODEL), F32),
                   jax.ShapeDtypeStruct((bsz, seq, D_MODEL), BF16)],
        compiler_params=params,
        name="outproj",
    )(y_ssd, o_sb, x, w_a, w_b, g_sb, g_post, g_ffn)

    last_blk = lambda b: (0, lp // BLK - 1, 0)
    xn2_meta = pl.pallas_call(
        _outproj_meta_kernel,
        grid=(1,),
        in_specs=[
            pl.BlockSpec((1, BLK, SSD_INNER), last_blk),
            pl.BlockSpec((1, BLK, SB_WIDTH), last_blk),
            _const_spec((BLK, D_MODEL)),
        ] + weights,
        out_specs=pl.BlockSpec((1, BLK, D_MODEL), lambda b: (0, 0, 0)),
        out_shape=jax.ShapeDtypeStruct((1, BLK, D_MODEL), BF16),
        compiler_params=params,
        name="outproj_meta",
    )(y_ssd, o_sb, hm, w_a, w_b, g_sb, g_post, g_ffn)
    return h1, xn2, xn2_meta


def _gelu_tanh(x):
    return 0.5 * x * (1.0 + jnp.tanh(math.sqrt(2.0 / math.pi) * (x + 0.044715 * (x * x * x))))


def _ffn_kernel(xn_ref, xm_ref, h1_ref, wup_ref, cw_ref, cb_ref, wd_ref, g_ref, out_ref,
                gbuf0, gbuf1, ubuf0, ubuf1, halo, act_scr, *, tm, tf):
    nch = D_FF // tf
    i = pl.program_id(1)
    gbufs, ubufs = (gbuf0, gbuf1), (ubuf0, ubuf1)
    cols = [slice(c * tf, (c + 1) * tf) for c in range(nch)]
    up_cols = [slice(D_FF + c * tf, D_FF + (c + 1) * tf) for c in range(nch)]

    @pl.when(i == 0)
    def _():
        for c in range(nch):
            gm = jnp.dot(xm_ref[0], wup_ref[:, cols[c]], preferred_element_type=F32)
            halo[c] = gm[N_META - 8:, :]

    def project(c, slot):
        gbuf, ubuf = gbufs[slot], ubufs[slot]
        xn = xn_ref[0]
        gate = jnp.dot(xn, wup_ref[:, cols[c]], preferred_element_type=F32)
        gbuf[0:8, :] = halo[c]
        gbuf[8:8 + tm, :] = gate
        halo[c] = gate[tm - 8:, :]
        ubuf[...] = jnp.dot(xn, wup_ref[:, up_cols[c]], preferred_element_type=F32)

    def activate(c, slot):
        gbuf, ubuf = gbufs[slot], ubufs[slot]
        cw = cw_ref[:, cols[c]]
        conv = cb_ref[:, cols[c]] + cw[2:3, :] * gbuf[8:8 + tm, :]
        conv = conv + cw[1:2, :] * gbuf[7:7 + tm, :] + cw[0:1, :] * gbuf[6:6 + tm, :]
        act_scr[c] = (_gelu_tanh(conv) * ubuf[...]).astype(BF16)

    project(0, 0)
    for c in range(nch - 1):
        project(c + 1, (c + 1) % 2)
        activate(c, c % 2)
    activate(nch - 1, (nch - 1) % 2)
    act = jnp.concatenate([act_scr[c] for c in range(nch)], axis=1)
    down = jnp.dot(act, wd_ref[...], preferred_element_type=F32)
    out_ref[0] = h1_ref[0] + _rms(down, g_ref[...])


def _ffn(xn2, xn2_meta, h1, w_up, conv_w, conv_b, w_down, g_post):
    bsz, seq, _ = xn2.shape
    tm, tf = FFN_TM, FFN_TF
    nch = D_FF // tf
    xblk = lambda b, i: (b, i, 0)
    meta = lambda b, i: (0, BLK // N_META - 1, 0)
    return pl.pallas_call(
        functools.partial(_ffn_kernel, tm=tm, tf=tf),
        grid=(bsz, seq // tm),
        in_specs=[
            pl.BlockSpec((1, tm, D_MODEL), xblk),
            pl.BlockSpec((1, N_META, D_MODEL), meta),
            pl.BlockSpec((1, tm, D_MODEL), xblk),
            _const_spec((D_MODEL, 2 * D_FF)),
            _const_spec((FFN_CONV, D_FF)),
            _const_spec((1, D_FF)),
            _const_spec((D_FF, D_MODEL)),
            _const_spec((1, D_MODEL)),
        ],
        out_specs=pl.BlockSpec((1, tm, D_MODEL), xblk),
        out_shape=jax.ShapeDtypeStruct((bsz, seq, D_MODEL), F32),
        scratch_shapes=[
            pltpu.VMEM((tm + 8, tf), F32),
            pltpu.VMEM((tm + 8, tf), F32),
            pltpu.VMEM((tm, tf), F32),
            pltpu.VMEM((tm, tf), F32),
            pltpu.VMEM((nch, 8, tf), F32),
            pltpu.VMEM((nch, tm, tf), BF16),
        ],
        compiler_params=pltpu.CompilerParams(
            dimension_semantics=("arbitrary", "arbitrary"), vmem_limit_bytes=V7X_VMEM_LIMIT),
        name="ffn",
    )(xn2, xn2_meta, h1, w_up, conv_w, conv_b, w_down, g_post)


def _pad_lanes(v, n):
    return jnp.pad(v, ((0, 0), (0, n - v.shape[-1])))


def kernel(x, meta_tokens, mix_pre_g, w_in, ssd_conv_w, ssd_conv_b, ssd_dt_bias, ssd_a_log, ssd_d,
           ssd_norm_g, sb_norm_g, w_out, mix_post_g, ffn_pre_g, w_up, ffn_conv_w, ffn_conv_b,
           w_down, ffn_post_g):
    bsz, seq, d = x.shape
    depth = w_in.shape[0]
    assert depth == 1 and d == D_MODEL and seq % OUTPROJ_TM == 0

    hm = jnp.concatenate([jnp.zeros((PAD, d), x.dtype), meta_tokens.astype(x.dtype)], axis=0)

    l = 0
    off_xbc = SSD_INNER
    off_dt = off_xbc + XBC_WIDTH
    off_q = off_dt + SSD_HEADS
    w = w_in[l]
    w_main = jnp.concatenate([w[:, :off_dt], w[:, off_q:]], axis=1).astype(BF16)
    w_dt = _pad_lanes(w[:, off_dt:off_q], DT_LANES).astype(BF16)
    z, xbc, q, k, v0, v1, dt_raw = _inproj(x, hm, mix_pre_g[l][None], w_main, w_dt)

    head_of_col = jnp.arange(SSD_INNER) // SSD_HEAD_DIM
    expand = (jnp.arange(DT_LANES)[:, None] == head_of_col[None, :]).astype(BF16)
    d_exp = jnp.repeat(ssd_d[l].astype(F32), SSD_HEAD_DIM)[None]
    y_ssd = _ssd(xbc, z, dt_raw, ssd_conv_w[l], ssd_conv_b[l][None],
                 _pad_lanes(ssd_dt_bias[l][None], DT_LANES), _pad_lanes(ssd_a_log[l][None], DT_LANES),
                 d_exp, ssd_norm_g[l][None], expand)

    kk = jnp.arange(BLK)
    tt = (kk[:, None] > kk[None, :]).astype(BF16)
    o_sb = _sb_attention(q, k, v0, v1, tt)

    wo = w_out[l].astype(BF16)
    h1, xn2, xn2_meta = _outproj(y_ssd, o_sb, x, hm, wo[:SSD_INNER], wo[SSD_INNER:], sb_norm_g[l][None],
                                 mix_post_g[l][None], ffn_pre_g[l][None])

    return _ffn(xn2, xn2_meta, h1, w_up[l].astype(BF16), ffn_conv_w[l], ffn_conv_b[l][None],
                w_down[l].astype(BF16), ffn_post_g[l][None])
```

```python
import functools
import math

import jax
import jax.numpy as jnp
from jax import lax
from jax.experimental import pallas as pl
from jax.experimental.pallas import tpu as pltpu

F32 = jnp.float32
BF16 = jnp.bfloat16

D_MODEL = 1024
N_META = 16
BLK = 128
PAD = BLK - N_META
SSD_HEADS = 16
SSD_HEAD_DIM = 64
SSD_GROUPS = 2
SSD_STATE = 128
SSD_INNER = 1024
SSD_CONV = 4
XBC_WIDTH = SSD_INNER + 2 * SSD_GROUPS * SSD_STATE
SB_WIDTH = 1024
SB_HEAD_DIM = 64
D_FF = 2816
FFN_CONV = 3
EPS = 1e-6
DT_LANES = 128
GROUP_COLS = SSD_INNER // SSD_GROUPS
LOG2E = 1.4426950408889634

V7X_VMEM_LIMIT = 56 * 1024 * 1024

INPROJ_TM = 1024
OUTPROJ_TM = 1024
OUTPROJ_SUBTILES = 8
FFN_TM = 1024
FFN_TF = 256
SSD_SEQS_PER_STEP = 2
SB_Q_BLOCKS_PER_STEP = 3


def _rms(x, g):
    return x * lax.rsqrt(jnp.mean(x * x, axis=-1, keepdims=True) + EPS) * g


def _const_spec(shape):
    nd = len(shape)
    return pl.BlockSpec(shape, lambda *_: (0,) * nd, pipeline_mode=pl.Buffered(1))


_PROJ_SPLITS = (("z", SSD_INNER), ("xbc", XBC_WIDTH), ("q", SB_WIDTH), ("k", SB_WIDTH), ("v", SB_WIDTH))
_INPROJ_OUT_WIDTHS = (SSD_INNER, XBC_WIDTH, SB_WIDTH, SB_WIDTH, SB_WIDTH, SB_WIDTH, DT_LANES)
_INPROJ_OUT_DTYPES = (BF16, BF16, BF16, BF16, BF16, BF16, F32)


def _inproj_rows(h, g_ref, w_ref, wdt_ref, out_refs):
    z_ref, xbc_ref, q_ref, k_ref, v0_ref, v1_ref, dt_ref = out_refs
    xn = _rms(h, g_ref[...]).astype(BF16)
    lo = 0
    proj = {}
    for name, width in _PROJ_SPLITS:
        proj[name] = jnp.dot(xn, w_ref[:, lo:lo + width], preferred_element_type=F32)
        lo += width
    z_ref[0] = (proj["z"] * jax.nn.sigmoid(proj["z"])).astype(BF16)
    xbc_ref[0] = proj["xbc"].astype(BF16)
    q_ref[0] = (proj["q"] * (LOG2E / math.sqrt(SB_HEAD_DIM))).astype(BF16)
    k_ref[0] = proj["k"].astype(BF16)
    v = proj["v"].astype(BF16)
    lane = lax.broadcasted_iota(jnp.int32, v.shape, 1)
    even_head = (lane & SB_HEAD_DIM) == 0
    zero = jnp.zeros_like(v)
    v0_ref[0] = jnp.where(even_head, v, zero)
    v1_ref[0] = jnp.where(even_head, zero, v)
    dt_ref[0] = jnp.dot(xn, wdt_ref[...], preferred_element_type=F32)


def _inproj_kernel(x_ref, g_ref, w_ref, wdt_ref, *out_refs):
    _inproj_rows(x_ref[...], g_ref, w_ref, wdt_ref, out_refs)


def _inproj_meta_kernel(hm_ref, g_ref, w_ref, wdt_ref, *refs):
    nout = len(_INPROJ_OUT_WIDTHS)
    out_refs, scratch = refs[nout:2 * nout], refs[2 * nout:]

    @pl.when(pl.program_id(0) == 0)
    def _():
        _inproj_rows(hm_ref[...], g_ref, w_ref, wdt_ref, scratch)

    for out_ref, scr in zip(out_refs, scratch):
        out_ref[...] = scr[...]


def _inproj(x, hm, g, w_main, w_dt):
    bsz, seq, _ = x.shape
    lp = seq + BLK
    tm = INPROJ_TM
    per_seq = seq // tm
    n_main = w_main.shape[1]
    out_shape = [jax.ShapeDtypeStruct((bsz, lp, width), dt)
                 for width, dt in zip(_INPROJ_OUT_WIDTHS, _INPROJ_OUT_DTYPES)]
    weights = [_const_spec((1, D_MODEL)), _const_spec((D_MODEL, n_main)), _const_spec((D_MODEL, DT_LANES))]
    params = pltpu.CompilerParams(dimension_semantics=("arbitrary",), vmem_limit_bytes=V7X_VMEM_LIMIT)

    tile = lambda t: (t // per_seq, t % per_seq, 0)
    outs = pl.pallas_call(
        _inproj_kernel,
        grid=(bsz * per_seq,),
        in_specs=[pl.BlockSpec((tm, D_MODEL), lambda t: (t, 0))] + weights,
        out_specs=[pl.BlockSpec((1, tm, width), tile) for width in _INPROJ_OUT_WIDTHS],
        out_shape=out_shape,
        compiler_params=params,
        name="inproj",
    )(x.reshape(bsz * seq, D_MODEL), g, w_main, w_dt)

    nout = len(outs)
    last_blk = lambda b: (b, lp // BLK - 1, 0)
    return pl.pallas_call(
        _inproj_meta_kernel,
        grid=(bsz,),
        in_specs=[_const_spec((BLK, D_MODEL))] + weights + [pl.BlockSpec(memory_space=pl.ANY)] * nout,
        out_specs=[pl.BlockSpec((1, BLK, width), last_blk) for width in _INPROJ_OUT_WIDTHS],
        out_shape=out_shape,
        input_output_aliases={4 + j: j for j in range(nout)},
        scratch_shapes=[pltpu.VMEM((1, BLK, width), dt)
                        for width, dt in zip(_INPROJ_OUT_WIDTHS, _INPROJ_OUT_DTYPES)],
        compiler_params=params,
        name="inproj_meta",
    )(hm, g, w_main, w_dt, *outs)


def _dot_f32_by_01(lhs, rhs, f32_side):
    x = lhs if f32_side == "lhs" else rhs
    p1 = x.astype(BF16)
    r1 = x - p1.astype(F32)
    p2 = r1.astype(BF16)
    p3 = (r1 - p2.astype(F32)).astype(BF16)
    out = None
    for piece in (p1, p2, p3):
        ops = (piece, rhs) if f32_side == "lhs" else (lhs, piece)
        term = jnp.dot(*ops, preferred_element_type=F32)
        out = term if out is None else out + term
    return out


def _ssd_kernel(xbc_ref, z_ref, dt_ref, cw_ref, cb_ref, dtb_ref, alog_ref, dexp_ref, g_ref, e_ref,
                out_ref, prev, state, *, nseq):
    c = pl.program_id(1)

    @pl.when(c == 0)
    def _():
        prev[...] = jnp.zeros_like(prev)
        state[...] = jnp.zeros_like(state)

    seqs = range(nseq)
    groups = range(SSD_GROUPS)
    row = lax.broadcasted_iota(jnp.int32, (BLK, 1), 0)
    real = jnp.logical_or(c > 0, row >= PAD)
    lane = lax.broadcasted_iota(jnp.int32, (BLK, DT_LANES), 1)
    dt_valid = jnp.logical_and(real, lane < SSD_HEADS)
    ri = lax.broadcasted_iota(jnp.int32, (BLK, BLK), 0)
    ci = lax.broadcasted_iota(jnp.int32, (BLK, BLK), 1)
    causal = ri >= ci
    tri = jnp.where(causal, 1.0, 0.0).astype(BF16)
    half = ci < SSD_HEAD_DIM
    cw = cw_ref[...]
    e = e_ref[...]
    a_neg = -jnp.exp(alog_ref[...]) * LOG2E

    xc, dt, adt = [], [], []
    for s in seqs:
        x_raw = xbc_ref[s].astype(F32)
        prev[s, 8:8 + BLK, :] = x_raw
        acc = cb_ref[...] + cw[3:4, :] * x_raw
        for tap in range(SSD_CONV - 1):
            acc = acc + cw[tap:tap + 1, :] * prev[s, 5 + tap:5 + tap + BLK, :]
        prev[s, 0:8, :] = x_raw[BLK - 8:, :]
        half_acc = 0.5 * acc
        xc.append(jnp.where(real, half_acc + half_acc * jnp.tanh(half_acc), 0.0))
        dts = jax.nn.softplus(dt_ref[s] + dtb_ref[...])
        dt.append(jnp.where(dt_valid, dts, 0.0))
        adt.append(dt[s] * a_neg)
    xs = [xc[s][:, :SSD_INNER] for s in seqs]

    acs = [_dot_f32_by_01(tri, adt[s], f32_side="rhs") for s in seqs]
    acs_t = [acs[s].T for s in seqs]
    dt_exp = [_dot_f32_by_01(dt[s], e, f32_side="lhs") for s in seqs]
    acs_exp = [_dot_f32_by_01(acs[s], e, f32_side="lhs") for s in seqs]

    xdt_b, xdec, off_scale, chunk_decay, b_t, c_gb, b_gb = [], [], [], [], [], [], []
    for s in seqs:
        last = acs_exp[s][BLK - 1:BLK, :]
        xdt = xs[s] * dt_exp[s]
        xdec.append((xdt * jnp.exp2(last - acs_exp[s])).astype(BF16))
        xdt_b.append(xdt.astype(BF16))
        off_scale.append(jnp.exp2(acs_exp[s]))
        chunk_decay.append(jnp.exp2(last))
        b_g = [xc[s][:, SSD_INNER + g * SSD_STATE:SSD_INNER + (g + 1) * SSD_STATE] for g in groups]
        c_g = [xc[s][:, SSD_INNER + (SSD_GROUPS + g) * SSD_STATE:
                     SSD_INNER + (SSD_GROUPS + g + 1) * SSD_STATE] for g in groups]
        b_gb.append([x.astype(BF16) for x in b_g])
        c_gb.append([x.astype(BF16) for x in c_g])
        b_t.append([x.T.astype(BF16) for x in b_g])

    gcols = [slice(g * GROUP_COLS, (g + 1) * GROUP_COLS) for g in groups]
    cb_mat = [[lax.dot_general(c_gb[s][g], b_gb[s][g], (((1,), (1,)), ((), ())),
                               preferred_element_type=F32) for g in groups] for s in seqs]
    s_prev = [[state[s, :, gcols[g]] for g in groups] for s in seqs]
    y_off = [[jnp.dot(c_gb[s][g], s_prev[s][g].astype(BF16), preferred_element_type=F32)
              for g in groups] for s in seqs]

    heads_per_group = SSD_HEADS // SSD_GROUPS
    decay_mats = [[None] * SSD_HEADS for _ in seqs]
    for s in seqs:
        for hh in range(SSD_HEADS):
            seg = acs[s][:, hh:hh + 1] - acs_t[s][hh:hh + 1, :]
            ldec = jnp.exp2(jnp.where(causal, seg, -1e30))
            decay_mats[s][hh] = (cb_mat[s][hh // heads_per_group] * ldec).astype(BF16)

    y_diag = [[jnp.dot(decay_mats[s][hh], xdt_b[s][:, (hh // 2) * BLK:(hh // 2 + 1) * BLK],
                       preferred_element_type=F32) for hh in range(SSD_HEADS)] for s in seqs]
    new = [[jnp.dot(b_t[s][g], xdec[s][:, gcols[g]], preferred_element_type=F32)
            for g in groups] for s in seqs]

    for s in seqs:
        y_cols = []
        for g in groups:
            state[s, :, gcols[g]] = s_prev[s][g] * chunk_decay[s][:, gcols[g]] + new[s][g]
            yo = y_off[s][g] * off_scale[s][:, gcols[g]]
            for pair in range(GROUP_COLS // BLK):
                h0 = (g * GROUP_COLS + pair * BLK) // SSD_HEAD_DIM
                y_cols.append(jnp.where(half, y_diag[s][h0], y_diag[s][h0 + 1])
                              + yo[:, pair * BLK:(pair + 1) * BLK])
        y = jnp.concatenate(y_cols, axis=1) + xs[s] * dexp_ref[...]
        y = y * z_ref[s].astype(F32)
        out_ref[s] = _rms(y, g_ref[...]).astype(BF16)


def _ssd(xbc, z, dt_raw, conv_w, conv_b, dt_bias, a_log, d_exp, norm_g, expand):
    bsz, lp, _ = xbc.shape
    nblk = lp // BLK
    nseq = SSD_SEQS_PER_STEP
    assert bsz % nseq == 0
    blk = lambda b, c: (b, (c + nblk - 1) % nblk, 0)
    return pl.pallas_call(
        functools.partial(_ssd_kernel, nseq=nseq),
        grid=(bsz // nseq, nblk),
        in_specs=[
            pl.BlockSpec((nseq, BLK, XBC_WIDTH), blk),
            pl.BlockSpec((nseq, BLK, SSD_INNER), blk),
            pl.BlockSpec((nseq, BLK, DT_LANES), blk),
            _const_spec((SSD_CONV, XBC_WIDTH)),
            _const_spec((1, XBC_WIDTH)),
            _const_spec((1, DT_LANES)),
            _const_spec((1, DT_LANES)),
            _const_spec((1, SSD_INNER)),
            _const_spec((1, SSD_INNER)),
            _const_spec((DT_LANES, SSD_INNER)),
        ],
        out_specs=pl.BlockSpec((nseq, BLK, SSD_INNER), blk),
        out_shape=jax.ShapeDtypeStruct((bsz, lp, SSD_INNER), BF16),
        scratch_shapes=[
            pltpu.VMEM((nseq, BLK + 8, XBC_WIDTH), F32),
            pltpu.VMEM((nseq, SSD_STATE, SSD_INNER), F32),
        ],
        compiler_params=pltpu.CompilerParams(
            dimension_semantics=("arbitrary", "arbitrary"), vmem_limit_bytes=V7X_VMEM_LIMIT),
        name="ssd",
    )(xbc, z, dt_raw, conv_w, conv_b, dt_bias, a_log, d_exp, norm_g, expand)


SB_DEAD_LOG2 = -151.0
SB_LEAD = 3
SB_TOP_ROWS = 32
FULL, TOP, BOT = slice(0, BLK), slice(0, SB_TOP_ROWS), slice(SB_TOP_ROWS, BLK)


def _sb_kv_copies(hbm_refs, bufs, sems, batch, part, nblk, qpb):
    nparts = nblk // qpb
    seq_rows = (nblk - 1) * BLK
    if part == nparts:
        rows = pl.ds(seq_rows, BLK)
    else:
        start = part * qpb * BLK
        rows = pl.ds(start, min(qpb * BLK, seq_rows - start))
    return [pltpu.make_async_copy(src.at[batch, rows, :], dst.at[rows, :], sems.at[a, part])
            for a, (src, dst) in enumerate(zip(hbm_refs, bufs))]


def _sb_kernel(q_ref, k_hbm, v0_hbm, v1_hbm, tt_ref, o_ref, k_ref, v0_ref, v1_ref, kv_sems,
               qm_scr, run_scr, acc_scr, worst_scr, *, nblk, npairs, qpb):
    batch, step = pl.program_id(0), pl.program_id(1)
    nparts = nblk // qpb
    hbm_refs, bufs = (k_hbm, v0_hbm, v1_hbm), (k_ref, v0_ref, v1_ref)

    @pl.when(step == 0)
    def _():
        for part in [nparts] + list(range(nparts)):
            for cp in _sb_kv_copies(hbm_refs, bufs, kv_sems, batch, part, nblk, qpb):
                cp.start()
        for cp in _sb_kv_copies(hbm_refs, bufs, kv_sems, batch, nparts, nblk, qpb):
            cp.wait()

    for part in range(nparts):
        @pl.when(step == part)
        def _(part=part):
            for cp in _sb_kv_copies(hbm_refs, bufs, kv_sems, batch, part, nblk, qpb):
                cp.wait()

    lane = lax.broadcasted_iota(jnp.int32, (BLK, BLK), 1)
    sub = lax.broadcasted_iota(jnp.int32, (BLK, BLK), 0)
    first = lane < SB_HEAD_DIM
    strictly_older = lane < sub
    tt = tt_ref[...]

    def query_block(r, carry):
        qi = (step * qpb + r + 1) % nblk
        rows = pl.ds(pl.multiple_of(r * BLK, BLK), BLK)
        for p in range(npairs):
            qp = q_ref[0, rows, p * BLK:(p + 1) * BLK]
            zero = jnp.zeros_like(qp)
            qm_scr[p, :BLK, :] = jnp.where(first, qp, zero)
            qm_scr[p, BLK:, :] = jnp.where(first, zero, qp)

        def process(blocks, init, check, row_sets=None):
            offs = [pl.multiple_of(((j + nblk - 1) % nblk) * BLK, BLK) for j in blocks]
            nb = range(len(blocks))
            row_sets = row_sets or [FULL] * len(blocks)
            heads = range(2 * npairs)
            scores = {}
            for b in nb:
                rs = row_sets[b]
                for p in range(npairs):
                    if rs is FULL:
                        lhs = qm_scr[p]
                    else:
                        lhs = jnp.concatenate([qm_scr[p, rs, :],
                                               qm_scr[p, BLK + rs.start:BLK + rs.stop, :]], axis=0)
                    scores[b, p] = lax.dot_general(
                        lhs, k_ref[pl.ds(offs[b], BLK), p * BLK:(p + 1) * BLK],
                        (((1,), (1,)), ((), ())), preferred_element_type=F32)
            log_betas, log_keeps, totals = {}, {}, {}
            for b in nb:
                n = row_sets[b].stop - row_sets[b].start
                for h in heads:
                    s = scores[b, h // 2][(h % 2) * n:(h % 2 + 1) * n, :]
                    neg_part = jnp.minimum(s, 0.0)
                    neg_pos = neg_part - s
                    sp = jnp.log(1.0 + jnp.exp2(neg_part + neg_pos)) * LOG2E
                    log_beta = neg_part - sp
                    log_keep = neg_pos - sp
                    if init and b == 0:
                        log_keep = jnp.where(strictly_older, log_keep, 0.0)
                    log_betas[b, h] = log_beta
                    log_keeps[b, h] = log_keep.astype(BF16)
                    totals[b, h] = jnp.sum(log_keep, axis=1, keepdims=True)
            sums = [jnp.dot(jnp.concatenate([log_keeps[b, h] for h in heads], axis=0), tt,
                            preferred_element_type=F32) for b in nb]
            weights = {}
            worst = None
            for h in heads:
                run = None if init else run_scr[h]
                for b in nb:
                    rs = row_sets[b]
                    n = rs.stop - rs.start
                    after, total = sums[b][h * n:(h + 1) * n, :], totals[b, h]
                    if run is None:
                        w = jnp.where(strictly_older, jnp.exp2(log_betas[b, h] + after), 0.0)
                        run = jnp.broadcast_to(total, (BLK, BLK))
                    else:
                        part = run if rs is FULL else run[rs, :]
                        w = jnp.exp2(log_betas[b, h] + (after + part))
                        part = part + total
                        if rs is FULL:
                            run = part
                        elif rs is TOP:
                            run = jnp.concatenate([part, run[BOT, :]], axis=0)
                        else:
                            run = jnp.concatenate([run[TOP, :], part], axis=0)
                    weights[b, h] = w.astype(BF16)
                run_scr[h] = run
                if check:
                    worst = run if worst is None else jnp.maximum(worst, run)
            for p in range(npairs):
                pv = None if init else acc_scr[p]
                for b in nb:
                    rs = row_sets[b]
                    v_heads = jnp.concatenate(
                        [v0_ref[pl.ds(offs[b], BLK), p * BLK:(p + 1) * BLK],
                         v1_ref[pl.ds(offs[b], BLK), p * BLK:(p + 1) * BLK]], axis=0)
                    w_heads = jnp.concatenate([weights[b, 2 * p], weights[b, 2 * p + 1]], axis=1)
                    contrib = jnp.dot(w_heads, v_heads, preferred_element_type=F32)
                    if pv is None:
                        pv = contrib
                    elif rs is FULL:
                        pv = pv + contrib
                    elif rs is TOP:
                        pv = jnp.concatenate([pv[TOP, :] + contrib, pv[BOT, :]], axis=0)
                    else:
                        pv = jnp.concatenate([pv[TOP, :], pv[BOT, :] + contrib], axis=0)
                acc_scr[p] = pv
            if check:
                worst_scr[0] = jnp.max(worst[TOP, :])
                worst_scr[1] = jnp.max(worst[BOT, :])

        for lead in range(1, SB_LEAD + 1):
            last = lead == SB_LEAD

            @pl.when(qi >= lead - 1 if last else qi == lead - 1)
            def _(lead=lead, last=last):
                if last:
                    process([qi - i for i in range(lead)], True, True, [FULL] * (lead - 1) + [TOP])
                else:
                    process([qi - i for i in range(lead)], True, False)
                    worst_scr[0] = jnp.float32(SB_DEAD_LOG2)
                    worst_scr[1] = jnp.float32(SB_DEAD_LOG2)

        @pl.when(jnp.logical_and(qi >= SB_LEAD - 1, worst_scr[1] > SB_DEAD_LOG2))
        def _():
            process([qi - (SB_LEAD - 1)], False, True, [BOT])

        def cond(st):
            return jnp.logical_and(st[0] >= 0, st[1] > SB_DEAD_LOG2)

        def body(st):
            process([st[0]], False, True)
            return st[0] - 1, jnp.maximum(worst_scr[0], worst_scr[1])

        lax.while_loop(cond, body, (qi - SB_LEAD, jnp.maximum(worst_scr[0], worst_scr[1])))
        for p in range(npairs):
            o_ref[0, rows, p * BLK:(p + 1) * BLK] = acc_scr[p].astype(BF16)
        return carry

    lax.fori_loop(0, qpb, query_block, 0)


def _sb_attention(q, k, v0, v1, tt):
    bsz, lp, width = q.shape
    nblk = lp // BLK
    qpb = SB_Q_BLOCKS_PER_STEP
    assert nblk % qpb == 0
    npairs = width // BLK
    nparts = nblk // qpb
    qblk = lambda b, g: (b, g, 0)
    kv_hbm = pl.BlockSpec(memory_space=pl.ANY)
    return pl.pallas_call(
        functools.partial(_sb_kernel, nblk=nblk, npairs=npairs, qpb=qpb),
        grid=(bsz, nparts),
        in_specs=[
            pl.BlockSpec((1, qpb * BLK, width), qblk),
            kv_hbm,
            kv_hbm,
            kv_hbm,
            _const_spec((BLK, BLK)),
        ],
        out_specs=pl.BlockSpec((1, qpb * BLK, width), qblk),
        out_shape=jax.ShapeDtypeStruct((bsz, lp, width), BF16),
        scratch_shapes=[
            pltpu.VMEM((lp, width), BF16),
            pltpu.VMEM((lp, width), BF16),
            pltpu.VMEM((lp, width), BF16),
            pltpu.SemaphoreType.DMA((3, nparts + 1)),
            pltpu.VMEM((npairs, 2 * BLK, BLK), BF16),
            pltpu.VMEM((2 * npairs, BLK, BLK), F32),
            pltpu.VMEM((npairs, BLK, BLK), F32),
            pltpu.SMEM((2,), F32),
        ],
        compiler_params=pltpu.CompilerParams(
            dimension_semantics=("arbitrary", "arbitrary"), vmem_limit_bytes=V7X_VMEM_LIMIT),
        name="sb_attention",
    )(q, k, v0, v1, tt)


def _outproj_rows(ys, o, h, wa_ref, wb_ref, gsb_ref, gpost_ref, gffn_ref):
    osb = _rms(o.astype(F32), gsb_ref[...]).astype(BF16)
    mix = jnp.dot(ys, wa_ref[...], preferred_element_type=F32)
    mix = mix + jnp.dot(osb, wb_ref[...], preferred_element_type=F32)
    h1 = h + _rms(mix, gpost_ref[...])
    return h1, _rms(h1, gffn_ref[...]).astype(BF16)


def _outproj_kernel(ys_ref, o_ref, h_ref, wa_ref, wb_ref, gsb_ref, gpost_ref, gffn_ref,
                    h1_ref, xn2_ref, *, nsub):
    sub = h_ref.shape[1] // nsub
    rows = [pl.ds(k * sub, sub) for k in range(nsub)]
    osb = [_rms(o_ref[0, r, :].astype(F32), gsb_ref[...]).astype(BF16) for r in rows]
    mix = [jnp.dot(ys_ref[0, r, :], wa_ref[...], preferred_element_type=F32)
           + jnp.dot(osb[k], wb_ref[...], preferred_element_type=F32) for k, r in enumerate(rows)]
    for k, r in enumerate(rows):
        h1 = h_ref[0, r, :] + _rms(mix[k], gpost_ref[...])
        h1_ref[0, r, :] = h1
        xn2_ref[0, r, :] = _rms(h1, gffn_ref[...]).astype(BF16)


def _outproj_meta_kernel(ys_ref, o_ref, hm_ref, wa_ref, wb_ref, gsb_ref, gpost_ref, gffn_ref, xn2_ref):
    _, xn2 = _outproj_rows(ys_ref[0], o_ref[0], hm_ref[...], wa_ref, wb_ref, gsb_ref, gpost_ref, gffn_ref)
    xn2_ref[0] = xn2


def _outproj(y_ssd, o_sb, x, hm, w_a, w_b, g_sb, g_post, g_ffn):
    bsz, seq, _ = x.shape
    lp = y_ssd.shape[1]
    tm = OUTPROJ_TM
    per_seq = seq // tm
    weights = [
        _const_spec((SSD_INNER, D_MODEL)),
        _const_spec((SB_WIDTH, D_MODEL)),
        _const_spec((1, SB_WIDTH)),
        _const_spec((1, D_MODEL)),
        _const_spec((1, D_MODEL)),
    ]
    params = pltpu.CompilerParams(dimension_semantics=("arbitrary",), vmem_limit_bytes=V7X_VMEM_LIMIT)
    tile = lambda t: (t // per_seq, t % per_seq, 0)
    h1, xn2 = pl.pallas_call(
        functools.partial(_outproj_kernel, nsub=OUTPROJ_SUBTILES),
        grid=(bsz * per_seq,),
        in_specs=[
            pl.BlockSpec((1, tm, SSD_INNER), tile),
            pl.BlockSpec((1, tm, SB_WIDTH), tile),
            pl.BlockSpec((1, tm, D_MODEL), tile),
        ] + weights,
        out_specs=[pl.BlockSpec((1, tm, D_MODEL), tile), pl.BlockSpec((1, tm, D_MODEL), tile)],
        out_shape=[jax.ShapeDtypeStruct((bsz, seq, D_MODEL), F32),
                   jax.ShapeDtypeStruct((bsz, seq, D_MODEL), BF16)],
        compiler_params=params,
        name="outproj",
    )(y_ssd, o_sb, x, w_a, w_b, g_sb, g_post, g_ffn)

    last_blk = lambda b: (0, lp // BLK - 1, 0)
    xn2_meta = pl.pallas_call(
        _outproj_meta_kernel,
        grid=(1,),
        in_specs=[
            pl.BlockSpec((1, BLK, SSD_INNER), last_blk),
            pl.BlockSpec((1, BLK, SB_WIDTH), last_blk),
            _const_spec((BLK, D_MODEL)),
        ] + weights,
        out_specs=pl.BlockSpec((1, BLK, D_MODEL), lambda b: (0, 0, 0)),
        out_shape=jax.ShapeDtypeStruct((1, BLK, D_MODEL), BF16),
        compiler_params=params,
        name="outproj_meta",
    )(y_ssd, o_sb, hm, w_a, w_b, g_sb, g_post, g_ffn)
    return h1, xn2, xn2_meta


def _gelu_tanh(x):
    return 0.5 * x * (1.0 + jnp.tanh(math.sqrt(2.0 / math.pi) * (x + 0.044715 * (x * x * x))))


def _ffn_kernel(xn_ref, xm_ref, h1_ref, wup_ref, cw_ref, cb_ref, wd_ref, g_ref, out_ref,
                gbuf0, gbuf1, ubuf0, ubuf1, halo, act_scr, *, tm, tf):
    nch = D_FF // tf
    i = pl.program_id(1)
    gbufs, ubufs = (gbuf0, gbuf1), (ubuf0, ubuf1)
    cols = [slice(c * tf, (c + 1) * tf) for c in range(nch)]
    up_cols = [slice(D_FF + c * tf, D_FF + (c + 1) * tf) for c in range(nch)]

    @pl.when(i == 0)
    def _():
        for c in range(nch):
            gm = jnp.dot(xm_ref[0], wup_ref[:, cols[c]], preferred_element_type=F32)
            halo[c] = gm[N_META - 8:, :]

    def project(c, slot):
        gbuf, ubuf = gbufs[slot], ubufs[slot]
        xn = xn_ref[0]
        gate = jnp.dot(xn, wup_ref[:, cols[c]], preferred_element_type=F32)
        gbuf[0:8, :] = halo[c]
        gbuf[8:8 + tm, :] = gate
        halo[c] = gate[tm - 8:, :]
        ubuf[...] = jnp.dot(xn, wup_ref[:, up_cols[c]], preferred_element_type=F32)

    def activate(c, slot):
        gbuf, ubuf = gbufs[slot], ubufs[slot]
        cw = cw_ref[:, cols[c]]
        conv = cb_ref[:, cols[c]] + cw[2:3, :] * gbuf[8:8 + tm, :]
        conv = conv + cw[1:2, :] * gbuf[7:7 + tm, :] + cw[0:1, :] * gbuf[6:6 + tm, :]
        act_scr[c] = (_gelu_tanh(conv) * ubuf[...]).astype(BF16)

    project(0, 0)
    for c in range(nch - 1):
        project(c + 1, (c + 1) % 2)
        activate(c, c % 2)
    activate(nch - 1, (nch - 1) % 2)
    act = jnp.concatenate([act_scr[c] for c in range(nch)], axis=1)
    down = jnp.dot(act, wd_ref[...], preferred_element_type=F32)
    out_ref[0] = h1_ref[0] + _rms(down, g_ref[...])


def _ffn(xn2, xn2_meta, h1, w_up, conv_w, conv_b, w_down, g_post):
    bsz, seq, _ = xn2.shape
    tm, tf = FFN_TM, FFN_TF
    nch = D_FF // tf
    xblk = lambda b, i: (b, i, 0)
    meta = lambda b, i: (0, BLK // N_META - 1, 0)
    return pl.pallas_call(
        functools.partial(_ffn_kernel, tm=tm, tf=tf),
        grid=(bsz, seq // tm),
        in_specs=[
            pl.BlockSpec((1, tm, D_MODEL), xblk),
            pl.BlockSpec((1, N_META, D_MODEL), meta),
            pl.BlockSpec((1, tm, D_MODEL), xblk),
            _const_spec((D_MODEL, 2 * D_FF)),
            _const_spec((FFN_CONV, D_FF)),
            _const_spec((1, D_FF)),
            _const_spec((D_FF, D_MODEL)),
            _const_spec((1, D_MODEL)),
        ],
        out_specs=pl.BlockSpec((1, tm, D_MODEL), xblk),
        out_shape=jax.ShapeDtypeStruct((bsz, seq, D_MODEL), F32),
        scratch_shapes=[
            pltpu.VMEM((tm + 8, tf), F32),
            pltpu.VMEM((tm + 8, tf), F32),
            pltpu.VMEM((tm, tf), F32),
            pltpu.VMEM((tm, tf), F32),
            pltpu.VMEM((nch, 8, tf), F32),
            pltpu.VMEM((nch, tm, tf), BF16),
        ],
        compiler_params=pltpu.CompilerParams(
            dimension_semantics=("arbitrary", "arbitrary"), vmem_limit_bytes=V7X_VMEM_LIMIT),
        name="ffn",
    )(xn2, xn2_meta, h1, w_up, conv_w, conv_b, w_down, g_post)


def _pad_lanes(v, n):
    return jnp.pad(v, ((0, 0), (0, n - v.shape[-1])))


def kernel(x, meta_tokens, mix_pre_g, w_in, ssd_conv_w, ssd_conv_b, ssd_dt_bias, ssd_a_log, ssd_d,
           ssd_norm_g, sb_norm_g, w_out, mix_post_g, ffn_pre_g, w_up, ffn_conv_w, ffn_conv_b,
           w_down, ffn_post_g):
    bsz, seq, d = x.shape
    depth = w_in.shape[0]
    assert depth == 1 and d == D_MODEL and seq % OUTPROJ_TM == 0

    hm = jnp.concatenate([jnp.zeros((PAD, d), x.dtype), meta_tokens.astype(x.dtype)], axis=0)

    l = 0
    off_xbc = SSD_INNER
    off_dt = off_xbc + XBC_WIDTH
    off_q = off_dt + SSD_HEADS
    w = w_in[l]
    w_main = jnp.concatenate([w[:, :off_dt], w[:, off_q:]], axis=1).astype(BF16)
    w_dt = _pad_lanes(w[:, off_dt:off_q], DT_LANES).astype(BF16)
    z, xbc, q, k, v0, v1, dt_raw = _inproj(x, hm, mix_pre_g[l][None], w_main, w_dt)

    head_of_col = jnp.arange(SSD_INNER) // SSD_HEAD_DIM
    expand = (jnp.arange(DT_LANES)[:, None] == head_of_col[None, :]).astype(BF16)
    d_exp = jnp.repeat(ssd_d[l].astype(F32), SSD_HEAD_DIM)[None]
    y_ssd = _ssd(xbc, z, dt_raw, ssd_conv_w[l], ssd_conv_b[l][None],
                 _pad_lanes(ssd_dt_bias[l][None], DT_LANES), _pad_lanes(ssd_a_log[l][None], DT_LANES),
                 d_exp, ssd_norm_g[l][None], expand)

    kk = jnp.arange(BLK)
    tt = (kk[:, None] > kk[None, :]).astype(BF16)
    o_sb = _sb_attention(q, k, v0, v1, tt)

    wo = w_out[l].astype(BF16)
    h1, xn2, xn2_meta = _outproj(y_ssd, o_sb, x, hm, wo[:SSD_INNER], wo[SSD_INNER:], sb_norm_g[l][None],
                                 mix_post_g[l][None], ffn_pre_g[l][None])

    return _ffn(xn2, xn2_meta, h1, w_up[l].astype(BF16), ffn_conv_w[l], ffn_conv_b[l][None],
                w_down[l].astype(BF16), ffn_post_g[l][None])
```
